```python
import math
import jax, jax.numpy as jnp
from jax import lax
import numpy as np

D_MODEL = 4096
BATCH = 8
SEQ = 4096
DEPTH = 1

D_MIX = D_MODEL
D_ATTN = D_MIX // 2
D_SSM = D_MIX - D_ATTN
HEAD_DIM = 64
N_Q_HEADS = D_ATTN // HEAD_DIM
N_KV_HEADS = max(1, N_Q_HEADS // 8)
Q_PER_KV = N_Q_HEADS // N_KV_HEADS
D_KV = N_KV_HEADS * HEAD_DIM
WINDOW = 128
BLOCK = WINDOW
ROPE_THETA = 10000.0
SSM_GROUP = 16
N_SSM_GROUPS = D_SSM // SSM_GROUP
STATE = 64
DT_MIN = 1e-3
DT_MAX = 1e-1
D_FF = 4 * D_MODEL
N_MOD = 6
EPS = 1e-6
D_IN = D_ATTN + 2 * D_KV + D_SSM

kernel_name = "hymba_s5_swa_sink_adaln_block"


def rmsnorm(x, g):
    xf = x.astype(jnp.float32)
    y = xf * lax.rsqrt(jnp.mean(xf * xf, axis=-1, keepdims=True) + EPS)
    return (y * g.astype(jnp.float32)).astype(x.dtype)


def rope(x):
    s = x.shape[1]
    half = x.shape[-1] // 2
    inv_freq = ROPE_THETA ** (-jnp.arange(half, dtype=jnp.float32) / half)
    ang = jnp.arange(s, dtype=jnp.float32)[:, None] * inv_freq[None, :]
    cos = jnp.cos(ang)[None, :, None, :]
    sin = jnp.sin(ang)[None, :, None, :]
    xf = x.astype(jnp.float32)
    x1, x2 = xf[..., :half], xf[..., half:]
    out = jnp.concatenate([x1 * cos - x2 * sin, x2 * cos + x1 * sin], axis=-1)
    return out.astype(x.dtype)


def sliding_window_attention(q, k, v, sinks):
    b, s = q.shape[0], q.shape[1]
    nb = s // BLOCK
    qb = q.reshape(b, nb, BLOCK, N_KV_HEADS, Q_PER_KV, HEAD_DIM).astype(jnp.float32)
    kb = k.reshape(b, nb, BLOCK, N_KV_HEADS, HEAD_DIM).astype(jnp.float32)
    vb = v.reshape(b, nb, BLOCK, N_KV_HEADS, HEAD_DIM).astype(jnp.float32)
    kk = jnp.concatenate([jnp.concatenate([jnp.zeros_like(kb[:, :1]), kb[:, :-1]], axis=1), kb], axis=2)
    vv = jnp.concatenate([jnp.concatenate([jnp.zeros_like(vb[:, :1]), vb[:, :-1]], axis=1), vb], axis=2)
    scores = jnp.einsum('bnqhgd,bnkhd->bnhgqk', qb, kk) * (HEAD_DIM ** -0.5)
    qi = jnp.arange(BLOCK)[:, None] + BLOCK
    kj = jnp.arange(2 * BLOCK)[None, :]
    rel = qi - kj
    band = (rel >= 0) & (rel < WINDOW)
    key_ok = (jnp.arange(nb)[:, None] > 0) | (jnp.arange(2 * BLOCK)[None, :] >= BLOCK)
    mask = band[None, :, :] & key_ok[:, None, :]
    scores = jnp.where(mask[None, :, None, None], scores, jnp.float32(-1e30))
    sink = sinks.astype(jnp.float32).reshape(N_KV_HEADS, Q_PER_KV)[None, None, :, :, None, None]
    m = jnp.maximum(jnp.max(scores, axis=-1, keepdims=True), sink)
    p = jnp.exp(scores - m)
    probs = p / (jnp.sum(p, axis=-1, keepdims=True) + jnp.exp(sink - m))
    out = jnp.einsum('bnhgqk,bnkhd->bnqhgd', probs, vv)
    return out.reshape(b, s, N_Q_HEADS * HEAD_DIM).astype(q.dtype)


def _scan_op(e1, e2):
    a1, b1 = e1
    a2, b2 = e2
    return a2 * a1, a2 * b1 + b2


def s5_mixer(u, lam_re, lam_im, log_step, b_re, b_im, c_re, c_im, d_skip, w_glu, b_glu):
    bsz, s = u.shape[0], u.shape[1]
    f32 = jnp.float32
    uf = u.astype(f32).reshape(bsz, s, N_SSM_GROUPS, SSM_GROUP)
    step = jnp.exp(log_step.astype(f32))[:, None]
    lam = lax.complex(lam_re.astype(f32), lam_im.astype(f32))
    lam_bar = jnp.exp(lam * step)
    coef = (lam_bar - 1.0) / lam
    b_bar = coef[..., None] * lax.complex(b_re.astype(f32), b_im.astype(f32))
    bu = jnp.einsum('bsgh,gph->sbgp', uf.astype(jnp.complex64), b_bar)
    a = jnp.broadcast_to(lam_bar[None, None], (s, 1, N_SSM_GROUPS, STATE))
    _, states = lax.associative_scan(_scan_op, (a, bu), axis=0)
    c_mat = lax.complex(c_re.astype(f32), c_im.astype(f32))
    y = jnp.real(jnp.einsum('sbgp,ghp->bsgh', states, c_mat))
    y = y + d_skip.astype(f32).reshape(N_SSM_GROUPS, SSM_GROUP) * uf
    y = jax.nn.gelu(y.reshape(bsz, s, D_SSM), approximate=False)
    out = y * jax.nn.sigmoid(y @ w_glu.astype(f32) + b_glu.astype(f32))
    return out.astype(u.dtype)


def _fwd_setup_inputs(seed: int = 0) -> dict:
    key = jax.random.key(seed)
    ks = jax.random.split(key, 24)
    f32 = jnp.float32
    nrm = lambda k, shape, sc: jax.random.normal(k, shape, f32) * sc
    inputs = {
        "x": nrm(ks[0], (BATCH, SEQ, D_MODEL), 1.0),
        "c": nrm(ks[1], (BATCH, D_MODEL), 1.0),
        "w_ada": nrm(ks[2], (DEPTH, D_MODEL, N_MOD * D_MODEL), 0.5 * D_MODEL ** -0.5),
        "b_ada": nrm(ks[3], (DEPTH, N_MOD * D_MODEL), 0.01),
        "norm1_g": 1.0 + nrm(ks[4], (DEPTH, D_MODEL), 0.02),
        "w_in": nrm(ks[5], (DEPTH, D_MODEL, D_IN), D_MODEL ** -0.5),
        "sinks": nrm(ks[6], (DEPTH, N_Q_HEADS), 0.5),
        "ssm_lam_re": -0.5 + nrm(ks[7], (DEPTH, N_SSM_GROUPS, STATE), 0.01),
        "ssm_lam_im": jnp.pi * jnp.arange(STATE, dtype=f32)[None, None, :] + nrm(ks[8], (DEPTH, N_SSM_GROUPS, STATE), 0.01),
        "ssm_log_step": jax.random.uniform(ks[9], (DEPTH, N_SSM_GROUPS), f32, math.log(DT_MIN), math.log(DT_MAX)),
        "ssm_b_re": nrm(ks[10], (DEPTH, N_SSM_GROUPS, STATE, SSM_GROUP), (2 * SSM_GROUP) ** -0.5),
        "ssm_b_im": nrm(ks[11], (DEPTH, N_SSM_GROUPS, STATE, SSM_GROUP), (2 * SSM_GROUP) ** -0.5),
        "ssm_c_re": nrm(ks[12], (DEPTH, N_SSM_GROUPS, SSM_GROUP, STATE), (2 * STATE) ** -0.5),
        "ssm_c_im": nrm(ks[13], (DEPTH, N_SSM_GROUPS, SSM_GROUP, STATE), (2 * STATE) ** -0.5),
        "ssm_d": nrm(ks[14], (DEPTH, D_SSM), 1.0),
        "w_glu": nrm(ks[15], (DEPTH, D_SSM, D_SSM), D_SSM ** -0.5),
        "b_glu": nrm(ks[16], (DEPTH, D_SSM), 0.01),
        "attn_out_g": 1.0 + nrm(ks[17], (DEPTH, D_ATTN), 0.02),
        "ssm_out_g": 1.0 + nrm(ks[18], (DEPTH, D_SSM), 0.02),
        "w_out": nrm(ks[19], (DEPTH, D_MIX, D_MODEL), D_MIX ** -0.5),
        "norm2_g": 1.0 + nrm(ks[20], (DEPTH, D_MODEL), 0.02),
        "w_ff1": nrm(ks[21], (DEPTH, D_MODEL, D_FF), D_MODEL ** -0.5),
        "w_ff2": nrm(ks[22], (DEPTH, D_FF, D_MODEL), D_FF ** -0.5),
        "final_g": 1.0 + nrm(ks[23], (D_MODEL,), 0.02),
    }
    return inputs


def _fwd_reference(x, c, w_ada, b_ada, norm1_g, w_in, sinks, ssm_lam_re, ssm_lam_im, ssm_log_step,
              ssm_b_re, ssm_b_im, ssm_c_re, ssm_c_im, ssm_d, w_glu, b_glu, attn_out_g, ssm_out_g,
              w_out, norm2_g, w_ff1, w_ff2, final_g):
    bsz, s, _ = x.shape
    c_act = jax.nn.silu(c.astype(jnp.float32))
    for l in range(DEPTH):
        mod = (c_act @ w_ada[l].astype(jnp.float32) + b_ada[l].astype(jnp.float32)).astype(x.dtype)
        shift1, scale1, gate1, shift2, scale2, gate2 = [m[:, None, :] for m in jnp.split(mod, N_MOD, axis=-1)]

        h = rmsnorm(x, norm1_g[l]) * (1.0 + scale1) + shift1
        proj = h @ w_in[l]
        q = proj[..., :D_ATTN].reshape(bsz, s, N_Q_HEADS, HEAD_DIM)
        k = proj[..., D_ATTN:D_ATTN + D_KV].reshape(bsz, s, N_KV_HEADS, HEAD_DIM)
        v = proj[..., D_ATTN + D_KV:D_ATTN + 2 * D_KV].reshape(bsz, s, N_KV_HEADS, HEAD_DIM)
        u = proj[..., D_ATTN + 2 * D_KV:]
        attn = sliding_window_attention(rope(q), rope(k), v, sinks[l])
        ssm = s5_mixer(u, ssm_lam_re[l], ssm_lam_im[l], ssm_log_step[l], ssm_b_re[l], ssm_b_im[l],
                       ssm_c_re[l], ssm_c_im[l], ssm_d[l], w_glu[l], b_glu[l])
        mixed = jnp.concatenate([rmsnorm(attn, attn_out_g[l]), rmsnorm(ssm, ssm_out_g[l])], axis=-1)
        x = x + gate1 * (mixed @ w_out[l])

        h2 = rmsnorm(x, norm2_g[l]) * (1.0 + scale2) + shift2
        ff = jnp.square(jax.nn.relu(h2 @ w_ff1[l])) @ w_ff2[l]
        x = x + gate2 * ff
    return rmsnorm(x, final_g)


import jax as _jax
import jax.numpy as _jnp

TWIN_FORMAT = 'train_step'
FWD_PARAMS = ['x', 'c', 'w_ada', 'b_ada', 'norm1_g', 'w_in', 'sinks', 'ssm_lam_re', 'ssm_lam_im', 'ssm_log_step', 'ssm_b_re', 'ssm_b_im', 'ssm_c_re', 'ssm_c_im', 'ssm_d', 'w_glu', 'b_glu', 'attn_out_g', 'ssm_out_g', 'w_out', 'norm2_g', 'w_ff1', 'w_ff2', 'final_g']
TWIN_WEIGHTS = ['w_ada', 'b_ada', 'norm1_g', 'w_in', 'sinks', 'ssm_lam_re', 'ssm_lam_im', 'ssm_log_step', 'ssm_b_re', 'ssm_b_im', 'ssm_c_re', 'ssm_c_im', 'ssm_d', 'w_glu', 'b_glu', 'attn_out_g', 'ssm_out_g', 'w_out', 'norm2_g', 'w_ff1', 'w_ff2', 'final_g']
TWIN_DIFF_INPUT = 'x'
TWIN_INPUTS = ['x', 'c', 'w_ada', 'b_ada', 'norm1_g', 'w_in', 'sinks', 'ssm_lam_re', 'ssm_lam_im', 'ssm_log_step', 'ssm_b_re', 'ssm_b_im', 'ssm_c_re', 'ssm_c_im', 'ssm_d', 'w_glu', 'b_glu', 'attn_out_g', 'ssm_out_g', 'w_out', 'norm2_g', 'w_ff1', 'w_ff2', 'final_g', 'loss_target', 'm_w_ada', 'm_b_ada', 'm_norm1_g', 'm_w_in', 'm_sinks', 'm_ssm_lam_re', 'm_ssm_lam_im', 'm_ssm_log_step', 'm_ssm_b_re', 'm_ssm_b_im', 'm_ssm_c_re', 'm_ssm_c_im', 'm_ssm_d', 'm_w_glu', 'm_b_glu', 'm_attn_out_g', 'm_ssm_out_g', 'm_w_out', 'm_norm2_g', 'm_w_ff1', 'm_w_ff2', 'm_final_g', 'v_w_ada', 'v_b_ada', 'v_norm1_g', 'v_w_in', 'v_sinks', 'v_ssm_lam_re', 'v_ssm_lam_im', 'v_ssm_log_step', 'v_ssm_b_re', 'v_ssm_b_im', 'v_ssm_c_re', 'v_ssm_c_im', 'v_ssm_d', 'v_w_glu', 'v_b_glu', 'v_attn_out_g', 'v_ssm_out_g', 'v_w_out', 'v_norm2_g', 'v_w_ff1', 'v_w_ff2', 'v_final_g']
TWIN_OUTPUTS = ['loss', 'grad_x', 'grad_w_ada', 'grad_b_ada', 'grad_norm1_g', 'grad_w_in', 'grad_sinks', 'grad_ssm_lam_re', 'grad_ssm_lam_im', 'grad_ssm_log_step', 'grad_ssm_b_re', 'grad_ssm_b_im', 'grad_ssm_c_re', 'grad_ssm_c_im', 'grad_ssm_d', 'grad_w_glu', 'grad_b_glu', 'grad_attn_out_g', 'grad_ssm_out_g', 'grad_w_out', 'grad_norm2_g', 'grad_w_ff1', 'grad_w_ff2', 'grad_final_g', 'delta_w_ada', 'delta_b_ada', 'delta_norm1_g', 'delta_w_in', 'delta_sinks', 'delta_ssm_lam_re', 'delta_ssm_lam_im', 'delta_ssm_log_step', 'delta_ssm_b_re', 'delta_ssm_b_im', 'delta_ssm_c_re', 'delta_ssm_c_im', 'delta_ssm_d', 'delta_w_glu', 'delta_b_glu', 'delta_attn_out_g', 'delta_ssm_out_g', 'delta_w_out', 'delta_norm2_g', 'delta_w_ff1', 'delta_w_ff2', 'delta_final_g', 'new_m_w_ada', 'new_m_b_ada', 'new_m_norm1_g', 'new_m_w_in', 'new_m_sinks', 'new_m_ssm_lam_re', 'new_m_ssm_lam_im', 'new_m_ssm_log_step', 'new_m_ssm_b_re', 'new_m_ssm_b_im', 'new_m_ssm_c_re', 'new_m_ssm_c_im', 'new_m_ssm_d', 'new_m_w_glu', 'new_m_b_glu', 'new_m_attn_out_g', 'new_m_ssm_out_g', 'new_m_w_out', 'new_m_norm2_g', 'new_m_w_ff1', 'new_m_w_ff2', 'new_m_final_g', 'new_v_w_ada', 'new_v_b_ada', 'new_v_norm1_g', 'new_v_w_in', 'new_v_sinks', 'new_v_ssm_lam_re', 'new_v_ssm_lam_im', 'new_v_ssm_log_step', 'new_v_ssm_b_re', 'new_v_ssm_b_im', 'new_v_ssm_c_re', 'new_v_ssm_c_im', 'new_v_ssm_d', 'new_v_w_glu', 'new_v_b_glu', 'new_v_attn_out_g', 'new_v_ssm_out_g', 'new_v_w_out', 'new_v_norm2_g', 'new_v_w_ff1', 'new_v_w_ff2', 'new_v_final_g']
TWIN_LEAF_KINDS = {'loss': 'loss', 'grad_x': 'grad_x', 'grad_w_ada': 'grad_w', 'grad_b_ada': 'grad_w', 'grad_norm1_g': 'grad_w', 'grad_w_in': 'grad_w', 'grad_sinks': 'grad_w', 'grad_ssm_lam_re': 'grad_w', 'grad_ssm_lam_im': 'grad_w', 'grad_ssm_log_step': 'grad_w', 'grad_ssm_b_re': 'grad_w', 'grad_ssm_b_im': 'grad_w', 'grad_ssm_c_re': 'grad_w', 'grad_ssm_c_im': 'grad_w', 'grad_ssm_d': 'grad_w', 'grad_w_glu': 'grad_w', 'grad_b_glu': 'grad_w', 'grad_attn_out_g': 'grad_w', 'grad_ssm_out_g': 'grad_w', 'grad_w_out': 'grad_w', 'grad_norm2_g': 'grad_w', 'grad_w_ff1': 'grad_w', 'grad_w_ff2': 'grad_w', 'grad_final_g': 'grad_w', 'delta_w_ada': 'delta_w', 'delta_b_ada': 'delta_w', 'delta_norm1_g': 'delta_w', 'delta_w_in': 'delta_w', 'delta_sinks': 'delta_w', 'delta_ssm_lam_re': 'delta_w', 'delta_ssm_lam_im': 'delta_w', 'delta_ssm_log_step': 'delta_w', 'delta_ssm_b_re': 'delta_w', 'delta_ssm_b_im': 'delta_w', 'delta_ssm_c_re': 'delta_w', 'delta_ssm_c_im': 'delta_w', 'delta_ssm_d': 'delta_w', 'delta_w_glu': 'delta_w', 'delta_b_glu': 'delta_w', 'delta_attn_out_g': 'delta_w', 'delta_ssm_out_g': 'delta_w', 'delta_w_out': 'delta_w', 'delta_norm2_g': 'delta_w', 'delta_w_ff1': 'delta_w', 'delta_w_ff2': 'delta_w', 'delta_final_g': 'delta_w', 'new_m_w_ada': 'new_m', 'new_m_b_ada': 'new_m', 'new_m_norm1_g': 'new_m', 'new_m_w_in': 'new_m', 'new_m_sinks': 'new_m', 'new_m_ssm_lam_re': 'new_m', 'new_m_ssm_lam_im': 'new_m', 'new_m_ssm_log_step': 'new_m', 'new_m_ssm_b_re': 'new_m', 'new_m_ssm_b_im': 'new_m', 'new_m_ssm_c_re': 'new_m', 'new_m_ssm_c_im': 'new_m', 'new_m_ssm_d': 'new_m', 'new_m_w_glu': 'new_m', 'new_m_b_glu': 'new_m', 'new_m_attn_out_g': 'new_m', 'new_m_ssm_out_g': 'new_m', 'new_m_w_out': 'new_m', 'new_m_norm2_g': 'new_m', 'new_m_w_ff1': 'new_m', 'new_m_w_ff2': 'new_m', 'new_m_final_g': 'new_m', 'new_v_w_ada': 'new_v', 'new_v_b_ada': 'new_v', 'new_v_norm1_g': 'new_v', 'new_v_w_in': 'new_v', 'new_v_sinks': 'new_v', 'new_v_ssm_lam_re': 'new_v', 'new_v_ssm_lam_im': 'new_v', 'new_v_ssm_log_step': 'new_v', 'new_v_ssm_b_re': 'new_v', 'new_v_ssm_b_im': 'new_v', 'new_v_ssm_c_re': 'new_v', 'new_v_ssm_c_im': 'new_v', 'new_v_ssm_d': 'new_v', 'new_v_w_glu': 'new_v', 'new_v_b_glu': 'new_v', 'new_v_attn_out_g': 'new_v', 'new_v_ssm_out_g': 'new_v', 'new_v_w_out': 'new_v', 'new_v_norm2_g': 'new_v', 'new_v_w_ff1': 'new_v', 'new_v_w_ff2': 'new_v', 'new_v_final_g': 'new_v'}


def _forward(args):
    return _fwd_reference(*[args[k] for k in FWD_PARAMS])


def _output_shape():
    out = _jax.eval_shape(lambda: _forward(_fwd_setup_inputs(0)))
    return out.shape, out.dtype

N_MICROBATCH = 1
ADAM_LR = 0.001
ADAM_B1 = 0.9
ADAM_B2 = 0.999
ADAM_EPS = 1e-08
ADAM_WD = 0.01
ADAM_STEP = 10
PER_EXAMPLE_BATCH_AXIS = {'x': 0, 'c': 0, 'loss_target': 0}
SHARED_INPUTS = []
_WEIGHT_DTYPES = {'w_ada': _jnp.float32, 'b_ada': _jnp.float32, 'norm1_g': _jnp.float32, 'w_in': _jnp.float32, 'sinks': _jnp.float32, 'ssm_lam_re': _jnp.float32, 'ssm_lam_im': _jnp.float32, 'ssm_log_step': _jnp.float32, 'ssm_b_re': _jnp.float32, 'ssm_b_im': _jnp.float32, 'ssm_c_re': _jnp.float32, 'ssm_c_im': _jnp.float32, 'ssm_d': _jnp.float32, 'w_glu': _jnp.float32, 'b_glu': _jnp.float32, 'attn_out_g': _jnp.float32, 'ssm_out_g': _jnp.float32, 'w_out': _jnp.float32, 'norm2_g': _jnp.float32, 'w_ff1': _jnp.float32, 'w_ff2': _jnp.float32, 'final_g': _jnp.float32}
MOMENT_SCALE = {'w_ada': 2.483135e-02, 'b_ada': 4.244512e-02, 'norm1_g': 1.350140e-02, 'w_in': 1.518265e-02, 'sinks': 3.859104e-03, 'ssm_lam_re': 1.038501e-03, 'ssm_lam_im': 1.266624e-03, 'ssm_log_step': 4.960469e-01, 'ssm_b_re': 5.695928e-04, 'ssm_b_im': 5.950032e-04, 'ssm_c_re': 1.126473e-03, 'ssm_c_im': 1.180704e-03, 'ssm_d': 1.605474e-02, 'w_glu': 4.444541e-03, 'b_glu': 7.399001e-03, 'attn_out_g': 1.471503e-02, 'ssm_out_g': 1.665947e-02, 'w_out': 1.464511e-02, 'norm2_g': 1.885217e-02, 'w_ff1': 9.905668e-03, 'w_ff2': 1.842123e-02, 'final_g': 8.060115e+00}


def _to_microbatches(a, axis):
    t = _jnp.moveaxis(a, axis, 0)
    t = t.reshape((N_MICROBATCH, t.shape[0] // N_MICROBATCH) + t.shape[1:])
    return _jnp.moveaxis(t, 1, axis + 1)


def setup_inputs(seed: int = 0) -> dict:
    inp = _fwd_setup_inputs(seed)
    key = _jax.random.fold_in(_jax.random.key(seed), 7919)
    shape, _ = _output_shape()
    out = dict(inp)
    out["loss_target"] = _jax.random.normal(_jax.random.fold_in(key, 0), shape, _jnp.float32)
    for i, name in enumerate(TWIN_WEIGHTS):
        w = inp[name].astype(_jnp.float32)
        if MOMENT_SCALE is None:
            s = _jnp.sqrt(_jnp.mean(_jnp.square(w)) + 1e-30)
        else:
            s = MOMENT_SCALE[name]
        km, kv = _jax.random.split(_jax.random.fold_in(key, i + 1))
        out[name] = w
        out["m_" + name] = s * _jax.random.normal(km, w.shape, _jnp.float32)
        out["v_" + name] = (s * s) * _jax.random.uniform(kv, w.shape, _jnp.float32, 0.5, 1.5)
    if N_MICROBATCH > 1:
        for name, axis in PER_EXAMPLE_BATCH_AXIS.items():
            out[name] = _to_microbatches(out[name], axis)
    return {'x': out['x'], 'c': out['c'], 'w_ada': out['w_ada'], 'b_ada': out['b_ada'], 'norm1_g': out['norm1_g'], 'w_in': out['w_in'], 'sinks': out['sinks'], 'ssm_lam_re': out['ssm_lam_re'], 'ssm_lam_im': out['ssm_lam_im'], 'ssm_log_step': out['ssm_log_step'], 'ssm_b_re': out['ssm_b_re'], 'ssm_b_im': out['ssm_b_im'], 'ssm_c_re': out['ssm_c_re'], 'ssm_c_im': out['ssm_c_im'], 'ssm_d': out['ssm_d'], 'w_glu': out['w_glu'], 'b_glu': out['b_glu'], 'attn_out_g': out['attn_out_g'], 'ssm_out_g': out['ssm_out_g'], 'w_out': out['w_out'], 'norm2_g': out['norm2_g'], 'w_ff1': out['w_ff1'], 'w_ff2': out['w_ff2'], 'final_g': out['final_g'], 'loss_target': out['loss_target'], 'm_w_ada': out['m_w_ada'], 'm_b_ada': out['m_b_ada'], 'm_norm1_g': out['m_norm1_g'], 'm_w_in': out['m_w_in'], 'm_sinks': out['m_sinks'], 'm_ssm_lam_re': out['m_ssm_lam_re'], 'm_ssm_lam_im': out['m_ssm_lam_im'], 'm_ssm_log_step': out['m_ssm_log_step'], 'm_ssm_b_re': out['m_ssm_b_re'], 'm_ssm_b_im': out['m_ssm_b_im'], 'm_ssm_c_re': out['m_ssm_c_re'], 'm_ssm_c_im': out['m_ssm_c_im'], 'm_ssm_d': out['m_ssm_d'], 'm_w_glu': out['m_w_glu'], 'm_b_glu': out['m_b_glu'], 'm_attn_out_g': out['m_attn_out_g'], 'm_ssm_out_g': out['m_ssm_out_g'], 'm_w_out': out['m_w_out'], 'm_norm2_g': out['m_norm2_g'], 'm_w_ff1': out['m_w_ff1'], 'm_w_ff2': out['m_w_ff2'], 'm_final_g': out['m_final_g'], 'v_w_ada': out['v_w_ada'], 'v_b_ada': out['v_b_ada'], 'v_norm1_g': out['v_norm1_g'], 'v_w_in': out['v_w_in'], 'v_sinks': out['v_sinks'], 'v_ssm_lam_re': out['v_ssm_lam_re'], 'v_ssm_lam_im': out['v_ssm_lam_im'], 'v_ssm_log_step': out['v_ssm_log_step'], 'v_ssm_b_re': out['v_ssm_b_re'], 'v_ssm_b_im': out['v_ssm_b_im'], 'v_ssm_c_re': out['v_ssm_c_re'], 'v_ssm_c_im': out['v_ssm_c_im'], 'v_ssm_d': out['v_ssm_d'], 'v_w_glu': out['v_w_glu'], 'v_b_glu': out['v_b_glu'], 'v_attn_out_g': out['v_attn_out_g'], 'v_ssm_out_g': out['v_ssm_out_g'], 'v_w_out': out['v_w_out'], 'v_norm2_g': out['v_norm2_g'], 'v_w_ff1': out['v_w_ff1'], 'v_w_ff2': out['v_w_ff2'], 'v_final_g': out['v_final_g']}


def _loss(weights, diff, rest, loss_target):
    with _jax.named_scope("forward"):
        args = {**rest, TWIN_DIFF_INPUT: diff, **{k: w.astype(_WEIGHT_DTYPES[k]) for k, w in weights.items()}}
        y = _forward(args)
    with _jax.named_scope("loss_head"):
        err = _jnp.square(y.astype(_jnp.float32) - loss_target)
        return 0.5 * _jnp.sum(_jnp.mean(err, axis=-1)) if err.ndim else 0.5 * err


def _adamw(w, g, m, v):
    m = ADAM_B1 * m + (1.0 - ADAM_B1) * g
    v = ADAM_B2 * v + (1.0 - ADAM_B2) * _jnp.square(g)
    m_hat = m / (1.0 - ADAM_B1 ** ADAM_STEP)
    v_hat = v / (1.0 - ADAM_B2 ** ADAM_STEP)
    delta = -ADAM_LR * (m_hat / (_jnp.sqrt(v_hat) + ADAM_EPS) + ADAM_WD * w)
    return delta, m, v


def reference(x, c, w_ada, b_ada, norm1_g, w_in, sinks, ssm_lam_re, ssm_lam_im, ssm_log_step, ssm_b_re, ssm_b_im, ssm_c_re, ssm_c_im, ssm_d, w_glu, b_glu, attn_out_g, ssm_out_g, w_out, norm2_g, w_ff1, w_ff2, final_g, loss_target, m_w_ada, m_b_ada, m_norm1_g, m_w_in, m_sinks, m_ssm_lam_re, m_ssm_lam_im, m_ssm_log_step, m_ssm_b_re, m_ssm_b_im, m_ssm_c_re, m_ssm_c_im, m_ssm_d, m_w_glu, m_b_glu, m_attn_out_g, m_ssm_out_g, m_w_out, m_norm2_g, m_w_ff1, m_w_ff2, m_final_g, v_w_ada, v_b_ada, v_norm1_g, v_w_in, v_sinks, v_ssm_lam_re, v_ssm_lam_im, v_ssm_log_step, v_ssm_b_re, v_ssm_b_im, v_ssm_c_re, v_ssm_c_im, v_ssm_d, v_w_glu, v_b_glu, v_attn_out_g, v_ssm_out_g, v_w_out, v_norm2_g, v_w_ff1, v_w_ff2, v_final_g):
    given = dict(x=x, c=c, w_ada=w_ada, b_ada=b_ada, norm1_g=norm1_g, w_in=w_in, sinks=sinks, ssm_lam_re=ssm_lam_re, ssm_lam_im=ssm_lam_im, ssm_log_step=ssm_log_step, ssm_b_re=ssm_b_re, ssm_b_im=ssm_b_im, ssm_c_re=ssm_c_re, ssm_c_im=ssm_c_im, ssm_d=ssm_d, w_glu=w_glu, b_glu=b_glu, attn_out_g=attn_out_g, ssm_out_g=ssm_out_g, w_out=w_out, norm2_g=norm2_g, w_ff1=w_ff1, w_ff2=w_ff2, final_g=final_g, loss_target=loss_target, m_w_ada=m_w_ada, m_b_ada=m_b_ada, m_norm1_g=m_norm1_g, m_w_in=m_w_in, m_sinks=m_sinks, m_ssm_lam_re=m_ssm_lam_re, m_ssm_lam_im=m_ssm_lam_im, m_ssm_log_step=m_ssm_log_step, m_ssm_b_re=m_ssm_b_re, m_ssm_b_im=m_ssm_b_im, m_ssm_c_re=m_ssm_c_re, m_ssm_c_im=m_ssm_c_im, m_ssm_d=m_ssm_d, m_w_glu=m_w_glu, m_b_glu=m_b_glu, m_attn_out_g=m_attn_out_g, m_ssm_out_g=m_ssm_out_g, m_w_out=m_w_out, m_norm2_g=m_norm2_g, m_w_ff1=m_w_ff1, m_w_ff2=m_w_ff2, m_final_g=m_final_g, v_w_ada=v_w_ada, v_b_ada=v_b_ada, v_norm1_g=v_norm1_g, v_w_in=v_w_in, v_sinks=v_sinks, v_ssm_lam_re=v_ssm_lam_re, v_ssm_lam_im=v_ssm_lam_im, v_ssm_log_step=v_ssm_log_step, v_ssm_b_re=v_ssm_b_re, v_ssm_b_im=v_ssm_b_im, v_ssm_c_re=v_ssm_c_re, v_ssm_c_im=v_ssm_c_im, v_ssm_d=v_ssm_d, v_w_glu=v_w_glu, v_b_glu=v_b_glu, v_attn_out_g=v_attn_out_g, v_ssm_out_g=v_ssm_out_g, v_w_out=v_w_out, v_norm2_g=v_norm2_g, v_w_ff1=v_w_ff1, v_w_ff2=v_w_ff2, v_final_g=v_final_g)
    weights = {n: given[n] for n in TWIN_WEIGHTS}
    shared = {n: given[n] for n in SHARED_INPUTS}
    per_example = {n: given[n] for n in ['x', 'c']}
    grad_fn = _jax.value_and_grad(_loss, argnums=(0, 1))

    def one_microbatch(ex, loss_target):
        ex = dict(ex)
        diff = ex.pop(TWIN_DIFF_INPUT)
        return grad_fn(weights, diff, {**shared, **ex}, loss_target)

    if N_MICROBATCH == 1:
        loss, (grad_w, grad_x) = one_microbatch(per_example, given["loss_target"])
    else:
        def body(carry, xs):
            loss_sum, grad_sum = carry
            l_k, (gw_k, gx_k) = one_microbatch(xs[0], xs[1])
            with _jax.named_scope("update"):
                return (loss_sum + l_k, _jax.tree.map(_jnp.add, grad_sum, gw_k)), gx_k

        init = (_jnp.zeros((), _jnp.float32), _jax.tree.map(_jnp.zeros_like, weights))
        (loss, grad_w), grad_x = _jax.lax.scan(body, init, (per_example, given["loss_target"]))
    with _jax.named_scope("update"):
        delta_w, new_m, new_v = {}, {}, {}
        for n in TWIN_WEIGHTS:
            delta_w[n], new_m[n], new_v[n] = _adamw(weights[n], grad_w[n], given["m_" + n], given["v_" + n])
    return (loss, grad_x, *[grad_w[n] for n in TWIN_WEIGHTS], *[delta_w[n] for n in TWIN_WEIGHTS],
            *[new_m[n] for n in TWIN_WEIGHTS], *[new_v[n] for n in TWIN_WEIGHTS])
```

```python
import functools
import math

import jax
import jax.numpy as jnp
from jax import lax
from jax.experimental import pallas as pl
from jax.experimental.pallas import tpu as pltpu

F32 = jnp.float32
BF16 = jnp.bfloat16
MESH = pl.DeviceIdType.MESH

EPS = 1e-6
HEAD_DIM = 64
Q_PER_KV = 8
WINDOW = 128
ROPE_THETA = 10000.0
SSM_GROUP = 16
STATE = 64
GROUPS_PER_BLOCK = 16
N_MOD = 6
N_CHIPS = 4
N_DEV = 8
ADAM_LR = 0.001
ADAM_B1 = 0.9
ADAM_B2 = 0.999
ADAM_EPS = 1e-08
ADAM_WD = 0.01
ADAM_STEP = 10
LANES = 128
SUBLANES = 8
VMEM_LIMIT = 56 * 1024 * 1024
PACK_COLS = 512


def _pcall(body, **kw):
    return pl.pallas_call(body, **kw)


def _params(sem=None):
    return pltpu.CompilerParams(dimension_semantics=sem, vmem_limit_bytes=VMEM_LIMIT)


def _tile(n, want, unit):
    if n <= want:
        return n
    t = (want // unit) * unit
    while t > unit and n % t:
        t -= unit
    assert n % t == 0, (n, want, unit)
    return t


_NN = (((1,), (0,)), ((), ()))
_NT = (((1,), (1,)), ((), ()))
_TN = (((0,), (0,)), ((), ()))


def _matmul(a, b, mode, name, out_dtypes, epilogue=None, extras=(), b_stacked=False, out_stacked=0,
            tm=1024, tn=1024, tk=512, precision=None):
    if mode == "nn":
        m, kdim = a.shape
        if b_stacked:
            s, _, nsh = b.shape
            n = s * nsh
            tn = _tile(nsh, tn, LANES)
        else:
            n = b.shape[1]
            tn = _tile(n, tn, LANES)
        tm, tk = _tile(m, tm, SUBLANES * 2), _tile(kdim, tk, LANES)
        a_spec = pl.BlockSpec((tm, tk), lambda i, j, k: (i, k))
        if b_stacked:
            npb = nsh // tn
            b_spec = pl.BlockSpec((None, tk, tn), lambda i, j, k: (j // npb, k, j % npb))
        else:
            b_spec = pl.BlockSpec((tk, tn), lambda i, j, k: (k, j))
        dims = _NN
    elif mode == "nt":
        m, kdim = a.shape
        if b_stacked:
            s, n, ksh = b.shape
            tk = _tile(ksh, tk, LANES)
            kpb = ksh // tk
            tn = _tile(n, tn, LANES)
            b_spec = pl.BlockSpec((None, tn, tk), lambda i, j, k: (k // kpb, j, k % kpb))
        else:
            n = b.shape[0]
            tk = _tile(kdim, tk, LANES)
            tn = _tile(n, tn, LANES)
            b_spec = pl.BlockSpec((tn, tk), lambda i, j, k: (j, k))
        tm = _tile(m, tm, SUBLANES * 2)
        a_spec = pl.BlockSpec((tm, tk), lambda i, j, k: (i, k))
        dims = _NT
    else:
        kdim, m = a.shape
        n = b.shape[1]
        tm = _tile(m, tm, LANES)
        tk = _tile(kdim, tk, SUBLANES * 2)
        if out_stacked:
            nsh = n // out_stacked
            tn = _tile(nsh, tn, LANES)
        else:
            tn = _tile(n, tn, LANES)
        a_spec = pl.BlockSpec((tk, tm), lambda i, j, k: (k, i))
        b_spec = pl.BlockSpec((tk, tn), lambda i, j, k: (k, j))
        dims = _TN
    nk = kdim // tk
    grid = (m // tm, n // tn, nk)
    if out_stacked:
        npo = (n // out_stacked) // tn
        o_spec = pl.BlockSpec((None, tm, tn), lambda i, j, k: (j // npo, i, j % npo))
        out_shape = [jax.ShapeDtypeStruct((out_stacked, m, n // out_stacked), dt) for dt in out_dtypes]
    else:
        o_spec = pl.BlockSpec((tm, tn), lambda i, j, k: (i, j))
        out_shape = [jax.ShapeDtypeStruct((m, n), dt) for dt in out_dtypes]
    x_spec = pl.BlockSpec((tm, tn), lambda i, j, k: (i, j))
    n_ex, n_out = len(extras), len(out_dtypes)

    def body(a_ref, b_ref, *rest):
        ex_refs, out_refs, acc_ref = rest[:n_ex], rest[n_ex:n_ex + n_out], rest[-1]
        k = pl.program_id(2)

        @pl.when(k == 0)
        def _():
            acc_ref[...] = jnp.zeros_like(acc_ref)

        acc_ref[...] += lax.dot_general(a_ref[...], b_ref[...], dims, precision=precision,
                                        preferred_element_type=F32)

        @pl.when(k == nk - 1)
        def _():
            acc = acc_ref[...]
            outs = (acc,) if epilogue is None else epilogue(acc, *[r[...] for r in ex_refs])
            for r, o in zip(out_refs, outs):
                r[...] = o.astype(r.dtype)

    res = _pcall(
        body, name=name, grid=grid,
        in_specs=[a_spec, b_spec] + [x_spec] * n_ex,
        out_specs=[o_spec] * n_out, out_shape=out_shape,
        scratch_shapes=[pltpu.VMEM((tm, tn), F32)],
        compiler_params=_params(("parallel", "parallel", "arbitrary")),
    )(a, b, *extras)
    return res[0] if n_out == 1 else res


def _rowwise(body, name, rows, row_ins, vec_ins, row_outs, acc_outs, tr=128):
    tr = _tile(rows, tr, SUBLANES * 2)
    n_ri, n_vi, n_ro, n_ao = len(row_ins), len(vec_ins), len(row_outs), len(acc_outs)

    def kern(*refs):
        ri, vi = refs[:n_ri], refs[n_ri:n_ri + n_vi]
        ro = refs[n_ri + n_vi:n_ri + n_vi + n_ro]
        ao = refs[n_ri + n_vi + n_ro:]

        @pl.when(pl.program_id(0) == 0)
        def _():
            for r in ao:
                r[...] = jnp.zeros_like(r)

        body(ri, vi, ro, ao)

    in_specs = [pl.BlockSpec((tr, w), functools.partial(lambda i, cb: (i, cb), cb=cb)) for _, w, cb in row_ins]
    in_specs += [pl.BlockSpec(v.shape, lambda i: (0, 0)) for v in vec_ins]
    out_specs = [pl.BlockSpec((tr, w), lambda i: (i, 0)) for w, _ in row_outs]
    out_specs += [pl.BlockSpec((1, w), lambda i: (0, 0)) for w in acc_outs]
    out_shape = [jax.ShapeDtypeStruct((rows, w), dt) for w, dt in row_outs]
    out_shape += [jax.ShapeDtypeStruct((1, w), F32) for w in acc_outs]
    return _pcall(
        kern, name=name, grid=(rows // tr,), in_specs=in_specs, out_specs=out_specs, out_shape=out_shape,
        compiler_params=_params(("arbitrary",)),
    )(*[a for a, _, _ in row_ins], *vec_ins)


def _colsum(x):
    return jnp.sum(x, axis=0, keepdims=True)


def _rstd(x):
    return lax.rsqrt(jnp.mean(x * x, axis=-1, keepdims=True) + EPS)


def _norm_bwd(dxn, xn, r):
    return r * (dxn - xn * jnp.mean(dxn * xn, axis=-1, keepdims=True))


_SQRT_HALF = math.sqrt(0.5)
_INV_SQRT_2PI = 1.0 / math.sqrt(2.0 * math.pi)


def _gelu(y):
    return 0.5 * y * (1.0 + lax.erf(y * _SQRT_HALF))


def _gelu_grad(y):
    return 0.5 * (1.0 + lax.erf(y * _SQRT_HALF)) + y * jnp.exp(-0.5 * y * y) * _INV_SQRT_2PI


def _norm_mod(x, g, scale, shift):
    def body(ri, vi, ro, ao):
        xv = ri[0][...]
        h = xv * _rstd(xv) * vi[0][...] * (1.0 + vi[1][...]) + vi[2][...]
        ro[0][...] = h.astype(BF16)

    d = x.shape[1]
    return _rowwise(body, "norm_mod", x.shape[0], [(x, d, 0)], [g, scale, shift], [(d, BF16)], [])[0]


def _res_norm_mod(x, mo, gate, g, scale, shift):
    def body(ri, vi, ro, ao):
        x1 = ri[0][...] + vi[0][...] * ri[1][...]
        ro[0][...] = x1
        ro[1][...] = (x1 * _rstd(x1) * vi[1][...] * (1.0 + vi[2][...]) + vi[3][...]).astype(BF16)

    d = x.shape[1]
    return _rowwise(body, "res_norm_mod", x.shape[0], [(x, d, 0), (mo, d, 0)], [gate, g, scale, shift],
                    [(d, F32), (d, BF16)], [])


def _mix(attn, y0, gl, b_glu, ga, gs):
    da, ds = attn.shape[1], y0.shape[1]

    def body(ri, vi, ro, ao):
        at = ri[0][...]
        z = _gelu(ri[1][...])
        o = z * jax.nn.sigmoid(ri[2][...] + vi[0][...])
        ro[0][:, :da] = (at * _rstd(at) * vi[1][...]).astype(BF16)
        ro[0][:, da:] = (o * _rstd(o) * vi[2][...]).astype(BF16)

    return _rowwise(body, "mix", attn.shape[0], [(attn, da, 0), (y0, ds, 0), (gl, ds, 0)], [b_glu, ga, gs],
                    [(da + ds, BF16)], [])[0]


def _final(x1, ff, tgt, gate2, gf):
    d = x1.shape[1]

    def body(ri, vi, ro, ao):
        ffv = ri[1][...]
        x2 = ri[0][...] + vi[0][...] * ffv
        r = _rstd(x2)
        xn = x2 * r
        e = xn * vi[1][...] - ri[2][...]
        ao[0][...] += _colsum(e * e)
        dy = e * (1.0 / d)
        ao[1][...] += _colsum(dy * xn)
        dx2 = _norm_bwd(dy * vi[1][...], xn, r)
        ao[2][...] += _colsum(dx2 * ffv)
        ro[0][...] = dx2
        ro[1][...] = (dx2 * vi[0][...]).astype(BF16)

    return _rowwise(body, "final", x1.shape[0], [(x1, d, 0), (ff, d, 0), (tgt, d, 0)], [gate2, gf],
                    [(d, F32), (d, BF16)], [d, d, d])


def _bwd_norm2(x1, dh2, dx2, mo, g2, scale2, gate1):
    d = x1.shape[1]

    def body(ri, vi, ro, ao):
        xv, dh = ri[0][...], ri[1][...]
        r = _rstd(xv)
        xn = xv * r
        ao[0][...] += _colsum(dh)
        ao[1][...] += _colsum(dh * xn * vi[0][...])
        dn = dh * (1.0 + vi[1][...])
        ao[2][...] += _colsum(dn * xn)
        dx1 = ri[2][...] + _norm_bwd(dn * vi[0][...], xn, r)
        ao[3][...] += _colsum(dx1 * ri[3][...])
        ro[0][...] = dx1
        ro[1][...] = (dx1 * vi[2][...]).astype(BF16)

    return _rowwise(body, "bwd_norm2", x1.shape[0], [(x1, d, 0), (dh2, d, 0), (dx2, d, 0), (mo, d, 0)],
                    [g2, scale2, gate1], [(d, F32), (d, BF16)], [d, d, d, d])


def _bwd_mix(dmixed, attn, y0, gl, b_glu, ga, gs):
    da, ds = attn.shape[1], y0.shape[1]

    def body(ri, vi, ro, ao):
        dan, dsn = ri[0][:, :da], ri[0][:, da:]
        at = ri[1][...]
        ra = _rstd(at)
        an = at * ra
        ao[0][...] += _colsum(dan * an)
        ro[0][...] = _norm_bwd(dan * vi[1][...], an, ra)
        z = _gelu(ri[2][...])
        sg = jax.nn.sigmoid(ri[3][...] + vi[0][...])
        o = z * sg
        rs = _rstd(o)
        on = o * rs
        ao[1][...] += _colsum(dsn * on)
        do = _norm_bwd(dsn * vi[2][...], on, rs)
        ro[2][...] = do * sg
        dgl = do * z * sg * (1.0 - sg)
        ao[2][...] += _colsum(dgl)
        ro[1][...] = dgl.astype(BF16)

    return _rowwise(body, "bwd_mix", attn.shape[0],
                    [(dmixed, da + ds, 0), (attn, da, 0), (y0, ds, 0), (gl, ds, 0)], [b_glu, ga, gs],
                    [(da, F32), (ds, BF16), (ds, F32)], [da, ds, ds])


def _bwd_norm1(x, dh, dx1, g1, scale1):
    d = x.shape[1]

    def body(ri, vi, ro, ao):
        xv, dhv = ri[0][...], ri[1][...]
        r = _rstd(xv)
        xn = xv * r
        ao[0][...] += _colsum(dhv)
        ao[1][...] += _colsum(dhv * xn * vi[0][...])
        dn = dhv * (1.0 + vi[1][...])
        ao[2][...] += _colsum(dn * xn)
        ro[0][...] = ri[2][...] + _norm_bwd(dn * vi[0][...], xn, r)

    return _rowwise(body, "bwd_norm1", x.shape[0], [(x, d, 0), (dh, d, 0), (dx1, d, 0)], [g1, scale1],
                    [(d, F32)], [d, d, d])


def _rope_apply(x, cos, sin, sign):
    first = (lax.broadcasted_iota(jnp.int32, cos.shape, 1) % HEAD_DIM) < (HEAD_DIM // 2)
    outs = []
    for j in range(x.shape[1] // LANES):
        xc = x[:, j * LANES:(j + 1) * LANES]
        rot = jnp.where(first, -pltpu.roll(xc, LANES - HEAD_DIM // 2, 1), pltpu.roll(xc, HEAD_DIM // 2, 1))
        outs.append(xc * cos + sign * (rot * sin))
    return outs


def _rope_fwd(proj, cos, sin, d_attn, d_kv):
    scale = HEAD_DIM ** -0.5
    kcb, vcb = d_attn // d_kv, d_attn // d_kv + 1

    def body(ri, vi, ro, ao):
        c, s = ri[3][...], ri[4][...]
        for j, o in enumerate(_rope_apply(ri[0][...], c, s, 1.0)):
            ro[0][:, j * LANES:(j + 1) * LANES] = (o * scale).astype(BF16)
        for j, o in enumerate(_rope_apply(ri[1][...], c, s, 1.0)):
            ro[1][:, j * LANES:(j + 1) * LANES] = o.astype(BF16)
        ro[2][...] = ri[2][...].astype(BF16)

    return _rowwise(body, "rope_fwd", proj.shape[0],
                    [(proj, d_attn, 0), (proj, d_kv, kcb), (proj, d_kv, vcb), (cos, LANES, 0), (sin, LANES, 0)], [],
                    [(d_attn, BF16), (d_kv, BF16), (d_kv, BF16)], [])


def _rope_bwd(dqr, dkc, dkp, dvc, dvp, cos, sin):
    scale = HEAD_DIM ** -0.5
    d_attn, d_kv = dqr.shape[1], dkc.shape[1]

    def body(ri, vi, ro, ao):
        c, s = ri[5][...], ri[6][...]
        for j, o in enumerate(_rope_apply(ri[0][...], c, s, -1.0)):
            ro[0][:, j * LANES:(j + 1) * LANES] = (o * scale).astype(BF16)
        for j, o in enumerate(_rope_apply(ri[1][...] + ri[2][...], c, s, -1.0)):
            ro[1][:, j * LANES:(j + 1) * LANES] = o.astype(BF16)
        ro[2][...] = (ri[3][...] + ri[4][...]).astype(BF16)

    return _rowwise(body, "rope_bwd", dqr.shape[0],
                    [(dqr, d_attn, 0), (dkc, d_kv, 0), (dkp, d_kv, 0), (dvc, d_kv, 0), (dvp, d_kv, 0),
                     (cos, LANES, 0), (sin, LANES, 0)], [],
                    [(d_attn, BF16), (d_kv, BF16), (d_kv, BF16)], [])


def _attn_probs(q, k, sink_ref, g, n):
    rows = Q_PER_KV * WINDOW
    s = lax.dot_general(q, k, _NT, preferred_element_type=F32)
    qi = lax.broadcasted_iota(jnp.int32, (rows, 2 * WINDOW), 0) % WINDOW + WINDOW
    kj = lax.broadcasted_iota(jnp.int32, (rows, 2 * WINDOW), 1)
    rel = qi - kj
    mask = (rel >= 0) & (rel < WINDOW) & ((n > 0) | (kj >= WINDOW))
    s = jnp.where(mask, s, -1e30)
    sink = jnp.concatenate([jnp.full((WINDOW, 1), sink_ref[g * Q_PER_KV + j], F32) for j in range(Q_PER_KV)], axis=0)
    m = jnp.maximum(jnp.max(s, axis=-1, keepdims=True), sink)
    p = jnp.exp(s - m)
    es = jnp.exp(sink - m)
    l = jnp.sum(p, axis=-1, keepdims=True) + es
    return p, l, es


def _attn_specs(n_kv):
    qspec = pl.BlockSpec((Q_PER_KV, WINDOW, HEAD_DIM), lambda g, n: (g, n, 0))
    cur = pl.BlockSpec((1, WINDOW, HEAD_DIM), lambda g, n: (g, n, 0))
    prev = pl.BlockSpec((1, WINDOW, HEAD_DIM), lambda g, n: (g, jnp.maximum(n - 1, 0), 0))
    return qspec, cur, prev


def _attn_fwd(q, k, v, sinks):
    n_kv, t = k.shape[0], k.shape[1]
    rows = Q_PER_KV * WINDOW

    def body(sink_ref, q_ref, kp_ref, kc_ref, vp_ref, vc_ref, o_ref):
        g, n = pl.program_id(0), pl.program_id(1)
        qv = q_ref[...].reshape(rows, HEAD_DIM)
        kv = jnp.concatenate([kp_ref[0], kc_ref[0]], axis=0)
        vv = jnp.concatenate([vp_ref[0], vc_ref[0]], axis=0)
        p, l, _ = _attn_probs(qv, kv, sink_ref, g, n)
        o = jnp.dot(p.astype(BF16), vv, preferred_element_type=F32) / l
        o_ref[...] = o.reshape(Q_PER_KV, WINDOW, HEAD_DIM)

    qspec, cur, prev = _attn_specs(n_kv)
    return _pcall(
        body, name="attn_fwd", grid=(n_kv, t // WINDOW),
        in_specs=[pl.BlockSpec(memory_space=pltpu.SMEM), qspec, prev, cur, prev, cur],
        out_specs=qspec, out_shape=jax.ShapeDtypeStruct(q.shape, F32),
        compiler_params=_params(("parallel", "arbitrary")),
    )(sinks, q, k, k, v, v)


def _attn_bwd(q, k, v, o, do, sinks):
    n_kv, t = k.shape[0], k.shape[1]
    rows = Q_PER_KV * WINDOW

    def body(sink_ref, q_ref, kp_ref, kc_ref, vp_ref, vc_ref, o_ref, do_ref,
             dq_ref, dkc_ref, dkp_ref, dvc_ref, dvp_ref, ds_ref):
        g, n = pl.program_id(0), pl.program_id(1)
        qv = q_ref[...].reshape(rows, HEAD_DIM)
        kv = jnp.concatenate([kp_ref[0], kc_ref[0]], axis=0)
        vv = jnp.concatenate([vp_ref[0], vc_ref[0]], axis=0)
        p, l, es = _attn_probs(qv, kv, sink_ref, g, n)
        inv_l = 1.0 / l
        pn = p * inv_l
        dov = do_ref[...].reshape(rows, HEAD_DIM)
        delta = jnp.sum(dov * o_ref[...].reshape(rows, HEAD_DIM), axis=-1, keepdims=True)
        dob = dov.astype(BF16)
        dv = lax.dot_general(pn.astype(BF16), dob, _TN, preferred_element_type=F32)
        dp = lax.dot_general(dob, vv, _NT, preferred_element_type=F32)
        dsb = (pn * (dp - delta)).astype(BF16)
        dq_ref[...] = jnp.dot(dsb, kv, preferred_element_type=F32).reshape(Q_PER_KV, WINDOW, HEAD_DIM)
        dk = lax.dot_general(dsb, qv, _TN, preferred_element_type=F32)
        dkp_ref[0] = dk[:WINDOW]
        dkc_ref[0] = dk[WINDOW:]
        dvp_ref[0] = dv[:WINDOW]
        dvc_ref[0] = dv[WINDOW:]

        @pl.when(n == 0)
        def _():
            ds_ref[...] = jnp.zeros_like(ds_ref)

        ds_ref[...] += -(es * inv_l) * delta

    qspec, cur, prev = _attn_specs(n_kv)
    sspec = pl.BlockSpec((None, rows, 1), lambda g, n: (g, 0, 0))
    kshape = jax.ShapeDtypeStruct(k.shape, F32)
    return _pcall(
        body, name="attn_bwd", grid=(n_kv, t // WINDOW),
        in_specs=[pl.BlockSpec(memory_space=pltpu.SMEM), qspec, prev, cur, prev, cur, qspec, qspec],
        out_specs=[qspec, cur, cur, cur, cur, sspec],
        out_shape=[jax.ShapeDtypeStruct(q.shape, F32), kshape, kshape, kshape, kshape,
                   jax.ShapeDtypeStruct((n_kv, rows, 1), F32)],
        compiler_params=_params(("parallel", "arbitrary")),
    )(sinks, q, k, k, v, v, o, do)


def _cmul(ar, ai, br, bi):
    return ar * br - ai * bi, ar * bi + ai * br


def _scan_consts(ar, ai, half, reverse):
    row = lax.broadcasted_iota(jnp.int32, (SUBLANES, half), 0)
    a2 = _cmul(ar, ai, ar, ai)
    a4 = _cmul(*a2, *a2)
    steps = [(1, ar, ai), (2, *a2), (4, *a4)]
    pr, pi = ar, ai
    pwr = jnp.zeros((SUBLANES, half), F32)
    pwi = jnp.zeros((SUBLANES, half), F32)
    for r in range(SUBLANES):
        sel = row == (SUBLANES - 1 - r if reverse else r)
        pwr = jnp.where(sel, pr, pwr)
        pwi = jnp.where(sel, pi, pwi)
        pr, pi = _cmul(pr, pi, ar, ai)
    return row, steps, pwr, pwi


def _scan8(xr, xi, row, steps, pwr, pwi, cr, ci, reverse):
    for d, er, ei in steps:
        if reverse:
            keep, shift = row < SUBLANES - d, SUBLANES - d
        else:
            keep, shift = row >= d, d
        sr = jnp.where(keep, pltpu.roll(xr, shift, 0), 0.0)
        si = jnp.where(keep, pltpu.roll(xi, shift, 0), 0.0)
        tr, ti = _cmul(er, ei, sr, si)
        xr, xi = xr + tr, xi + ti
    tr, ti = _cmul(pwr, pwi, cr, ci)
    return xr + tr, xi + ti


def _ssm_fwd(proj, ucb0, bexp, cexp, a_cat, d_skip, tt=512):
    t = proj.shape[0]
    ngb, cw, two_l = bexp.shape
    half = two_l // 2
    tt = _tile(t, tt, SUBLANES * 2)
    nt = t // tt

    def body(u_ref, b_ref, c_ref, a_ref, d_ref, y_ref, z_ref, st_ref, carry_ref):
        @pl.when(pl.program_id(1) == 0)
        def _():
            carry_ref[...] = jnp.zeros_like(carry_ref)

        u = u_ref[...]
        st_ref[...] = jnp.dot(u.astype(BF16), b_ref[...], preferred_element_type=F32)
        ar, ai = a_ref[:, :half], a_ref[:, half:]
        row, steps, pwr, pwi = _scan_consts(ar, ai, half, False)

        def tile(i, carry):
            base = pl.multiple_of(i * SUBLANES, SUBLANES)
            xr, xi = _scan8(st_ref[pl.ds(base, SUBLANES), :half], st_ref[pl.ds(base, SUBLANES), half:],
                            row, steps, pwr, pwi, carry[0], carry[1], False)
            st_ref[pl.ds(base, SUBLANES), :half] = xr
            st_ref[pl.ds(base, SUBLANES), half:] = xi
            return xr[SUBLANES - 1:, :], xi[SUBLANES - 1:, :]

        cr, ci = lax.fori_loop(0, tt // SUBLANES, tile, (carry_ref[0:1, :half], carry_ref[0:1, half:]))
        carry_ref[0:1, :half] = cr
        carry_ref[0:1, half:] = ci
        y = jnp.dot(st_ref[...].astype(BF16), c_ref[...], preferred_element_type=F32) + d_ref[...] * u
        y_ref[...] = y
        z_ref[...] = _gelu(y).astype(BF16)

    d_ssm = ngb * cw
    return _pcall(
        body, name="ssm_fwd", grid=(ngb, nt),
        in_specs=[pl.BlockSpec((tt, cw), lambda g, i: (i, ucb0 + g)),
                  pl.BlockSpec((None, cw, two_l), lambda g, i: (g, 0, 0)),
                  pl.BlockSpec((None, two_l, cw), lambda g, i: (g, 0, 0)),
                  pl.BlockSpec((None, 1, two_l), lambda g, i: (g, 0, 0)),
                  pl.BlockSpec((1, cw), lambda g, i: (0, g))],
        out_specs=[pl.BlockSpec((tt, cw), lambda g, i: (i, g)),
                   pl.BlockSpec((tt, cw), lambda g, i: (i, g)),
                   pl.BlockSpec((tt, two_l), lambda g, i: (i, g))],
        out_shape=[jax.ShapeDtypeStruct((t, d_ssm), F32), jax.ShapeDtypeStruct((t, d_ssm), BF16),
                   jax.ShapeDtypeStruct((t, ngb * two_l), F32)],
        scratch_shapes=[pltpu.VMEM((SUBLANES, two_l), F32)],
        compiler_params=_params(("parallel", "arbitrary")),
    )(proj, bexp, cexp, a_cat, d_skip)


def _ssm_bwd(dy0, proj, ucb0, states, bexp, cexp, a_cat, d_skip, tt=512):
    t = dy0.shape[0]
    ngb, cw, two_l = bexp.shape
    half = two_l // 2
    tt = _tile(t, tt, SUBLANES * 2)
    nt = t // tt

    def body(dy_ref, u_ref, st_ref, b_ref, c_ref, a_ref, d_ref,
             du_ref, db_ref, dc_ref, da_ref, dd_ref, lam_ref, carry_ref, acc_ref):
        step = pl.program_id(1)

        @pl.when(step == 0)
        def _():
            carry_ref[...] = jnp.zeros_like(carry_ref)
            acc_ref[...] = jnp.zeros_like(acc_ref)
            db_ref[...] = jnp.zeros_like(db_ref)
            dc_ref[...] = jnp.zeros_like(dc_ref)
            dd_ref[...] = jnp.zeros_like(dd_ref)

        dy, u = dy_ref[...], u_ref[...]
        dyb = dy.astype(BF16)
        lam_ref[...] = lax.dot_general(dyb, c_ref[...], _NT, preferred_element_type=F32)
        ar, ai = a_ref[:, :half], -a_ref[:, half:]
        row, steps, pwr, pwi = _scan_consts(ar, ai, half, True)
        last = row == SUBLANES - 1

        def tile(i, carry):
            cr, ci, accr, acci = carry
            base = pl.multiple_of((tt // SUBLANES - 1 - i) * SUBLANES, SUBLANES)
            xr, xi = _scan8(lam_ref[pl.ds(base, SUBLANES), :half], lam_ref[pl.ds(base, SUBLANES), half:],
                            row, steps, pwr, pwi, cr, ci, True)
            lam_ref[pl.ds(base, SUBLANES), :half] = xr
            lam_ref[pl.ds(base, SUBLANES), half:] = xi
            nr = jnp.where(last, cr, pltpu.roll(xr, SUBLANES - 1, 0))
            ni = jnp.where(last, ci, pltpu.roll(xi, SUBLANES - 1, 0))
            sr, si = st_ref[pl.ds(base, SUBLANES), :half], st_ref[pl.ds(base, SUBLANES), half:]
            return xr[0:1, :], xi[0:1, :], accr + sr * nr + si * ni, acci + sr * ni - si * nr

        cr, ci, accr, acci = lax.fori_loop(
            0, tt // SUBLANES, tile,
            (carry_ref[0:1, :half], carry_ref[0:1, half:], acc_ref[:, :half], acc_ref[:, half:]))
        carry_ref[0:1, :half] = cr
        carry_ref[0:1, half:] = ci
        acc_ref[:, :half] = accr
        acc_ref[:, half:] = acci
        lamb = lam_ref[...].astype(BF16)
        du = lax.dot_general(lamb, b_ref[...], _NT, preferred_element_type=F32) + d_ref[...] * dy
        du_ref[...] = du.astype(BF16)
        db_ref[...] += lax.dot_general(u.astype(BF16), lamb, _TN, preferred_element_type=F32)
        dc_ref[...] += lax.dot_general(st_ref[...].astype(BF16), dyb, _TN, preferred_element_type=F32)
        dd_ref[...] += _colsum(dy * u)

        @pl.when(step == nt - 1)
        def _():
            da_ref[...] = _colsum(acc_ref[...])

    d_ssm = ngb * cw
    return _pcall(
        body, name="ssm_bwd", grid=(ngb, nt),
        in_specs=[pl.BlockSpec((tt, cw), lambda g, i: (nt - 1 - i, g)),
                  pl.BlockSpec((tt, cw), lambda g, i: (nt - 1 - i, ucb0 + g)),
                  pl.BlockSpec((tt, two_l), lambda g, i: (nt - 1 - i, g)),
                  pl.BlockSpec((None, cw, two_l), lambda g, i: (g, 0, 0)),
                  pl.BlockSpec((None, two_l, cw), lambda g, i: (g, 0, 0)),
                  pl.BlockSpec((None, 1, two_l), lambda g, i: (g, 0, 0)),
                  pl.BlockSpec((1, cw), lambda g, i: (0, g))],
        out_specs=[pl.BlockSpec((tt, cw), lambda g, i: (nt - 1 - i, g)),
                   pl.BlockSpec((None, cw, two_l), lambda g, i: (g, 0, 0)),
                   pl.BlockSpec((None, two_l, cw), lambda g, i: (g, 0, 0)),
                   pl.BlockSpec((None, 1, two_l), lambda g, i: (g, 0, 0)),
                   pl.BlockSpec((1, cw), lambda g, i: (0, g))],
        out_shape=[jax.ShapeDtypeStruct((t, d_ssm), BF16),
                   jax.ShapeDtypeStruct((ngb, cw, two_l), F32),
                   jax.ShapeDtypeStruct((ngb, two_l, cw), F32),
                   jax.ShapeDtypeStruct((ngb, 1, two_l), F32),
                   jax.ShapeDtypeStruct((1, d_ssm), F32)],
        scratch_shapes=[pltpu.VMEM((tt, two_l), F32), pltpu.VMEM((SUBLANES, two_l), F32),
                        pltpu.VMEM((SUBLANES, two_l), F32)],
        compiler_params=_params(("parallel", "arbitrary")),
    )(dy0, proj, states, bexp, cexp, a_cat, d_skip)


def _zoh(lr, li, ls):
    step = jnp.exp(ls)
    e = jnp.exp(lr * step)
    ar, ai = e * jnp.cos(li * step), e * jnp.sin(li * step)
    den = lr * lr + li * li
    cr = ((ar - 1.0) * lr + ai * li) / den
    ci = (ai * lr - (ar - 1.0) * li) / den
    return step, ar, ai, den, cr, ci


def _ssm_param_fwd(lr, li, ls, br, bi):
    def body(lr_ref, li_ref, ls_ref, br_ref, bi_ref, ar_ref, ai_ref, bbr_ref, bbi_ref):
        _, ar, ai, _, cr, ci = _zoh(lr_ref[...], li_ref[...], ls_ref[...])
        ar_ref[...] = ar
        ai_ref[...] = ai
        bbr, bbi = _cmul(cr, ci, br_ref[...], bi_ref[...])
        bbr_ref[...] = bbr
        bbi_ref[...] = bbi

    small, big = jax.ShapeDtypeStruct(lr.shape, F32), jax.ShapeDtypeStruct(br.shape, F32)
    return _pcall(body, name="ssm_param_fwd", out_shape=[small, small, big, big],
                  compiler_params=_params())(lr, li, ls, br, bi)


def _ssm_param_bwd(lr, li, ls, br, bi, gar, gai, gbr, gbi):
    def body(lr_ref, li_ref, ls_ref, br_ref, bi_ref, gar_ref, gai_ref, gbr_ref, gbi_ref,
             dlr_ref, dli_ref, dls_ref, dbr_ref, dbi_ref):
        lrv, liv = lr_ref[...], li_ref[...]
        step, ar, ai, den, cr, ci = _zoh(lrv, liv, ls_ref[...])
        brv, biv, gr, gi = br_ref[...], bi_ref[...], gbr_ref[...], gbi_ref[...]
        dbr_ref[...] = cr * gr + ci * gi
        dbi_ref[...] = cr * gi - ci * gr
        gcr = jnp.sum(brv * gr + biv * gi, axis=1, keepdims=True)
        gci = jnp.sum(brv * gi - biv * gr, axis=1, keepdims=True)
        gtr = gar_ref[...] + (lrv * gcr - liv * gci) / den
        gti = gai_ref[...] + (lrv * gci + liv * gcr) / den
        qr = (cr * lrv + ci * liv) / den
        qi = (ci * lrv - cr * liv) / den
        gzr = ar * gtr + ai * gti
        gzi = ar * gti - ai * gtr
        dlr_ref[...] = step * gzr - (qr * gcr + qi * gci)
        dli_ref[...] = step * gzi - (qr * gci - qi * gcr)
        gstep = jnp.sum(lrv * gzr + liv * gzi, axis=2, keepdims=True)
        dls_ref[...] = jnp.broadcast_to(step * gstep, step.shape)

    small, big = jax.ShapeDtypeStruct(lr.shape, F32), jax.ShapeDtypeStruct(br.shape, F32)
    return _pcall(body, name="ssm_param_bwd", out_shape=[small, small, small, big, big],
                  compiler_params=_params())(lr, li, ls, br, bi, gar, gai, gbr, gbi)


def _block_diag_in(bb):
    g, h, p = bb.shape
    nb = g // GROUPS_PER_BLOCK
    eye = jnp.eye(GROUPS_PER_BLOCK, dtype=bb.dtype)[None, :, None, :, None]
    e = bb.reshape(nb, GROUPS_PER_BLOCK, h, 1, p) * eye
    return e.reshape(nb, GROUPS_PER_BLOCK * h, GROUPS_PER_BLOCK * p)


def _block_diag_take(e, h, p):
    nb = e.shape[0]
    eye = jnp.eye(GROUPS_PER_BLOCK, dtype=e.dtype)[None, :, None, :, None]
    d = jnp.sum(e.reshape(nb, GROUPS_PER_BLOCK, h, GROUPS_PER_BLOCK, p) * eye, axis=3)
    return d.reshape(nb * GROUPS_PER_BLOCK, h, p)


def _ada_fwd(c_all, w_sh, b_sh, tn=512):
    bsz, d = c_all.shape
    nsh = w_sh.shape[1]
    tn = _tile(nsh, tn, LANES)

    def body(c_ref, w_ref, b_ref, mod_ref, act_ref):
        act = c_ref[...] * jax.nn.sigmoid(c_ref[...])
        act_ref[...] = act
        mod_ref[...] = jnp.dot(act.astype(BF16), w_ref[...].astype(BF16), preferred_element_type=F32) + b_ref[...]

    return _pcall(
        body, name="ada_fwd", grid=(nsh // tn,),
        in_specs=[pl.BlockSpec((bsz, d), lambda j: (0, 0)), pl.BlockSpec((d, tn), lambda j: (0, j)),
                  pl.BlockSpec((1, tn), lambda j: (0, j))],
        out_specs=[pl.BlockSpec((bsz, tn), lambda j: (0, j)), pl.BlockSpec((bsz, d), lambda j: (0, 0))],
        out_shape=[jax.ShapeDtypeStruct((bsz, nsh), F32), jax.ShapeDtypeStruct((bsz, d), F32)],
        compiler_params=_params(("arbitrary",)),
    )(c_all, w_sh, b_sh)


def _adamw(w, g, m, v, name):
    r, c = w.shape
    tr = _tile(r, max(SUBLANES, (256 * 1024) // c // SUBLANES * SUBLANES), SUBLANES)
    c1, c2 = 1.0 / (1.0 - ADAM_B1 ** ADAM_STEP), 1.0 / (1.0 - ADAM_B2 ** ADAM_STEP)

    def body(w_ref, g_ref, m_ref, v_ref, d_ref, nm_ref, nv_ref):
        gv = g_ref[...]
        nm = ADAM_B1 * m_ref[...] + (1.0 - ADAM_B1) * gv
        nv = ADAM_B2 * v_ref[...] + (1.0 - ADAM_B2) * (gv * gv)
        nm_ref[...] = nm
        nv_ref[...] = nv
        d_ref[...] = -ADAM_LR * ((nm * c1) / (jnp.sqrt(nv * c2) + ADAM_EPS) + ADAM_WD * w_ref[...])

    spec = pl.BlockSpec((tr, c), lambda i: (i, 0))
    shp = jax.ShapeDtypeStruct((r, c), F32)
    return _pcall(body, name=name, grid=(r // tr,), in_specs=[spec] * 4, out_specs=[spec] * 3,
                  out_shape=[shp] * 3, compiler_params=_params(("parallel",)))(w, g, m, v)


def _sum_leading(arr, out_dtype, name):
    n, r, c = arr.shape
    tr = _tile(r, max(SUBLANES * 2, (512 * 1024) // (c * n) // (SUBLANES * 2) * (SUBLANES * 2)), SUBLANES * 2)

    def body(x_ref, o_ref):
        acc = x_ref[0].astype(F32)
        for k in range(1, n):
            acc = acc + x_ref[k].astype(F32)
        o_ref[...] = acc.astype(out_dtype)

    return _pcall(body, name=name, grid=(r // tr,),
                  in_specs=[pl.BlockSpec((n, tr, c), lambda i: (0, i, 0))],
                  out_specs=pl.BlockSpec((tr, c), lambda i: (i, 0)),
                  out_shape=jax.ShapeDtypeStruct((r, c), out_dtype),
                  compiler_params=_params(("parallel",)))(arr)


def _place():
    x, y, c = lax.axis_index("x"), lax.axis_index("y"), lax.axis_index("c")
    chips = [(1 - x, y), (x, 1 - y), (1 - x, 1 - y)]
    return x, y, c, chips


def _allgather8(v, name):
    m, n = v.shape

    def body(x_ref, out_ref, send_sems, recv_sems, local_sem):
        x, y, c, chips = _place()
        me, sibling = (x, y, c), (x, y, 1 - c)

        def slot(px, py, pc):
            return out_ref.at[4 * px + 2 * py + pc]

        def copy(k, block, to, src=None):
            return pltpu.make_async_remote_copy(
                src_ref=slot(*block) if src is None else src, dst_ref=slot(*block),
                send_sem=send_sems.at[k], recv_sem=recv_sems.at[k], device_id=to, device_id_type=MESH)

        mine = pltpu.make_async_copy(x_ref, slot(*me), local_sem)
        mine.start()
        first = [copy(0, me, sibling, src=x_ref)]
        first += [copy(1 + j, me, (*chip, c), src=x_ref) for j, chip in enumerate(chips)]
        for cp in first:
            cp.start()
        passed = [copy(4 + j, (*chip, c), sibling) for j, chip in enumerate(chips)]
        for j, chip in enumerate(chips):
            copy(1 + j, (*chip, c), me).wait_recv()
            passed[j].start()
        copy(0, sibling, me).wait_recv()
        for j, chip in enumerate(chips):
            copy(4 + j, (*chip, 1 - c), me).wait_recv()
        for cp in first + passed:
            cp.wait_send()
        mine.wait()

    return _pcall(
        body, name=name, out_shape=jax.ShapeDtypeStruct((N_DEV, m, n), F32),
        in_specs=[pl.BlockSpec(memory_space=pltpu.VMEM)], out_specs=pl.BlockSpec(memory_space=pltpu.VMEM),
        scratch_shapes=[pltpu.SemaphoreType.DMA((7,)), pltpu.SemaphoreType.DMA((7,)), pltpu.SemaphoreType.DMA],
        compiler_params=_params(),
    )(v)


def _gather_weights(shards):
    nw = len(shards)

    def body(*refs):
        ins, outs = refs[:nw], refs[nw:2 * nw]
        send_sems, recv_sems, local_sems = refs[2 * nw:]
        x, y, c, chips = _place()
        jme = 2 * x + y
        sibling = (x, y, 1 - c)

        def half(w, chip, hc):
            h = ins[w].shape[0] // 2
            return outs[w].at[2 * chip[0] + chip[1], pl.ds(pl.multiple_of(hc * h, SUBLANES * 2), h)]

        def copy(w, k, chip, hc, to, src=None):
            dst = half(w, chip, hc)
            return pltpu.make_async_remote_copy(
                src_ref=dst if src is None else src, dst_ref=dst,
                send_sem=send_sems.at[w, k], recv_sem=recv_sems.at[w, k], device_id=to, device_id_type=MESH)

        local, sent = [], []
        for w in range(nw):
            h = ins[w].shape[0] // 2
            mine = pltpu.make_async_copy(ins[w], outs[w].at[jme], local_sems.at[w])
            mine.start()
            local.append(mine)
            src = ins[w].at[pl.ds(pl.multiple_of(c * h, SUBLANES * 2), h)]
            for k, chip in enumerate(chips):
                cp = copy(w, k, (x, y), c, (*chip, c), src=src)
                cp.start()
                sent.append(cp)
        for w in range(nw):
            for k, chip in enumerate(chips):
                copy(w, k, chip, c, (x, y, c)).wait_recv()
                cp = copy(w, 3 + k, chip, c, sibling)
                cp.start()
                sent.append(cp)
        for w in range(nw):
            for k, chip in enumerate(chips):
                copy(w, 3 + k, chip, 1 - c, (x, y, c)).wait_recv()
        for cp in sent:
            cp.wait_send()
        for cp in local:
            cp.wait()

    any_spec = pl.BlockSpec(memory_space=pl.ANY)
    return _pcall(
        body, name="gather_weights",
        out_shape=[jax.ShapeDtypeStruct((N_CHIPS,) + s.shape, s.dtype) for s in shards],
        in_specs=[any_spec] * nw, out_specs=[any_spec] * nw,
        scratch_shapes=[pltpu.SemaphoreType.DMA((nw, 6)), pltpu.SemaphoreType.DMA((nw, 6)),
                        pltpu.SemaphoreType.DMA((nw,))],
        compiler_params=_params(),
    )(*shards)


def _pair_exchange(grads):
    nw = len(grads)

    def body(*refs):
        ins, outs = refs[:nw], refs[nw:2 * nw]
        send_sems, recv_sems, local_sems = refs[2 * nw:]
        x, y, c, _ = _place()
        sibling = (x, y, 1 - c)
        locs, rems = [], []
        for w in range(nw):
            lc = pltpu.make_async_copy(ins[w].at[:, c], outs[w].at[c], local_sems.at[w])
            lc.start()
            locs.append(lc)
            rc = pltpu.make_async_remote_copy(
                src_ref=ins[w].at[:, 1 - c], dst_ref=outs[w].at[c],
                send_sem=send_sems.at[w], recv_sem=recv_sems.at[w], device_id=sibling, device_id_type=MESH)
            rc.start()
            rems.append(rc)
        for w in range(nw):
            pltpu.make_async_remote_copy(
                src_ref=ins[w].at[:, c], dst_ref=outs[w].at[1 - c],
                send_sem=send_sems.at[w], recv_sem=recv_sems.at[w], device_id=sibling, device_id_type=MESH).wait_recv()
        for lc, rc in zip(locs, rems):
            rc.wait_send()
            lc.wait()

    any_spec = pl.BlockSpec(memory_space=pl.ANY)
    return _pcall(
        body, name="grad_pair_exchange",
        out_shape=[jax.ShapeDtypeStruct((2, N_CHIPS) + g.shape[2:], g.dtype) for g in grads],
        in_specs=[any_spec] * nw, out_specs=[any_spec] * nw,
        scratch_shapes=[pltpu.SemaphoreType.DMA((nw,)), pltpu.SemaphoreType.DMA((nw,)),
                        pltpu.SemaphoreType.DMA((nw,))],
        compiler_params=_params(),
    )(*grads)


def _chip_exchange(psums):
    nw = len(psums)

    def body(*refs):
        ins, outs = refs[:nw], refs[nw:2 * nw]
        send_sems, recv_sems, local_sems = refs[2 * nw:]
        x, y, c, chips = _place()
        jme = 2 * x + y
        started = []
        for w in range(nw):
            lc = pltpu.make_async_copy(ins[w].at[jme], outs[w].at[jme], local_sems.at[w])
            lc.start()
            started.append(lc)
            for k, chip in enumerate(chips):
                rc = pltpu.make_async_remote_copy(
                    src_ref=ins[w].at[2 * chip[0] + chip[1]], dst_ref=outs[w].at[jme],
                    send_sem=send_sems.at[w, k], recv_sem=recv_sems.at[w, k],
                    device_id=(*chip, c), device_id_type=MESH)
                rc.start()
                started.append(rc)
        for w in range(nw):
            for k, chip in enumerate(chips):
                pltpu.make_async_remote_copy(
                    src_ref=ins[w].at[jme], dst_ref=outs[w].at[2 * chip[0] + chip[1]],
                    send_sem=send_sems.at[w, k], recv_sem=recv_sems.at[w, k],
                    device_id=(*chip, c), device_id_type=MESH).wait_recv()
        for w in range(nw):
            started[4 * w].wait()
            for k in range(3):
                started[4 * w + 1 + k].wait_send()

    any_spec = pl.BlockSpec(memory_space=pl.ANY)
    return _pcall(
        body, name="grad_chip_exchange",
        out_shape=[jax.ShapeDtypeStruct(p.shape, p.dtype) for p in psums],
        in_specs=[any_spec] * nw, out_specs=[any_spec] * nw,
        scratch_shapes=[pltpu.SemaphoreType.DMA((nw, 3)), pltpu.SemaphoreType.DMA((nw, 3)),
                        pltpu.SemaphoreType.DMA((nw,))],
        compiler_params=_params(),
    )(*psums)


def _share_halves(halves):
    nw = len(halves)

    def body(*refs):
        ins, outs = refs[:nw], refs[nw:2 * nw]
        send_sems, recv_sems, local_sems = refs[2 * nw:]
        x, y, c, _ = _place()
        sibling = (x, y, 1 - c)
        locs, rems = [], []
        for w in range(nw):
            lc = pltpu.make_async_copy(ins[w], outs[w].at[c], local_sems.at[w])
            lc.start()
            locs.append(lc)
            rc = pltpu.make_async_remote_copy(
                src_ref=ins[w], dst_ref=outs[w].at[c],
                send_sem=send_sems.at[w], recv_sem=recv_sems.at[w], device_id=sibling, device_id_type=MESH)
            rc.start()
            rems.append(rc)
        for w in range(nw):
            pltpu.make_async_remote_copy(
                src_ref=ins[w], dst_ref=outs[w].at[1 - c],
                send_sem=send_sems.at[w], recv_sem=recv_sems.at[w], device_id=sibling, device_id_type=MESH).wait_recv()
        for lc, rc in zip(locs, rems):
            rc.wait_send()
            lc.wait()

    any_spec = pl.BlockSpec(memory_space=pl.ANY)
    return _pcall(
        body, name="grad_share_halves",
        out_shape=[jax.ShapeDtypeStruct((2,) + hv.shape, hv.dtype) for hv in halves],
        in_specs=[any_spec] * nw, out_specs=[any_spec] * nw,
        scratch_shapes=[pltpu.SemaphoreType.DMA((nw,)), pltpu.SemaphoreType.DMA((nw,)),
                        pltpu.SemaphoreType.DMA((nw,))],
        compiler_params=_params(),
    )(*halves)


def _pack(arrays):
    flat = jnp.concatenate([a.reshape(-1).astype(F32) for a in arrays])
    unit = 2 * SUBLANES * PACK_COLS
    pad = (-flat.shape[0]) % unit
    return jnp.pad(flat, (0, pad)).reshape(-1, PACK_COLS)


def _unpack(buf, shapes):
    flat, out, off = buf.reshape(-1), [], 0
    for s in shapes:
        n = math.prod(s)
        out.append(flat[off:off + n].reshape(s))
        off += n
    return out


def kernel(x, c, w_ada, b_ada, norm1_g, w_in, sinks, ssm_lam_re, ssm_lam_im, ssm_log_step, ssm_b_re, ssm_b_im, ssm_c_re, ssm_c_im, ssm_d, w_glu, b_glu, attn_out_g, ssm_out_g, w_out, norm2_g, w_ff1, w_ff2, final_g, loss_target, m_w_ada, m_b_ada, m_norm1_g, m_w_in, m_sinks, m_ssm_lam_re, m_ssm_lam_im, m_ssm_log_step, m_ssm_b_re, m_ssm_b_im, m_ssm_c_re, m_ssm_c_im, m_ssm_d, m_w_glu, m_b_glu, m_attn_out_g, m_ssm_out_g, m_w_out, m_norm2_g, m_w_ff1, m_w_ff2, m_final_g, v_w_ada, v_b_ada, v_norm1_g, v_w_in, v_sinks, v_ssm_lam_re, v_ssm_lam_im, v_ssm_log_step, v_ssm_b_re, v_ssm_b_im, v_ssm_c_re, v_ssm_c_im, v_ssm_d, v_w_glu, v_b_glu, v_attn_out_g, v_ssm_out_g, v_w_out, v_norm2_g, v_w_ff1, v_w_ff2, v_final_g):
    t, d = x.shape[1], x.shape[2]
    d_attn, d_ssm = attn_out_g.shape[1], ssm_d.shape[1]
    d_in = w_in.shape[2] * N_CHIPS
    d_kv = (d_in - d_attn - d_ssm) // 2
    n_q, n_kv = d_attn // HEAD_DIM, d_kv // HEAD_DIM
    n_grp = ssm_lam_re.shape[1]
    assert n_q == n_kv * Q_PER_KV and t % WINDOW == 0 and d_ssm == n_grp * SSM_GROUP
    assert d_kv % LANES == 0 and d_attn % d_kv == 0 and n_grp % GROUPS_PER_BLOCK == 0
    cw = GROUPS_PER_BLOCK * SSM_GROUP
    ucb0 = (d_attn + 2 * d_kv) // cw
    assert (d_attn + 2 * d_kv) % cw == 0
    xi, yi, ci = lax.axis_index("x"), lax.axis_index("y"), lax.axis_index("c")
    chip = 2 * xi + yi
    dev = 2 * chip + ci
    xs, tgt = x[0], loss_target[0]
    vec = lambda a: a.reshape(1, -1)

    n_ada = w_ada.shape[2]
    c_all = _allgather8(c.reshape(SUBLANES, d // SUBLANES), "gather_c").reshape(N_DEV, d)
    b_sh = lax.dynamic_slice_in_dim(b_ada, chip * n_ada, n_ada, axis=1)
    mod_sh, c_act = _ada_fwd(c_all, w_ada[0], b_sh)
    mod_all = _allgather8(mod_sh, "gather_mod")
    mod_me = lax.dynamic_index_in_dim(mod_all[0::2], dev, axis=1, keepdims=False)
    mod_me = mod_me.reshape(N_CHIPS * n_ada // d, 1, d)
    shift1, scale1, gate1, shift2, scale2, gate2 = [mod_me[i] for i in range(N_MOD)]

    win_s, wglu_s, wout_s, wff1_s, wff2_s = _gather_weights(
        [w_in[0].astype(BF16), w_glu[0].astype(BF16), w_out[0].astype(BF16), w_ff1[0].astype(BF16),
         w_ff2[0].astype(BF16)])
    wglu = wglu_s.reshape(d_ssm, d_ssm)
    wout = wout_s.reshape(d_attn + d_ssm, d)
    wff2 = wff2_s.reshape(-1, d)

    g3 = lambda a: a.reshape(n_grp, 1, STATE)
    lr3, li3 = g3(ssm_lam_re[0]), g3(ssm_lam_im[0])
    ls3 = jnp.broadcast_to(ssm_log_step[0].reshape(n_grp, 1, 1), (n_grp, 1, STATE))
    b_re3, b_im3 = ssm_b_re[0].transpose(0, 2, 1), ssm_b_im[0].transpose(0, 2, 1)
    a_re, a_im, bb_re, bb_im = _ssm_param_fwd(lr3, li3, ls3, b_re3, b_im3)
    ngb = n_grp // GROUPS_PER_BLOCK
    a_cat = jnp.concatenate([a_re.reshape(ngb, 1, -1), a_im.reshape(ngb, 1, -1)], axis=-1)
    bexp = jnp.concatenate([_block_diag_in(bb_re), _block_diag_in(bb_im)], axis=-1).astype(BF16)
    cexp = jnp.concatenate([_block_diag_in(ssm_c_re[0]), -_block_diag_in(ssm_c_im[0])], axis=-1)
    cexp = cexp.transpose(0, 2, 1).astype(BF16)

    half = HEAD_DIM // 2
    inv_freq = ROPE_THETA ** (-jnp.arange(half, dtype=F32) / half)
    ang = jnp.arange(t, dtype=F32)[:, None] * inv_freq[None, :]
    cos = jnp.tile(jnp.cos(ang), (1, LANES // half))
    sin = jnp.tile(jnp.sin(ang), (1, LANES // half))

    h = _norm_mod(xs, norm1_g, scale1, shift1)
    proj = _matmul(h, win_s, "nn", "mm_in", [F32], b_stacked=True, tn=win_s.shape[2])
    qr, kr, vb = _rope_fwd(proj, cos, sin, d_attn, d_kv)
    heads = lambda a, n: a.reshape(t, n, HEAD_DIM).transpose(1, 0, 2)
    unheads = lambda a: a.transpose(1, 0, 2).reshape(t, -1)
    qh, kh, vh = heads(qr, n_q), heads(kr, n_kv), heads(vb, n_kv)
    oh = _attn_fwd(qh, kh, vh, sinks[0])
    attn = unheads(oh)
    y0, z, states = _ssm_fwd(proj, ucb0, bexp, cexp, a_cat, ssm_d)
    gl = _matmul(z, wglu, "nn", "mm_glu", [F32])
    mixed = _mix(attn, y0, gl, b_glu, attn_out_g, ssm_out_g)
    mo = _matmul(mixed, wout, "nn", "mm_out", [F32])
    x1, h2 = _res_norm_mod(xs, mo, gate1, norm2_g, scale2, shift2)

    def relu2(acc):
        r = jnp.maximum(acc, 0.0)
        return acc, r * r

    a_act, rr = _matmul(h2, wff1_s, "nn", "mm_ff1", [BF16, BF16], epilogue=relu2, b_stacked=True)
    ff = _matmul(rr, wff2, "nn", "mm_ff2", [F32])
    dx2, dff, loss_cols, dgf, dgate2 = _final(x1, ff, tgt, gate2, vec(final_g))
    loss = lax.psum(0.5 * jnp.sum(loss_cols) / d, ("x", "y", "c"))

    d_relu2 = lambda acc, av: (acc * 2.0 * jnp.maximum(av.astype(F32), 0.0),)
    da = _matmul(dff, wff2, "nt", "mm_dff2", [BF16], epilogue=d_relu2, extras=(a_act,))
    gw_ff2 = _matmul(rr, dff, "tn", "mm_gw_ff2", [BF16])
    dh2 = _matmul(da, wff1_s, "nt", "mm_dff1", [F32], b_stacked=True)
    gw_ff1 = _matmul(h2, da, "tn", "mm_gw_ff1", [BF16], out_stacked=N_CHIPS)
    dx1, dmo, dshift2, dscale2, dg2, dgate1 = _bwd_norm2(x1, dh2, dx2, mo, norm2_g, scale2, gate1)
    dmixed = _matmul(dmo, wout, "nt", "mm_dout", [F32])
    gw_out = _matmul(mixed, dmo, "tn", "mm_gw_out", [BF16])
    dattn, dgl, dzp, dga, dgs, dbglu = _bwd_mix(dmixed, attn, y0, gl, b_glu, attn_out_g, ssm_out_g)
    d_gelu = lambda acc, dz, yv: ((acc + dz) * _gelu_grad(yv),)
    dy0 = _matmul(dgl, wglu, "nt", "mm_dglu", [F32], epilogue=d_gelu, extras=(dzp, y0))
    gw_glu = _matmul(z, dgl, "tn", "mm_gw_glu", [BF16])
    du, dbexp, dcexp, da_bar, dd = _ssm_bwd(dy0, proj, ucb0, states, bexp, cexp, a_cat, ssm_d)
    doh = heads(dattn, n_q)
    dqh, dkc, dkp, dvc, dvp, dsink = _attn_bwd(qh, kh, vh, oh, doh, sinks[0])
    up = lambda a: jnp.concatenate([unheads(a)[WINDOW:], jnp.zeros((WINDOW, d_kv), F32)], axis=0)
    dq, dk, dv = _rope_bwd(unheads(dqh), unheads(dkc), up(dkp), unheads(dvc), up(dvp), cos, sin)
    dproj = jnp.concatenate([dq, dk, dv, du], axis=1)
    dh = _matmul(dproj, win_s, "nt", "mm_din", [F32], b_stacked=True, tk=win_s.shape[2])
    gw_in = _matmul(h, dproj, "tn", "mm_gw_in", [BF16], out_stacked=N_CHIPS, tn=win_s.shape[2])
    grad_x, dshift1, dscale1, dg1 = _bwd_norm1(xs, dh, dx1, norm1_g, scale1)

    half_l = GROUPS_PER_BLOCK * STATE
    ga_re = da_bar[:, 0, :half_l].reshape(n_grp, 1, STATE)
    ga_im = da_bar[:, 0, half_l:].reshape(n_grp, 1, STATE)
    gbb_re = _block_diag_take(dbexp[:, :, :half_l], SSM_GROUP, STATE)
    gbb_im = _block_diag_take(dbexp[:, :, half_l:], SSM_GROUP, STATE)
    dcexp_t = dcexp.transpose(0, 2, 1)
    gc_re = _block_diag_take(dcexp_t[:, :, :half_l], SSM_GROUP, STATE)
    gc_im = -_block_diag_take(dcexp_t[:, :, half_l:], SSM_GROUP, STATE)
    dmod = jnp.concatenate([dshift1, dscale1, dgate1, dshift2, dscale2, dgate2], axis=1)
    dsinks = dsink.reshape(n_q, WINDOW).sum(axis=1)
    pieces = [dmod, dg1, dsinks, ga_re, ga_im, gbb_re, gbb_im, gc_re, gc_im, dd, dbglu, dga, dgs, dg2, dgf]
    gathered = _allgather8(_pack(pieces), "gather_small")
    summed = _sum_leading(gathered, F32, "sum_small")
    (g_b_ada, g_norm1, g_sinks, ga_re, ga_im, gbb_re, gbb_im, g_c_re, g_c_im, g_d, g_b_glu, g_attn_g, g_ssm_g,
     g_norm2, g_final) = _unpack(summed, [p.shape for p in pieces])
    g_lr, g_li, g_ls, g_b_re3, g_b_im3 = _ssm_param_bwd(lr3, li3, ls3, b_re3, b_im3, ga_re, ga_im, gbb_re, gbb_im)
    small_grads = [
        g_b_ada, g_norm1, g_sinks.reshape(1, -1), g_lr.reshape(1, n_grp, STATE), g_li.reshape(1, n_grp, STATE),
        g_ls[:, 0, 0].reshape(1, n_grp), g_b_re3.transpose(0, 2, 1)[None], g_b_im3.transpose(0, 2, 1)[None],
        g_c_re[None], g_c_im[None], g_d, g_b_glu, g_attn_g, g_ssm_g, g_norm2, g_final.reshape(-1)]
    small_w = [b_ada, norm1_g, sinks, ssm_lam_re, ssm_lam_im, ssm_log_step, ssm_b_re, ssm_b_im, ssm_c_re,
               ssm_c_im, ssm_d, b_glu, attn_out_g, ssm_out_g, norm2_g, final_g]
    small_m = [m_b_ada, m_norm1_g, m_sinks, m_ssm_lam_re, m_ssm_lam_im, m_ssm_log_step, m_ssm_b_re, m_ssm_b_im,
               m_ssm_c_re, m_ssm_c_im, m_ssm_d, m_b_glu, m_attn_out_g, m_ssm_out_g, m_norm2_g, m_final_g]
    small_v = [v_b_ada, v_norm1_g, v_sinks, v_ssm_lam_re, v_ssm_lam_im, v_ssm_log_step, v_ssm_b_re, v_ssm_b_im,
               v_ssm_c_re, v_ssm_c_im, v_ssm_d, v_b_glu, v_attn_out_g, v_ssm_out_g, v_norm2_g, v_final_g]
    small_grads = [g.reshape(w.shape) for g, w in zip(small_grads, small_w)]
    s_delta, s_m, s_v = _adamw(_pack(small_w), _pack(small_grads), _pack(small_m), _pack(small_v), "adamw_small")
    shapes = [w.shape for w in small_w]
    s_delta, s_m, s_v = _unpack(s_delta, shapes), _unpack(s_m, shapes), _unpack(s_v, shapes)

    dmod_rows = gathered.reshape(N_DEV, -1)[:, :dmod.shape[1]]
    dmod_sh = lax.dynamic_slice_in_dim(dmod_rows, chip * n_ada, n_ada, axis=1)
    g_w_ada = _matmul(c_act, dmod_sh, "tn", "mm_gw_ada", [F32], tk=N_DEV, precision=lax.Precision.HIGHEST)

    big_g = [gw_in, gw_glu, gw_out, gw_ff1, gw_ff2]
    big_w = [w_in[0], w_glu[0], w_out[0], w_ff1[0], w_ff2[0]]
    views = [g.reshape(N_CHIPS, 2, w.shape[0] // 2, w.shape[1]) for g, w in zip(big_g, big_w)]
    pairs = _pair_exchange(views)
    psums = [_sum_leading(p.reshape(2, -1, p.shape[-1]), BF16, f"pair_sum_{i}").reshape(p.shape[1:])
             for i, p in enumerate(pairs)]
    recvd = _chip_exchange(psums)
    halves = [_sum_leading(r, F32, f"chip_sum_{i}") for i, r in enumerate(recvd)]
    big_grads = [s.reshape(w.shape) for s, w in zip(_share_halves(halves), big_w)]

    big_names = ["w_ada", "w_in", "w_glu", "w_out", "w_ff1", "w_ff2"]
    big_w = [w_ada[0]] + big_w
    big_grads = [g_w_ada] + big_grads
    big_m = [m_w_ada[0], m_w_in[0], m_w_glu[0], m_w_out[0], m_w_ff1[0], m_w_ff2[0]]
    big_v = [v_w_ada[0], v_w_in[0], v_w_glu[0], v_w_out[0], v_w_ff1[0], v_w_ff2[0]]
    big_upd = {n: _adamw(w, g, m, v, "adamw_" + n) for n, w, g, m, v in zip(big_names, big_w, big_grads, big_m, big_v)}
    big_grad = dict(zip(big_names, big_grads))

    order = ["w_ada", "b_ada", "norm1_g", "w_in", "sinks", "ssm_lam_re", "ssm_lam_im", "ssm_log_step", "ssm_b_re",
             "ssm_b_im", "ssm_c_re", "ssm_c_im", "ssm_d", "w_glu", "b_glu", "attn_out_g", "ssm_out_g", "w_out",
             "norm2_g", "w_ff1", "w_ff2", "final_g"]
    small_names = [n for n in order if n not in big_names]
    grads, deltas, new_m, new_v = {}, {}, {}, {}
    for i, n in enumerate(small_names):
        grads[n], deltas[n], new_m[n], new_v[n] = small_grads[i], s_delta[i], s_m[i], s_v[i]
    for n in big_names:
        grads[n] = big_grad[n][None]
        deltas[n], new_m[n], new_v[n] = [a[None] for a in big_upd[n]]
    return (loss, grad_x[None], *[grads[n] for n in order], *[deltas[n] for n in order],
            *[new_m[n] for n in order], *[new_v[n] for n in order])
```

```python
import functools
import math

import jax
import jax.numpy as jnp
from jax import lax
from jax.experimental import pallas as pl
from jax.experimental.pallas import tpu as pltpu

F32 = jnp.float32
BF16 = jnp.bfloat16
MESH = pl.DeviceIdType.MESH

EPS = 1e-6
HEAD_DIM = 64
Q_PER_KV = 8
WINDOW = 128
ROPE_THETA = 10000.0
SSM_GROUP = 16
STATE = 64
GROUPS_PER_BLOCK = 16
N_MOD = 6
N_CHIPS = 4
N_DEV = 8
ADAM_LR = 0.001
ADAM_B1 = 0.9
ADAM_B2 = 0.999
ADAM_EPS = 1e-08
ADAM_WD = 0.01
ADAM_STEP = 10
LANES = 128
SUBLANES = 8
VMEM_LIMIT = 56 * 1024 * 1024
PACK_COLS = 512


def _pcall(body, **kw):
    return pl.pallas_call(body, **kw)


def _params(sem=None):
    return pltpu.CompilerParams(dimension_semantics=sem, vmem_limit_bytes=VMEM_LIMIT)


def _tile(n, want, unit):
    if n <= want:
        return n
    t = (want // unit) * unit
    while t > unit and n % t:
        t -= unit
    assert n % t == 0, (n, want, unit)
    return t


_NN = (((1,), (0,)), ((), ()))
_NT = (((1,), (1,)), ((), ()))
_TN = (((0,), (0,)), ((), ()))


def _matmul(a, b, mode, name, out_dtypes, epilogue=None, extras=(), b_stacked=False, out_stacked=0,
            tm=1024, tn=1024, tk=2048, precision=None, comm=None):
    if mode == "nn":
        m, kdim = a.shape
        if b_stacked:
            s, _, nsh = b.shape
            n = s * nsh
            tn = _tile(nsh, tn, LANES)
        else:
            n = b.shape[1]
            tn = _tile(n, tn, LANES)
        tm, tk = _tile(m, tm, SUBLANES * 2), _tile(kdim, tk, LANES)
        a_spec = pl.BlockSpec((tm, tk), lambda i, j, k: (i, k))
        if b_stacked:
            npb = nsh // tn
            b_spec = pl.BlockSpec((None, tk, tn), lambda i, j, k: (j // npb, k, j % npb))
        else:
            b_spec = pl.BlockSpec((tk, tn), lambda i, j, k: (k, j))
        dims = _NN
    elif mode == "nt":
        m, kdim = a.shape
        if b_stacked:
            s, n, ksh = b.shape
            tk = _tile(ksh, tk, LANES)
            kpb = ksh // tk
            tn = _tile(n, tn, LANES)
            b_spec = pl.BlockSpec((None, tn, tk), lambda i, j, k: (k // kpb, j, k % kpb))
        else:
            n = b.shape[0]
            tk = _tile(kdim, tk, LANES)
            tn = _tile(n, tn, LANES)
            b_spec = pl.BlockSpec((tn, tk), lambda i, j, k: (j, k))
        tm = _tile(m, tm, SUBLANES * 2)
        a_spec = pl.BlockSpec((tm, tk), lambda i, j, k: (i, k))
        dims = _NT
    else:
        kdim, m = a.shape
        n = b.shape[1]
        tm = _tile(m, tm, LANES)
        tk = _tile(kdim, tk, SUBLANES * 2)
        if out_stacked:
            nsh = n // out_stacked
            tn = _tile(nsh, tn, LANES)
        else:
            tn = _tile(n, tn, LANES)
        a_spec = pl.BlockSpec((tk, tm), lambda i, j, k: (k, i))
        b_spec = pl.BlockSpec((tk, tn), lambda i, j, k: (k, j))
        dims = _TN
    nk = kdim // tk
    grid = (m // tm, n // tn, nk)
    if out_stacked:
        npo = (n // out_stacked) // tn
        o_spec = pl.BlockSpec((None, tm, tn), lambda i, j, k: (j // npo, i, j % npo))
        out_shape = [jax.ShapeDtypeStruct((out_stacked, m, n // out_stacked), dt) for dt in out_dtypes]
    else:
        o_spec = pl.BlockSpec((tm, tn), lambda i, j, k: (i, j))
        out_shape = [jax.ShapeDtypeStruct((m, n), dt) for dt in out_dtypes]
    x_spec = pl.BlockSpec((tm, tn), lambda i, j, k: (i, j))
    n_ex, n_out = len(extras), len(out_dtypes)
    n_ci = len(comm.ins) if comm else 0
    n_co = len(comm.outs) if comm else 0

    def body(a_ref, b_ref, *rest):
        ex_refs, rest = rest[:n_ex], rest[n_ex:]
        ci_refs, rest = rest[:n_ci], rest[n_ci:]
        out_refs, rest = rest[:n_out], rest[n_out:]
        co_refs, rest = rest[:n_co], rest[n_co:]
        acc_ref = rest[0]
        i, j, k = pl.program_id(0), pl.program_id(1), pl.program_id(2)

        if comm:
            @pl.when((i == 0) & (j == 0) & (k == 0))
            def _():
                comm.start(ci_refs, co_refs, rest[1], rest[2])

        def finish(acc):
            outs = (acc,) if epilogue is None else epilogue(acc, *[r[...] for r in ex_refs])
            for r, o in zip(out_refs, outs):
                r[...] = o.astype(r.dtype)

        part = lax.dot_general(a_ref[...], b_ref[...], dims, precision=precision, preferred_element_type=F32)
        if nk == 1:
            finish(part)
        else:
            @pl.when(k == 0)
            def _():
                acc_ref[...] = part

            @pl.when(k > 0)
            def _():
                acc_ref[...] += part

            @pl.when(k == nk - 1)
            def _():
                finish(acc_ref[...])

        if comm:
            @pl.when((i == grid[0] - 1) & (j == grid[1] - 1) & (k == nk - 1))
            def _():
                comm.finish(ci_refs, co_refs, rest[1], rest[2])

    any_spec = pl.BlockSpec(memory_space=pl.ANY)
    scratch = [pltpu.VMEM((tm, tn) if nk > 1 else (SUBLANES, LANES), F32)]
    aliases = {}
    if comm:
        scratch += [pltpu.SemaphoreType.DMA((comm.n_sems,)), pltpu.SemaphoreType.DMA((comm.n_sems,))]
        aliases = {2 + n_ex + ci: n_out + co for ci, co in comm.aliases.items()}
    res = _pcall(
        body, name=name, grid=grid,
        in_specs=[a_spec, b_spec] + [x_spec] * n_ex + [any_spec] * n_ci,
        out_specs=[o_spec] * n_out + [any_spec] * n_co,
        out_shape=out_shape + (list(comm.outs) if comm else []),
        input_output_aliases=aliases, scratch_shapes=scratch,
        compiler_params=_params(("arbitrary",) * 3 if comm else ("parallel", "parallel", "arbitrary")),
    )(a, b, *extras, *(comm.ins if comm else ()))
    main = res[0] if n_out == 1 else tuple(res[:n_out])
    return (main, tuple(res[n_out:])) if comm else main


def _rowwise(body, name, rows, row_ins, vec_ins, row_outs, acc_outs, tr=128):
    tr = _tile(rows, tr, SUBLANES * 2)
    n_ri, n_vi, n_ro, n_ao = len(row_ins), len(vec_ins), len(row_outs), len(acc_outs)

    def kern(*refs):
        ri, vi = refs[:n_ri], refs[n_ri:n_ri + n_vi]
        ro = refs[n_ri + n_vi:n_ri + n_vi + n_ro]
        ao = refs[n_ri + n_vi + n_ro:]

        @pl.when(pl.program_id(0) == 0)
        def _():
            for r in ao:
                r[...] = jnp.zeros_like(r)

        body(ri, vi, ro, ao)

    in_specs = [pl.BlockSpec((tr, w), functools.partial(lambda i, cb: (i, cb), cb=cb)) for _, w, cb in row_ins]
    in_specs += [pl.BlockSpec(v.shape, lambda i: (0, 0)) for v in vec_ins]
    out_specs = [pl.BlockSpec((tr, w), lambda i: (i, 0)) for w, _ in row_outs]
    out_specs += [pl.BlockSpec((1, w), lambda i: (0, 0)) for w in acc_outs]
    out_shape = [jax.ShapeDtypeStruct((rows, w), dt) for w, dt in row_outs]
    out_shape += [jax.ShapeDtypeStruct((1, w), F32) for w in acc_outs]
    return _pcall(
        kern, name=name, grid=(rows // tr,), in_specs=in_specs, out_specs=out_specs, out_shape=out_shape,
        compiler_params=_params(("arbitrary",)),
    )(*[a for a, _, _ in row_ins], *vec_ins)


def _colsum(x):
    return jnp.sum(x, axis=0, keepdims=True)


def _rstd(x):
    return lax.rsqrt(jnp.mean(x * x, axis=-1, keepdims=True) + EPS)


def _norm_bwd(dxn, xn, r):
    return r * (dxn - xn * jnp.mean(dxn * xn, axis=-1, keepdims=True))


_SQRT_HALF = math.sqrt(0.5)
_INV_SQRT_2PI = 1.0 / math.sqrt(2.0 * math.pi)


def _gelu(y):
    return 0.5 * y * (1.0 + lax.erf(y * _SQRT_HALF))


def _gelu_grad(y):
    return 0.5 * (1.0 + lax.erf(y * _SQRT_HALF)) + y * jnp.exp(-0.5 * y * y) * _INV_SQRT_2PI


def _norm_mod(x, g, scale, shift):
    def body(ri, vi, ro, ao):
        xv = ri[0][...]
        h = xv * _rstd(xv) * vi[0][...] * (1.0 + vi[1][...]) + vi[2][...]
        ro[0][...] = h.astype(BF16)

    d = x.shape[1]
    return _rowwise(body, "norm_mod", x.shape[0], [(x, d, 0)], [g, scale, shift], [(d, BF16)], [])[0]


def _res_norm_mod(x, mo, gate, g, scale, shift):
    def body(ri, vi, ro, ao):
        x1 = ri[0][...] + vi[0][...] * ri[1][...]
        ro[0][...] = x1
        ro[1][...] = (x1 * _rstd(x1) * vi[1][...] * (1.0 + vi[2][...]) + vi[3][...]).astype(BF16)

    d = x.shape[1]
    return _rowwise(body, "res_norm_mod", x.shape[0], [(x, d, 0), (mo, d, 0)], [gate, g, scale, shift],
                    [(d, F32), (d, BF16)], [])


def _mix(attn, y0, gl, b_glu, ga, gs):
    da, ds = attn.shape[1], y0.shape[1]

    def body(ri, vi, ro, ao):
        at = ri[0][...]
        z = _gelu(ri[1][...])
        o = z * jax.nn.sigmoid(ri[2][...] + vi[0][...])
        ro[0][:, :da] = (at * _rstd(at) * vi[1][...]).astype(BF16)
        ro[0][:, da:] = (o * _rstd(o) * vi[2][...]).astype(BF16)

    return _rowwise(body, "mix", attn.shape[0], [(attn, da, 0), (y0, ds, 0), (gl, ds, 0)], [b_glu, ga, gs],
                    [(da + ds, BF16)], [])[0]


def _final(x1, ff, tgt, gate2, gf):
    d = x1.shape[1]

    def body(ri, vi, ro, ao):
        ffv = ri[1][...]
        x2 = ri[0][...] + vi[0][...] * ffv
        r = _rstd(x2)
        xn = x2 * r
        e = xn * vi[1][...] - ri[2][...]
        ao[0][...] += _colsum(e * e)
        dy = e * (1.0 / d)
        ao[1][...] += _colsum(dy * xn)
        dx2 = _norm_bwd(dy * vi[1][...], xn, r)
        ao[2][...] += _colsum(dx2 * ffv)
        ro[0][...] = dx2
        ro[1][...] = (dx2 * vi[0][...]).astype(BF16)

    return _rowwise(body, "final", x1.shape[0], [(x1, d, 0), (ff, d, 0), (tgt, d, 0)], [gate2, gf],
                    [(d, F32), (d, BF16)], [d, d, d])


def _bwd_norm2(x1, dh2, dx2, mo, g2, scale2, gate1):
    d = x1.shape[1]

    def body(ri, vi, ro, ao):
        xv, dh = ri[0][...], ri[1][...]
        r = _rstd(xv)
        xn = xv * r
        ao[0][...] += _colsum(dh)
        ao[1][...] += _colsum(dh * xn * vi[0][...])
        dn = dh * (1.0 + vi[1][...])
        ao[2][...] += _colsum(dn * xn)
        dx1 = ri[2][...] + _norm_bwd(dn * vi[0][...], xn, r)
        ao[3][...] += _colsum(dx1 * ri[3][...])
        ro[0][...] = dx1
        ro[1][...] = (dx1 * vi[2][...]).astype(BF16)

    return _rowwise(body, "bwd_norm2", x1.shape[0], [(x1, d, 0), (dh2, d, 0), (dx2, d, 0), (mo, d, 0)],
                    [g2, scale2, gate1], [(d, F32), (d, BF16)], [d, d, d, d])


def _bwd_mix(dmixed, attn, y0, gl, b_glu, ga, gs):
    da, ds = attn.shape[1], y0.shape[1]

    def body(ri, vi, ro, ao):
        dan, dsn = ri[0][:, :da], ri[0][:, da:]
        at = ri[1][...]
        ra = _rstd(at)
        an = at * ra
        ao[0][...] += _colsum(dan * an)
        ro[0][...] = _norm_bwd(dan * vi[1][...], an, ra)
        z = _gelu(ri[2][...])
        sg = jax.nn.sigmoid(ri[3][...] + vi[0][...])
        o = z * sg
        rs = _rstd(o)
        on = o * rs
        ao[1][...] += _colsum(dsn * on)
        do = _norm_bwd(dsn * vi[2][...], on, rs)
        ro[2][...] = do * sg
        dgl = do * z * sg * (1.0 - sg)
        ao[2][...] += _colsum(dgl)
        ro[1][...] = dgl.astype(BF16)

    return _rowwise(body, "bwd_mix", attn.shape[0],
                    [(dmixed, da + ds, 0), (attn, da, 0), (y0, ds, 0), (gl, ds, 0)], [b_glu, ga, gs],
                    [(da, F32), (ds, BF16), (ds, F32)], [da, ds, ds])


def _bwd_norm1(x, dh, dx1, g1, scale1):
    d = x.shape[1]

    def body(ri, vi, ro, ao):
        xv, dhv = ri[0][...], ri[1][...]
        r = _rstd(xv)
        xn = xv * r
        ao[0][...] += _colsum(dhv)
        ao[1][...] += _colsum(dhv * xn * vi[0][...])
        dn = dhv * (1.0 + vi[1][...])
        ao[2][...] += _colsum(dn * xn)
        ro[0][...] = ri[2][...] + _norm_bwd(dn * vi[0][...], xn, r)

    return _rowwise(body, "bwd_norm1", x.shape[0], [(x, d, 0), (dh, d, 0), (dx1, d, 0)], [g1, scale1],
                    [(d, F32)], [d, d, d])


def _rope_apply(x, cos, sin, sign):
    first = (lax.broadcasted_iota(jnp.int32, cos.shape, 1) % HEAD_DIM) < (HEAD_DIM // 2)
    outs = []
    for j in range(x.shape[1] // LANES):
        xc = x[:, j * LANES:(j + 1) * LANES]
        rot = jnp.where(first, -pltpu.roll(xc, LANES - HEAD_DIM // 2, 1), pltpu.roll(xc, HEAD_DIM // 2, 1))
        outs.append(xc * cos + sign * (rot * sin))
    return outs


def _rope_fwd(proj, cos, sin, d_attn, d_kv):
    scale = HEAD_DIM ** -0.5
    kcb, vcb = d_attn // d_kv, d_attn // d_kv + 1

    def body(ri, vi, ro, ao):
        c, s = ri[3][...], ri[4][...]
        for j, o in enumerate(_rope_apply(ri[0][...], c, s, 1.0)):
            ro[0][:, j * LANES:(j + 1) * LANES] = (o * scale).astype(BF16)
        for j, o in enumerate(_rope_apply(ri[1][...], c, s, 1.0)):
            ro[1][:, j * LANES:(j + 1) * LANES] = o.astype(BF16)
        ro[2][...] = ri[2][...].astype(BF16)

    return _rowwise(body, "rope_fwd", proj.shape[0],
                    [(proj, d_attn, 0), (proj, d_kv, kcb), (proj, d_kv, vcb), (cos, LANES, 0), (sin, LANES, 0)], [],
                    [(d_attn, BF16), (d_kv, BF16), (d_kv, BF16)], [])


def _rope_bwd(dqr, dkc, dkp, dvc, dvp, cos, sin):
    scale = HEAD_DIM ** -0.5
    d_attn, d_kv = dqr.shape[1], dkc.shape[1]

    def body(ri, vi, ro, ao):
        c, s = ri[5][...], ri[6][...]
        for j, o in enumerate(_rope_apply(ri[0][...], c, s, -1.0)):
            ro[0][:, j * LANES:(j + 1) * LANES] = (o * scale).astype(BF16)
        for j, o in enumerate(_rope_apply(ri[1][...] + ri[2][...], c, s, -1.0)):
            ro[1][:, j * LANES:(j + 1) * LANES] = o.astype(BF16)
        ro[2][...] = (ri[3][...] + ri[4][...]).astype(BF16)

    return _rowwise(body, "rope_bwd", dqr.shape[0],
                    [(dqr, d_attn, 0), (dkc, d_kv, 0), (dkp, d_kv, 0), (dvc, d_kv, 0), (dvp, d_kv, 0),
                     (cos, LANES, 0), (sin, LANES, 0)], [],
                    [(d_attn, BF16), (d_kv, BF16), (d_kv, BF16)], [])


def _attn_probs(q, k, sink_ref, g, n):
    rows = Q_PER_KV * WINDOW
    s = lax.dot_general(q, k, _NT, preferred_element_type=F32)
    qi = lax.broadcasted_iota(jnp.int32, (rows, 2 * WINDOW), 0) % WINDOW + WINDOW
    kj = lax.broadcasted_iota(jnp.int32, (rows, 2 * WINDOW), 1)
    rel = qi - kj
    mask = (rel >= 0) & (rel < WINDOW) & ((n > 0) | (kj >= WINDOW))
    s = jnp.where(mask, s, -1e30)
    sink = jnp.concatenate([jnp.full((WINDOW, 1), sink_ref[g * Q_PER_KV + j], F32) for j in range(Q_PER_KV)], axis=0)
    m = jnp.maximum(jnp.max(s, axis=-1, keepdims=True), sink)
    p = jnp.exp(s - m)
    es = jnp.exp(sink - m)
    l = jnp.sum(p, axis=-1, keepdims=True) + es
    return p, l, es


def _attn_specs(n_kv):
    qspec = pl.BlockSpec((Q_PER_KV, WINDOW, HEAD_DIM), lambda g, n: (g, n, 0))
    cur = pl.BlockSpec((1, WINDOW, HEAD_DIM), lambda g, n: (g, n, 0))
    prev = pl.BlockSpec((1, WINDOW, HEAD_DIM), lambda g, n: (g, jnp.maximum(n - 1, 0), 0))
    return qspec, cur, prev


def _attn_fwd(q, k, v, sinks):
    n_kv, t = k.shape[0], k.shape[1]
    rows = Q_PER_KV * WINDOW

    def body(sink_ref, q_ref, kp_ref, kc_ref, vp_ref, vc_ref, o_ref):
        g, n = pl.program_id(0), pl.program_id(1)
        qv = q_ref[...].reshape(rows, HEAD_DIM)
        kv = jnp.concatenate([kp_ref[0], kc_ref[0]], axis=0)
        vv = jnp.concatenate([vp_ref[0], vc_ref[0]], axis=0)
        p, l, _ = _attn_probs(qv, kv, sink_ref, g, n)
        o = jnp.dot(p.astype(BF16), vv, preferred_element_type=F32) / l
        o_ref[...] = o.reshape(Q_PER_KV, WINDOW, HEAD_DIM)

    qspec, cur, prev = _attn_specs(n_kv)
    return _pcall(
        body, name="attn_fwd", grid=(n_kv, t // WINDOW),
        in_specs=[pl.BlockSpec(memory_space=pltpu.SMEM), qspec, prev, cur, prev, cur],
        out_specs=qspec, out_shape=jax.ShapeDtypeStruct(q.shape, F32),
        compiler_params=_params(("parallel", "arbitrary")),
    )(sinks, q, k, k, v, v)


def _attn_bwd(q, k, v, o, do, sinks):
    n_kv, t = k.shape[0], k.shape[1]
    rows = Q_PER_KV * WINDOW

    def body(sink_ref, q_ref, kp_ref, kc_ref, vp_ref, vc_ref, o_ref, do_ref,
             dq_ref, dkc_ref, dkp_ref, dvc_ref, dvp_ref, ds_ref):
        g, n = pl.program_id(0), pl.program_id(1)
        qv = q_ref[...].reshape(rows, HEAD_DIM)
        kv = jnp.concatenate([kp_ref[0], kc_ref[0]], axis=0)
        vv = jnp.concatenate([vp_ref[0], vc_ref[0]], axis=0)
        p, l, es = _attn_probs(qv, kv, sink_ref, g, n)
        inv_l = 1.0 / l
        pn = p * inv_l
        dov = do_ref[...].reshape(rows, HEAD_DIM)
        delta = jnp.sum(dov * o_ref[...].reshape(rows, HEAD_DIM), axis=-1, keepdims=True)
        dob = dov.astype(BF16)
        dv = lax.dot_general(pn.astype(BF16), dob, _TN, preferred_element_type=F32)
        dp = lax.dot_general(dob, vv, _NT, preferred_element_type=F32)
        dsb = (pn * (dp - delta)).astype(BF16)
        dq_ref[...] = jnp.dot(dsb, kv, preferred_element_type=F32).reshape(Q_PER_KV, WINDOW, HEAD_DIM)
        dk = lax.dot_general(dsb, qv, _TN, preferred_element_type=F32)
        dkp_ref[0] = dk[:WINDOW]
        dkc_ref[0] = dk[WINDOW:]
        dvp_ref[0] = dv[:WINDOW]
        dvc_ref[0] = dv[WINDOW:]

        @pl.when(n == 0)
        def _():
            ds_ref[...] = jnp.zeros_like(ds_ref)

        ds_ref[...] += -(es * inv_l) * delta

    qspec, cur, prev = _attn_specs(n_kv)
    sspec = pl.BlockSpec((None, rows, 1), lambda g, n: (g, 0, 0))
    kshape = jax.ShapeDtypeStruct(k.shape, F32)
    return _pcall(
        body, name="attn_bwd", grid=(n_kv, t // WINDOW),
        in_specs=[pl.BlockSpec(memory_space=pltpu.SMEM), qspec, prev, cur, prev, cur, qspec, qspec],
        out_specs=[qspec, cur, cur, cur, cur, sspec],
        out_shape=[jax.ShapeDtypeStruct(q.shape, F32), kshape, kshape, kshape, kshape,
                   jax.ShapeDtypeStruct((n_kv, rows, 1), F32)],
        compiler_params=_params(("parallel", "arbitrary")),
    )(sinks, q, k, k, v, v, o, do)


def _cmul(ar, ai, br, bi):
    return ar * br - ai * bi, ar * bi + ai * br


def _scan_consts(ar, ai, half, reverse):
    row = lax.broadcasted_iota(jnp.int32, (SUBLANES, half), 0)
    a2 = _cmul(ar, ai, ar, ai)
    a4 = _cmul(*a2, *a2)
    steps = [(1, ar, ai), (2, *a2), (4, *a4)]
    pr, pi = ar, ai
    pwr = jnp.zeros((SUBLANES, half), F32)
    pwi = jnp.zeros((SUBLANES, half), F32)
    for r in range(SUBLANES):
        sel = row == (SUBLANES - 1 - r if reverse else r)
        pwr = jnp.where(sel, pr, pwr)
        pwi = jnp.where(sel, pi, pwi)
        pr, pi = _cmul(pr, pi, ar, ai)
    return row, steps, pwr, pwi


def _scan8(xr, xi, row, steps, pwr, pwi, cr, ci, reverse):
    for d, er, ei in steps:
        if reverse:
            keep, shift = row < SUBLANES - d, SUBLANES - d
        else:
            keep, shift = row >= d, d
        sr = jnp.where(keep, pltpu.roll(xr, shift, 0), 0.0)
        si = jnp.where(keep, pltpu.roll(xi, shift, 0), 0.0)
        tr, ti = _cmul(er, ei, sr, si)
        xr, xi = xr + tr, xi + ti
    tr, ti = _cmul(pwr, pwi, cr, ci)
    return xr + tr, xi + ti


def _ssm_fwd(proj, ucb0, bexp, cexp, a_cat, d_skip, tt=512):
    t = proj.shape[0]
    ngb, cw, two_l = bexp.shape
    half = two_l // 2
    tt = _tile(t, tt, SUBLANES * 2)
    nt = t // tt

    def body(u_ref, b_ref, c_ref, a_ref, d_ref, y_ref, z_ref, st_ref, carry_ref):
        @pl.when(pl.program_id(1) == 0)
        def _():
            carry_ref[...] = jnp.zeros_like(carry_ref)

        u = u_ref[...]
        st_ref[...] = jnp.dot(u.astype(BF16), b_ref[...], preferred_element_type=F32)
        ar, ai = a_ref[:, :half], a_ref[:, half:]
        row, steps, pwr, pwi = _scan_consts(ar, ai, half, False)

        def tile(i, carry):
            base = pl.multiple_of(i * SUBLANES, SUBLANES)
            xr, xi = _scan8(st_ref[pl.ds(base, SUBLANES), :half], st_ref[pl.ds(base, SUBLANES), half:],
                            row, steps, pwr, pwi, carry[0], carry[1], False)
            st_ref[pl.ds(base, SUBLANES), :half] = xr
            st_ref[pl.ds(base, SUBLANES), half:] = xi
            return xr[SUBLANES - 1:, :], xi[SUBLANES - 1:, :]

        cr, ci = lax.fori_loop(0, tt // SUBLANES, tile, (carry_ref[0:1, :half], carry_ref[0:1, half:]))
        carry_ref[0:1, :half] = cr
        carry_ref[0:1, half:] = ci
        y = jnp.dot(st_ref[...].astype(BF16), c_ref[...], preferred_element_type=F32) + d_ref[...] * u
        y_ref[...] = y
        z_ref[...] = _gelu(y).astype(BF16)

    d_ssm = ngb * cw
    return _pcall(
        body, name="ssm_fwd", grid=(ngb, nt),
        in_specs=[pl.BlockSpec((tt, cw), lambda g, i: (i, ucb0 + g)),
                  pl.BlockSpec((None, cw, two_l), lambda g, i: (g, 0, 0)),
                  pl.BlockSpec((None, two_l, cw), lambda g, i: (g, 0, 0)),
                  pl.BlockSpec((None, 1, two_l), lambda g, i: (g, 0, 0)),
                  pl.BlockSpec((1, cw), lambda g, i: (0, g))],
        out_specs=[pl.BlockSpec((tt, cw), lambda g, i: (i, g)),
                   pl.BlockSpec((tt, cw), lambda g, i: (i, g)),
                   pl.BlockSpec((tt, two_l), lambda g, i: (i, g))],
        out_shape=[jax.ShapeDtypeStruct((t, d_ssm), F32), jax.ShapeDtypeStruct((t, d_ssm), BF16),
                   jax.ShapeDtypeStruct((t, ngb * two_l), F32)],
        scratch_shapes=[pltpu.VMEM((SUBLANES, two_l), F32)],
        compiler_params=_params(("parallel", "arbitrary")),
    )(proj, bexp, cexp, a_cat, d_skip)


def _ssm_bwd(dy0, proj, ucb0, states, bexp, cexp, a_cat, d_skip, tt=512):
    t = dy0.shape[0]
    ngb, cw, two_l = bexp.shape
    half = two_l // 2
    tt = _tile(t, tt, SUBLANES * 2)
    nt = t // tt

    def body(dy_ref, u_ref, st_ref, b_ref, c_ref, a_ref, d_ref,
             du_ref, db_ref, dc_ref, da_ref, dd_ref, lam_ref, carry_ref, acc_ref):
        step = pl.program_id(1)

        @pl.when(step == 0)
        def _():
            carry_ref[...] = jnp.zeros_like(carry_ref)
            acc_ref[...] = jnp.zeros_like(acc_ref)
            db_ref[...] = jnp.zeros_like(db_ref)
            dc_ref[...] = jnp.zeros_like(dc_ref)
            dd_ref[...] = jnp.zeros_like(dd_ref)

        dy, u = dy_ref[...], u_ref[...]
        dyb = dy.astype(BF16)
        lam_ref[...] = lax.dot_general(dyb, c_ref[...], _NT, preferred_element_type=F32)
        ar, ai = a_ref[:, :half], -a_ref[:, half:]
        row, steps, pwr, pwi = _scan_consts(ar, ai, half, True)
        last = row == SUBLANES - 1

        def tile(i, carry):
            cr, ci, accr, acci = carry
            base = pl.multiple_of((tt // SUBLANES - 1 - i) * SUBLANES, SUBLANES)
            xr, xi = _scan8(lam_ref[pl.ds(base, SUBLANES), :half], lam_ref[pl.ds(base, SUBLANES), half:],
                            row, steps, pwr, pwi, cr, ci, True)
            lam_ref[pl.ds(base, SUBLANES), :half] = xr
            lam_ref[pl.ds(base, SUBLANES), half:] = xi
            nr = jnp.where(last, cr, pltpu.roll(xr, SUBLANES - 1, 0))
            ni = jnp.where(last, ci, pltpu.roll(xi, SUBLANES - 1, 0))
            sr, si = st_ref[pl.ds(base, SUBLANES), :half], st_ref[pl.ds(base, SUBLANES), half:]
            return xr[0:1, :], xi[0:1, :], accr + sr * nr + si * ni, acci + sr * ni - si * nr

        cr, ci, accr, acci = lax.fori_loop(
            0, tt // SUBLANES, tile,
            (carry_ref[0:1, :half], carry_ref[0:1, half:], acc_ref[:, :half], acc_ref[:, half:]))
        carry_ref[0:1, :half] = cr
        carry_ref[0:1, half:] = ci
        acc_ref[:, :half] = accr
        acc_ref[:, half:] = acci
        lamb = lam_ref[...].astype(BF16)
        du = lax.dot_general(lamb, b_ref[...], _NT, preferred_element_type=F32) + d_ref[...] * dy
        du_ref[...] = du.astype(BF16)
        db_ref[...] += lax.dot_general(u.astype(BF16), lamb, _TN, preferred_element_type=F32)
        dc_ref[...] += lax.dot_general(st_ref[...].astype(BF16), dyb, _TN, preferred_element_type=F32)
        dd_ref[...] += _colsum(dy * u)

        @pl.when(step == nt - 1)
        def _():
            da_ref[...] = _colsum(acc_ref[...])

    d_ssm = ngb * cw
    return _pcall(
        body, name="ssm_bwd", grid=(ngb, nt),
        in_specs=[pl.BlockSpec((tt, cw), lambda g, i: (nt - 1 - i, g)),
                  pl.BlockSpec((tt, cw), lambda g, i: (nt - 1 - i, ucb0 + g)),
                  pl.BlockSpec((tt, two_l), lambda g, i: (nt - 1 - i, g)),
                  pl.BlockSpec((None, cw, two_l), lambda g, i: (g, 0, 0)),
                  pl.BlockSpec((None, two_l, cw), lambda g, i: (g, 0, 0)),
                  pl.BlockSpec((None, 1, two_l), lambda g, i: (g, 0, 0)),
                  pl.BlockSpec((1, cw), lambda g, i: (0, g))],
        out_specs=[pl.BlockSpec((tt, cw), lambda g, i: (nt - 1 - i, g)),
                   pl.BlockSpec((None, cw, two_l), lambda g, i: (g, 0, 0)),
                   pl.BlockSpec((None, two_l, cw), lambda g, i: (g, 0, 0)),
                   pl.BlockSpec((None, 1, two_l), lambda g, i: (g, 0, 0)),
                   pl.BlockSpec((1, cw), lambda g, i: (0, g))],
        out_shape=[jax.ShapeDtypeStruct((t, d_ssm), BF16),
                   jax.ShapeDtypeStruct((ngb, cw, two_l), F32),
                   jax.ShapeDtypeStruct((ngb, two_l, cw), F32),
                   jax.ShapeDtypeStruct((ngb, 1, two_l), F32),
                   jax.ShapeDtypeStruct((1, d_ssm), F32)],
        scratch_shapes=[pltpu.VMEM((tt, two_l), F32), pltpu.VMEM((SUBLANES, two_l), F32),
                        pltpu.VMEM((SUBLANES, two_l), F32)],
        compiler_params=_params(("parallel", "arbitrary")),
    )(dy0, proj, states, bexp, cexp, a_cat, d_skip)


def _zoh(lr, li, ls):
    step = jnp.exp(ls)
    e = jnp.exp(lr * step)
    ar, ai = e * jnp.cos(li * step), e * jnp.sin(li * step)
    den = lr * lr + li * li
    cr = ((ar - 1.0) * lr + ai * li) / den
    ci = (ai * lr - (ar - 1.0) * li) / den
    return step, ar, ai, den, cr, ci


def _ssm_param_fwd(lr, li, ls, br, bi):
    def body(lr_ref, li_ref, ls_ref, br_ref, bi_ref, ar_ref, ai_ref, bbr_ref, bbi_ref):
        _, ar, ai, _, cr, ci = _zoh(lr_ref[...], li_ref[...], ls_ref[...])
        ar_ref[...] = ar
        ai_ref[...] = ai
        bbr, bbi = _cmul(cr, ci, br_ref[...], bi_ref[...])
        bbr_ref[...] = bbr
        bbi_ref[...] = bbi

    small, big = jax.ShapeDtypeStruct(lr.shape, F32), jax.ShapeDtypeStruct(br.shape, F32)
    return _pcall(body, name="ssm_param_fwd", out_shape=[small, small, big, big],
                  compiler_params=_params())(lr, li, ls, br, bi)


def _ssm_param_bwd(lr, li, ls, br, bi, gar, gai, gbr, gbi):
    def body(lr_ref, li_ref, ls_ref, br_ref, bi_ref, gar_ref, gai_ref, gbr_ref, gbi_ref,
             dlr_ref, dli_ref, dls_ref, dbr_ref, dbi_ref):
        lrv, liv = lr_ref[...], li_ref[...]
        step, ar, ai, den, cr, ci = _zoh(lrv, liv, ls_ref[...])
        brv, biv, gr, gi = br_ref[...], bi_ref[...], gbr_ref[...], gbi_ref[...]
        dbr_ref[...] = cr * gr + ci * gi
        dbi_ref[...] = cr * gi - ci * gr
        gcr = jnp.sum(brv * gr + biv * gi, axis=1, keepdims=True)
        gci = jnp.sum(brv * gi - biv * gr, axis=1, keepdims=True)
        gtr = gar_ref[...] + (lrv * gcr - liv * gci) / den
        gti = gai_ref[...] + (lrv * gci + liv * gcr) / den
        qr = (cr * lrv + ci * liv) / den
        qi = (ci * lrv - cr * liv) / den
        gzr = ar * gtr + ai * gti
        gzi = ar * gti - ai * gtr
        dlr_ref[...] = step * gzr - (qr * gcr + qi * gci)
        dli_ref[...] = step * gzi - (qr * gci - qi * gcr)
        gstep = jnp.sum(lrv * gzr + liv * gzi, axis=2, keepdims=True)
        dls_ref[...] = jnp.broadcast_to(step * gstep, step.shape)

    small, big = jax.ShapeDtypeStruct(lr.shape, F32), jax.ShapeDtypeStruct(br.shape, F32)
    return _pcall(body, name="ssm_param_bwd", out_shape=[small, small, small, big, big],
                  compiler_params=_params())(lr, li, ls, br, bi, gar, gai, gbr, gbi)


def _block_diag_in(bb):
    g, h, p = bb.shape
    nb = g // GROUPS_PER_BLOCK
    eye = jnp.eye(GROUPS_PER_BLOCK, dtype=bb.dtype)[None, :, None, :, None]
    e = bb.reshape(nb, GROUPS_PER_BLOCK, h, 1, p) * eye
    return e.reshape(nb, GROUPS_PER_BLOCK * h, GROUPS_PER_BLOCK * p)


def _block_diag_take(e, h, p):
    nb = e.shape[0]
    eye = jnp.eye(GROUPS_PER_BLOCK, dtype=e.dtype)[None, :, None, :, None]
    d = jnp.sum(e.reshape(nb, GROUPS_PER_BLOCK, h, GROUPS_PER_BLOCK, p) * eye, axis=3)
    return d.reshape(nb * GROUPS_PER_BLOCK, h, p)


def _ada_fwd(c_all, w_sh, b_sh, tn=512):
    bsz, d = c_all.shape
    nsh = w_sh.shape[1]
    tn = _tile(nsh, tn, LANES)

    def body(c_ref, w_ref, b_ref, mod_ref, act_ref):
        act = c_ref[...] * jax.nn.sigmoid(c_ref[...])
        act_ref[...] = act
        mod_ref[...] = jnp.dot(act.astype(BF16), w_ref[...].astype(BF16), preferred_element_type=F32) + b_ref[...]

    return _pcall(
        body, name="ada_fwd", grid=(nsh // tn,),
        in_specs=[pl.BlockSpec((bsz, d), lambda j: (0, 0)), pl.BlockSpec((d, tn), lambda j: (0, j)),
                  pl.BlockSpec((1, tn), lambda j: (0, j))],
        out_specs=[pl.BlockSpec((bsz, tn), lambda j: (0, j)), pl.BlockSpec((bsz, d), lambda j: (0, 0))],
        out_shape=[jax.ShapeDtypeStruct((bsz, nsh), F32), jax.ShapeDtypeStruct((bsz, d), F32)],
        compiler_params=_params(("arbitrary",)),
    )(c_all, w_sh, b_sh)


def _adamw(w, g, m, v, name):
    r, c = w.shape
    tr = _tile(r, max(SUBLANES, (256 * 1024) // c // SUBLANES * SUBLANES), SUBLANES)
    c1, c2 = 1.0 / (1.0 - ADAM_B1 ** ADAM_STEP), 1.0 / (1.0 - ADAM_B2 ** ADAM_STEP)

    def body(w_ref, g_ref, m_ref, v_ref, d_ref, nm_ref, nv_ref):
        gv = g_ref[...]
        nm = ADAM_B1 * m_ref[...] + (1.0 - ADAM_B1) * gv
        nv = ADAM_B2 * v_ref[...] + (1.0 - ADAM_B2) * (gv * gv)
        nm_ref[...] = nm
        nv_ref[...] = nv
        d_ref[...] = -ADAM_LR * ((nm * c1) / (jnp.sqrt(nv * c2) + ADAM_EPS) + ADAM_WD * w_ref[...])

    spec = pl.BlockSpec((tr, c), lambda i: (i, 0))
    shp = jax.ShapeDtypeStruct((r, c), F32)
    return _pcall(body, name=name, grid=(r // tr,), in_specs=[spec] * 4, out_specs=[spec] * 3,
                  out_shape=[shp] * 3, compiler_params=_params(("parallel",)))(w, g, m, v)


def _sum_leading(arr, out_dtype, name):
    n, r, c = arr.shape
    tr = _tile(r, max(SUBLANES * 2, (512 * 1024) // (c * n) // (SUBLANES * 2) * (SUBLANES * 2)), SUBLANES * 2)

    def body(x_ref, o_ref):
        acc = x_ref[0].astype(F32)
        for k in range(1, n):
            acc = acc + x_ref[k].astype(F32)
        o_ref[...] = acc.astype(out_dtype)

    return _pcall(body, name=name, grid=(r // tr,),
                  in_specs=[pl.BlockSpec((n, tr, c), lambda i: (0, i, 0))],
                  out_specs=pl.BlockSpec((tr, c), lambda i: (i, 0)),
                  out_shape=jax.ShapeDtypeStruct((r, c), out_dtype),
                  compiler_params=_params(("parallel",)))(arr)


def _place():
    x, y, c = lax.axis_index("x"), lax.axis_index("y"), lax.axis_index("c")
    chips = [(1 - x, y), (x, 1 - y), (1 - x, 1 - y)]
    return x, y, c, chips


def _allgather8(v, name):
    m, n = v.shape

    def body(x_ref, out_ref, send_sems, recv_sems, local_sem):
        x, y, c, chips = _place()
        me, sibling = (x, y, c), (x, y, 1 - c)

        def slot(px, py, pc):
            return out_ref.at[4 * px + 2 * py + pc]

        def copy(k, block, to, src=None):
            return pltpu.make_async_remote_copy(
                src_ref=slot(*block) if src is None else src, dst_ref=slot(*block),
                send_sem=send_sems.at[k], recv_sem=recv_sems.at[k], device_id=to, device_id_type=MESH)

        mine = pltpu.make_async_copy(x_ref, slot(*me), local_sem)
        mine.start()
        first = [copy(0, me, sibling, src=x_ref)]
        first += [copy(1 + j, me, (*chip, c), src=x_ref) for j, chip in enumerate(chips)]
        for cp in first:
            cp.start()
        passed = [copy(4 + j, (*chip, c), sibling) for j, chip in enumerate(chips)]
        for j, chip in enumerate(chips):
            copy(1 + j, (*chip, c), me).wait_recv()
            passed[j].start()
        copy(0, sibling, me).wait_recv()
        for j, chip in enumerate(chips):
            copy(4 + j, (*chip, 1 - c), me).wait_recv()
        for cp in first + passed:
            cp.wait_send()
        mine.wait()

    return _pcall(
        body, name=name, out_shape=jax.ShapeDtypeStruct((N_DEV, m, n), F32),
        in_specs=[pl.BlockSpec(memory_space=pltpu.VMEM)], out_specs=pl.BlockSpec(memory_space=pltpu.VMEM),
        scratch_shapes=[pltpu.SemaphoreType.DMA((7,)), pltpu.SemaphoreType.DMA((7,)), pltpu.SemaphoreType.DMA],
        compiler_params=_params(),
    )(v)


def _scalars(*vals):
    return jnp.stack([jnp.asarray(v, jnp.int32) for v in vals])


def _cast_place(w, chip, name):
    r, cdim = w.shape
    tr = _tile(r, max(SUBLANES * 2, (512 * 1024) // cdim // (SUBLANES * 2) * (SUBLANES * 2)), SUBLANES * 2)

    def body(s_ref, w_ref, o_ref):
        o_ref[...] = w_ref[...].astype(BF16)

    grid_spec = pltpu.PrefetchScalarGridSpec(
        num_scalar_prefetch=1, grid=(r // tr,),
        in_specs=[pl.BlockSpec((tr, cdim), lambda i, s: (i, 0))],
        out_specs=pl.BlockSpec((None, tr, cdim), lambda i, s: (s[0], i, 0)))
    return _pcall(body, name=name, grid_spec=grid_spec, out_shape=jax.ShapeDtypeStruct((N_CHIPS, r, cdim), BF16),
                  compiler_params=_params(("parallel",)))(_scalars(chip), w)


def _gather_weights(bufs):
    nw = len(bufs)

    def body(*refs):
        outs = refs[nw:2 * nw]
        send_sems, recv_sems = refs[2 * nw:]
        x, y, c, chips = _place()
        me, sibling = (x, y, c), (x, y, 1 - c)

        def copy(w, k, chip, hc, to):
            h = outs[w].shape[1] // 2
            ref = outs[w].at[2 * chip[0] + chip[1], pl.ds(pl.multiple_of(hc * h, SUBLANES * 2), h)]
            return pltpu.make_async_remote_copy(
                src_ref=ref, dst_ref=ref, send_sem=send_sems.at[w, k], recv_sem=recv_sems.at[w, k],
                device_id=to, device_id_type=MESH)

        sent = []
        for w in range(nw):
            for k, chip in enumerate(chips):
                sent.append(copy(w, k, (x, y), c, (*chip, c)))
                sent[-1].start()
        for w in range(nw):
            for k, chip in enumerate(chips):
                copy(w, k, chip, c, me).wait_recv()
                sent.append(copy(w, 3 + k, chip, c, sibling))
                sent[-1].start()
        for w in range(nw):
            for k, chip in enumerate(chips):
                copy(w, 3 + k, chip, 1 - c, me).wait_recv()
        for cp in sent:
            cp.wait_send()

    any_spec = pl.BlockSpec(memory_space=pl.ANY)
    return _pcall(
        body, name="gather_weights",
        out_shape=[jax.ShapeDtypeStruct(b.shape, b.dtype) for b in bufs],
        in_specs=[any_spec] * nw, out_specs=[any_spec] * nw,
        input_output_aliases={w: w for w in range(nw)},
        scratch_shapes=[pltpu.SemaphoreType.DMA((nw, 6)), pltpu.SemaphoreType.DMA((nw, 6))],
        compiler_params=_params(),
    )(*bufs)


class _Comm:
    def __init__(self, ins, outs, aliases, n_sems, start, finish):
        self.ins, self.outs, self.aliases, self.n_sems = ins, outs, aliases, n_sems
        self.start, self.finish = start, finish


def _comm_gather_ici(buf):
    def copies(out, send_sems, recv_sems, incoming):
        x, y, c, chips = _place()
        h = out.shape[1] // 2
        res = []
        for k, chip in enumerate(chips):
            blk = chip if incoming else (x, y)
            ref = out.at[2 * blk[0] + blk[1], pl.ds(pl.multiple_of(c * h, SUBLANES * 2), h)]
            res.append(pltpu.make_async_remote_copy(
                src_ref=ref, dst_ref=ref, send_sem=send_sems.at[k], recv_sem=recv_sems.at[k],
                device_id=(*chip, c), device_id_type=MESH))
        return res

    def start(ci, co, send_sems, recv_sems):
        for cp in copies(co[0], send_sems, recv_sems, False):
            cp.start()

    def finish(ci, co, send_sems, recv_sems):
        for cp in copies(co[0], send_sems, recv_sems, True):
            cp.wait_recv()
        for cp in copies(co[0], send_sems, recv_sems, False):
            cp.wait_send()

    return _Comm([buf], [jax.ShapeDtypeStruct(buf.shape, buf.dtype)], {0: 0}, N_CHIPS - 1, start, finish)


def _forward_halves(bufs):
    nw = len(bufs)

    def body(*refs):
        outs = refs[nw:2 * nw]
        send_sems, recv_sems = refs[2 * nw:]
        x, y, c, chips = _place()

        def copy(w, k, hc):
            chip = chips[k]
            h = outs[w].shape[1] // 2
            ref = outs[w].at[2 * chip[0] + chip[1], pl.ds(pl.multiple_of(hc * h, SUBLANES * 2), h)]
            return pltpu.make_async_remote_copy(
                src_ref=ref, dst_ref=ref, send_sem=send_sems.at[w, k], recv_sem=recv_sems.at[w, k],
                device_id=(x, y, 1 - c), device_id_type=MESH)

        pairs = [(w, k) for w in range(nw) for k in range(len(chips))]
        for w, k in pairs:
            copy(w, k, c).start()
        for w, k in pairs:
            copy(w, k, 1 - c).wait_recv()
        for w, k in pairs:
            copy(w, k, c).wait_send()

    any_spec = pl.BlockSpec(memory_space=pl.ANY)
    return _pcall(
        body, name="forward_halves",
        out_shape=[jax.ShapeDtypeStruct(b.shape, b.dtype) for b in bufs],
        in_specs=[any_spec] * nw, out_specs=[any_spec] * nw,
        input_output_aliases={w: w for w in range(nw)},
        scratch_shapes=[pltpu.SemaphoreType.DMA((nw, N_CHIPS - 1)), pltpu.SemaphoreType.DMA((nw, N_CHIPS - 1))],
        compiler_params=_params(),
    )(*bufs)


def _comm_chip_exchange(psum):
    def copies(src, dst, send_sems, recv_sems):
        x, y, c, chips = _place()
        return [pltpu.make_async_remote_copy(
            src_ref=src.at[2 * chip[0] + chip[1]], dst_ref=dst.at[k], send_sem=send_sems.at[k],
            recv_sem=recv_sems.at[k], device_id=(*chip, c), device_id_type=MESH) for k, chip in enumerate(chips)]

    def start(ci, co, send_sems, recv_sems):
        for cp in copies(ci[0], co[0], send_sems, recv_sems):
            cp.start()

    def finish(ci, co, send_sems, recv_sems):
        cps = copies(ci[0], co[0], send_sems, recv_sems)
        for cp in cps:
            cp.wait_recv()
        for cp in cps:
            cp.wait_send()

    out = jax.ShapeDtypeStruct((N_CHIPS - 1,) + psum.shape[1:], psum.dtype)
    return _Comm([psum], [out], {}, N_CHIPS - 1, start, finish)


def _pair_exchange(grads, name):
    nw = len(grads)

    def body(*refs):
        ins, outs = refs[:nw], refs[nw:2 * nw]
        send_sems, recv_sems = refs[2 * nw:]
        x, y, c, _ = _place()
        sibling = (x, y, 1 - c)

        def copy(w, k):
            return pltpu.make_async_remote_copy(
                src_ref=ins[w].at[k, 1 - c], dst_ref=outs[w].at[k],
                send_sem=send_sems.at[w, k], recv_sem=recv_sems.at[w, k], device_id=sibling, device_id_type=MESH)

        copies = [copy(w, k) for w in range(nw) for k in range(N_CHIPS)]
        for cp in copies:
            cp.start()
        for cp in copies:
            cp.wait_recv()
        for cp in copies:
            cp.wait_send()

    any_spec = pl.BlockSpec(memory_space=pl.ANY)
    return _pcall(
        body, name=name,
        out_shape=[jax.ShapeDtypeStruct((N_CHIPS,) + g.shape[2:], g.dtype) for g in grads],
        in_specs=[any_spec] * nw, out_specs=[any_spec] * nw,
        scratch_shapes=[pltpu.SemaphoreType.DMA((nw, N_CHIPS)), pltpu.SemaphoreType.DMA((nw, N_CHIPS))],
        compiler_params=_params(),
    )(*grads)


def _pair_sum(view, recv, core, name):
    n, _, h, cdim = view.shape
    th = _tile(h, max(SUBLANES * 2, (512 * 1024) // cdim // (SUBLANES * 2) * (SUBLANES * 2)), SUBLANES * 2)

    def body(s_ref, a_ref, b_ref, o_ref):
        o_ref[...] = (a_ref[...].astype(F32) + b_ref[...].astype(F32)).astype(BF16)

    grid_spec = pltpu.PrefetchScalarGridSpec(
        num_scalar_prefetch=1, grid=(n, h // th),
        in_specs=[pl.BlockSpec((None, None, th, cdim), lambda k, i, s: (k, s[0], i, 0)),
                  pl.BlockSpec((None, th, cdim), lambda k, i, s: (k, i, 0))],
        out_specs=pl.BlockSpec((None, th, cdim), lambda k, i, s: (k, i, 0)))
    return _pcall(body, name=name, grid_spec=grid_spec, out_shape=jax.ShapeDtypeStruct((n, h, cdim), BF16),
                  compiler_params=_params(("parallel", "parallel")))(_scalars(core), view, recv)


def _chip_exchange(psums):
    nw = len(psums)

    def body(*refs):
        ins, outs = refs[:nw], refs[nw:2 * nw]
        send_sems, recv_sems = refs[2 * nw:]
        x, y, c, chips = _place()

        def copy(w, k):
            chip = chips[k]
            return pltpu.make_async_remote_copy(
                src_ref=ins[w].at[2 * chip[0] + chip[1]], dst_ref=outs[w].at[k],
                send_sem=send_sems.at[w, k], recv_sem=recv_sems.at[w, k], device_id=(*chip, c), device_id_type=MESH)

        copies = [copy(w, k) for w in range(nw) for k in range(len(chips))]
        for cp in copies:
            cp.start()
        for cp in copies:
            cp.wait_recv()
        for cp in copies:
            cp.wait_send()

    any_spec = pl.BlockSpec(memory_space=pl.ANY)
    return _pcall(
        body, name="grad_chip_exchange",
        out_shape=[jax.ShapeDtypeStruct((N_CHIPS - 1,) + p.shape[1:], p.dtype) for p in psums],
        in_specs=[any_spec] * nw, out_specs=[any_spec] * nw,
        scratch_shapes=[pltpu.SemaphoreType.DMA((nw, N_CHIPS - 1)), pltpu.SemaphoreType.DMA((nw, N_CHIPS - 1))],
        compiler_params=_params(),
    )(*psums)


def _chip_sum(psums, recv, chip, core, name):
    _, h, cdim = psums.shape
    th = _tile(h, max(SUBLANES * 2, (256 * 1024) // cdim // (SUBLANES * 2) * (SUBLANES * 2)), SUBLANES * 2)

    def body(chip_ref, core_ref, a_ref, b_ref, o_ref):
        acc = a_ref[...].astype(F32)
        for k in range(N_CHIPS - 1):
            acc = acc + b_ref[k].astype(F32)
        o_ref[...] = acc

    grid_spec = pltpu.PrefetchScalarGridSpec(
        num_scalar_prefetch=2, grid=(h // th,),
        in_specs=[pl.BlockSpec((None, th, cdim), lambda i, s, t: (s[0], i, 0)),
                  pl.BlockSpec((N_CHIPS - 1, th, cdim), lambda i, s, t: (0, i, 0))],
        out_specs=pl.BlockSpec((None, th, cdim), lambda i, s, t: (t[0], i, 0)))
    return _pcall(body, name=name, grid_spec=grid_spec, out_shape=jax.ShapeDtypeStruct((2, h, cdim), F32),
                  compiler_params=_params(("parallel",)))(_scalars(chip), _scalars(core), psums, recv)


def _share_halves(bufs):
    nw = len(bufs)

    def body(*refs):
        outs = refs[nw:2 * nw]
        send_sems, recv_sems = refs[2 * nw:]
        x, y, c, _ = _place()
        copies = [pltpu.make_async_remote_copy(
            src_ref=outs[w].at[c], dst_ref=outs[w].at[c], send_sem=send_sems.at[w], recv_sem=recv_sems.at[w],
            device_id=(x, y, 1 - c), device_id_type=MESH) for w in range(nw)]
        for cp in copies:
            cp.start()
        for w in range(nw):
            pltpu.make_async_remote_copy(
                src_ref=outs[w].at[1 - c], dst_ref=outs[w].at[1 - c], send_sem=send_sems.at[w],
                recv_sem=recv_sems.at[w], device_id=(x, y, 1 - c), device_id_type=MESH).wait_recv()
        for cp in copies:
            cp.wait_send()

    any_spec = pl.BlockSpec(memory_space=pl.ANY)
    return _pcall(
        body, name="grad_share_halves",
        out_shape=[jax.ShapeDtypeStruct(b.shape, b.dtype) for b in bufs],
        in_specs=[any_spec] * nw, out_specs=[any_spec] * nw,
        input_output_aliases={w: w for w in range(nw)},
        scratch_shapes=[pltpu.SemaphoreType.DMA((nw,)), pltpu.SemaphoreType.DMA((nw,))],
        compiler_params=_params(),
    )(*bufs)


def _pack(arrays):
    flat = jnp.concatenate([a.reshape(-1).astype(F32) for a in arrays])
    unit = 2 * SUBLANES * PACK_COLS
    pad = (-flat.shape[0]) % unit
    return jnp.pad(flat, (0, pad)).reshape(-1, PACK_COLS)


def _unpack(buf, shapes):
    flat, out, off = buf.reshape(-1), [], 0
    for s in shapes:
        n = math.prod(s)
        out.append(flat[off:off + n].reshape(s))
        off += n
    return out


def kernel(x, c, w_ada, b_ada, norm1_g, w_in, sinks, ssm_lam_re, ssm_lam_im, ssm_log_step, ssm_b_re, ssm_b_im, ssm_c_re, ssm_c_im, ssm_d, w_glu, b_glu, attn_out_g, ssm_out_g, w_out, norm2_g, w_ff1, w_ff2, final_g, loss_target, m_w_ada, m_b_ada, m_norm1_g, m_w_in, m_sinks, m_ssm_lam_re, m_ssm_lam_im, m_ssm_log_step, m_ssm_b_re, m_ssm_b_im, m_ssm_c_re, m_ssm_c_im, m_ssm_d, m_w_glu, m_b_glu, m_attn_out_g, m_ssm_out_g, m_w_out, m_norm2_g, m_w_ff1, m_w_ff2, m_final_g, v_w_ada, v_b_ada, v_norm1_g, v_w_in, v_sinks, v_ssm_lam_re, v_ssm_lam_im, v_ssm_log_step, v_ssm_b_re, v_ssm_b_im, v_ssm_c_re, v_ssm_c_im, v_ssm_d, v_w_glu, v_b_glu, v_attn_out_g, v_ssm_out_g, v_w_out, v_norm2_g, v_w_ff1, v_w_ff2, v_final_g):
    t, d = x.shape[1], x.shape[2]
    d_attn, d_ssm = attn_out_g.shape[1], ssm_d.shape[1]
    d_in = w_in.shape[2] * N_CHIPS
    d_kv = (d_in - d_attn - d_ssm) // 2
    n_q, n_kv = d_attn // HEAD_DIM, d_kv // HEAD_DIM
    n_grp = ssm_lam_re.shape[1]
    assert n_q == n_kv * Q_PER_KV and t % WINDOW == 0 and d_ssm == n_grp * SSM_GROUP
    assert d_kv % LANES == 0 and d_attn % d_kv == 0 and n_grp % GROUPS_PER_BLOCK == 0
    cw = GROUPS_PER_BLOCK * SSM_GROUP
    ucb0 = (d_attn + 2 * d_kv) // cw
    assert (d_attn + 2 * d_kv) % cw == 0
    xi, yi, ci = lax.axis_index("x"), lax.axis_index("y"), lax.axis_index("c")
    chip = 2 * xi + yi
    dev = 2 * chip + ci
    xs, tgt = x[0], loss_target[0]
    vec = lambda a: a.reshape(1, -1)

    n_ada = w_ada.shape[2]
    c_all = _allgather8(c.reshape(SUBLANES, d // SUBLANES), "gather_c").reshape(N_DEV, d)
    b_sh = lax.dynamic_slice_in_dim(b_ada, chip * n_ada, n_ada, axis=1)
    mod_sh, c_act = _ada_fwd(c_all, w_ada[0], b_sh)
    mod_all = _allgather8(mod_sh, "gather_mod")
    mod_me = lax.dynamic_index_in_dim(mod_all[0::2], dev, axis=1, keepdims=False)
    mod_me = mod_me.reshape(N_CHIPS * n_ada // d, 1, d)
    shift1, scale1, gate1, shift2, scale2, gate2 = [mod_me[i] for i in range(N_MOD)]

    win_s, wglu_s, wout_s, wff1_s = _gather_weights(
        [_cast_place(w[0], chip, "cast_" + n)
         for n, w in (("w_in", w_in), ("w_glu", w_glu), ("w_out", w_out), ("w_ff1", w_ff1))])
    wff2_own = _cast_place(w_ff2[0], chip, "cast_w_ff2")
    wglu = wglu_s.reshape(d_ssm, d_ssm)
    wout = wout_s.reshape(d_attn + d_ssm, d)
    half_view = lambda g, w: g.reshape(N_CHIPS, 2, w.shape[0] // 2, w.shape[1])

    g3 = lambda a: a.reshape(n_grp, 1, STATE)
    lr3, li3 = g3(ssm_lam_re[0]), g3(ssm_lam_im[0])
    ls3 = jnp.broadcast_to(ssm_log_step[0].reshape(n_grp, 1, 1), (n_grp, 1, STATE))
    b_re3, b_im3 = ssm_b_re[0].transpose(0, 2, 1), ssm_b_im[0].transpose(0, 2, 1)
    a_re, a_im, bb_re, bb_im = _ssm_param_fwd(lr3, li3, ls3, b_re3, b_im3)
    ngb = n_grp // GROUPS_PER_BLOCK
    a_cat = jnp.concatenate([a_re.reshape(ngb, 1, -1), a_im.reshape(ngb, 1, -1)], axis=-1)
    bexp = jnp.concatenate([_block_diag_in(bb_re), _block_diag_in(bb_im)], axis=-1).astype(BF16)
    cexp = jnp.concatenate([_block_diag_in(ssm_c_re[0]), -_block_diag_in(ssm_c_im[0])], axis=-1)
    cexp = cexp.transpose(0, 2, 1).astype(BF16)

    half = HEAD_DIM // 2
    inv_freq = ROPE_THETA ** (-jnp.arange(half, dtype=F32) / half)
    ang = jnp.arange(t, dtype=F32)[:, None] * inv_freq[None, :]
    cos = jnp.tile(jnp.cos(ang), (1, LANES // half))
    sin = jnp.tile(jnp.sin(ang), (1, LANES // half))

    h = _norm_mod(xs, norm1_g, scale1, shift1)
    proj = _matmul(h, win_s, "nn", "mm_in", [F32], b_stacked=True, tn=win_s.shape[2])
    qr, kr, vb = _rope_fwd(proj, cos, sin, d_attn, d_kv)
    heads = lambda a, n: a.reshape(t, n, HEAD_DIM).transpose(1, 0, 2)
    unheads = lambda a: a.transpose(1, 0, 2).reshape(t, -1)
    qh, kh, vh = heads(qr, n_q), heads(kr, n_kv), heads(vb, n_kv)
    oh = _attn_fwd(qh, kh, vh, sinks[0])
    attn = unheads(oh)
    y0, z, states = _ssm_fwd(proj, ucb0, bexp, cexp, a_cat, ssm_d)
    gl = _matmul(z, wglu, "nn", "mm_glu", [F32])
    mixed = _mix(attn, y0, gl, b_glu, attn_out_g, ssm_out_g)
    mo = _matmul(mixed, wout, "nn", "mm_out", [F32])
    x1, h2 = _res_norm_mod(xs, mo, gate1, norm2_g, scale2, shift2)

    def relu2(acc):
        r = jnp.maximum(acc, 0.0)
        return acc, r * r

    (a_act, rr), (wff2_ici,) = _matmul(h2, wff1_s, "nn", "mm_ff1", [BF16, BF16], epilogue=relu2, b_stacked=True,
                                       comm=_comm_gather_ici(wff2_own))
    wff2 = _forward_halves([wff2_ici])[0].reshape(-1, d)
    ff = _matmul(rr, wff2, "nn", "mm_ff2", [F32])
    dx2, dff, loss_cols, dgf, dgate2 = _final(x1, ff, tgt, gate2, vec(final_g))
    loss = lax.psum(0.5 * jnp.sum(loss_cols) / d, ("x", "y", "c"))

    d_relu2 = lambda acc, av: (acc * 2.0 * jnp.maximum(av.astype(F32), 0.0),)
    da = _matmul(dff, wff2, "nt", "mm_dff2", [BF16], epilogue=d_relu2, extras=(a_act,))
    v_ff2 = half_view(_matmul(rr, dff, "tn", "mm_gw_ff2", [BF16]), w_ff2[0])
    p_ff2 = _pair_sum(v_ff2, _pair_exchange([v_ff2], "pair_exchange_ff2")[0], ci, "pair_sum_ff2")
    gw_ff1, (r_ff2,) = _matmul(h2, da, "tn", "mm_gw_ff1", [BF16], out_stacked=N_CHIPS,
                               comm=_comm_chip_exchange(p_ff2))
    v_ff1 = half_view(gw_ff1, w_ff1[0])
    p_ff1 = _pair_sum(v_ff1, _pair_exchange([v_ff1], "pair_exchange_ff1")[0], ci, "pair_sum_ff1")
    dh2, (r_ff1,) = _matmul(da, wff1_s, "nt", "mm_dff1", [F32], b_stacked=True, comm=_comm_chip_exchange(p_ff1))
    dx1, dmo, dshift2, dscale2, dg2, dgate1 = _bwd_norm2(x1, dh2, dx2, mo, norm2_g, scale2, gate1)
    dmixed = _matmul(dmo, wout, "nt", "mm_dout", [F32])
    gw_out = _matmul(mixed, dmo, "tn", "mm_gw_out", [BF16])
    dattn, dgl, dzp, dga, dgs, dbglu = _bwd_mix(dmixed, attn, y0, gl, b_glu, attn_out_g, ssm_out_g)
    d_gelu = lambda acc, dz, yv: ((acc + dz) * _gelu_grad(yv),)
    dy0 = _matmul(dgl, wglu, "nt", "mm_dglu", [F32], epilogue=d_gelu, extras=(dzp, y0))
    gw_glu = _matmul(z, dgl, "tn", "mm_gw_glu", [BF16])
    du, dbexp, dcexp, da_bar, dd = _ssm_bwd(dy0, proj, ucb0, states, bexp, cexp, a_cat, ssm_d)
    doh = heads(dattn, n_q)
    dqh, dkc, dkp, dvc, dvp, dsink = _attn_bwd(qh, kh, vh, oh, doh, sinks[0])
    up = lambda a: jnp.concatenate([unheads(a)[WINDOW:], jnp.zeros((WINDOW, d_kv), F32)], axis=0)
    dq, dk, dv = _rope_bwd(unheads(dqh), unheads(dkc), up(dkp), unheads(dvc), up(dvp), cos, sin)
    dproj = jnp.concatenate([dq, dk, dv, du], axis=1)
    dh = _matmul(dproj, win_s, "nt", "mm_din", [F32], b_stacked=True, tk=win_s.shape[2])
    gw_in = _matmul(h, dproj, "tn", "mm_gw_in", [BF16], out_stacked=N_CHIPS, tn=win_s.shape[2])
    grad_x, dshift1, dscale1, dg1 = _bwd_norm1(xs, dh, dx1, norm1_g, scale1)

    half_l = GROUPS_PER_BLOCK * STATE
    ga_re = da_bar[:, 0, :half_l].reshape(n_grp, 1, STATE)
    ga_im = da_bar[:, 0, half_l:].reshape(n_grp, 1, STATE)
    gbb_re = _block_diag_take(dbexp[:, :, :half_l], SSM_GROUP, STATE)
    gbb_im = _block_diag_take(dbexp[:, :, half_l:], SSM_GROUP, STATE)
    dcexp_t = dcexp.transpose(0, 2, 1)
    gc_re = _block_diag_take(dcexp_t[:, :, :half_l], SSM_GROUP, STATE)
    gc_im = -_block_diag_take(dcexp_t[:, :, half_l:], SSM_GROUP, STATE)
    dmod = jnp.concatenate([dshift1, dscale1, dgate1, dshift2, dscale2, dgate2], axis=1)
    dsinks = dsink.reshape(n_q, WINDOW).sum(axis=1)
    pieces = [dmod, dg1, dsinks, ga_re, ga_im, gbb_re, gbb_im, gc_re, gc_im, dd, dbglu, dga, dgs, dg2, dgf]
    gathered = _allgather8(_pack(pieces), "gather_small")
    summed = _sum_leading(gathered, F32, "sum_small")
    (g_b_ada, g_norm1, g_sinks, ga_re, ga_im, gbb_re, gbb_im, g_c_re, g_c_im, g_d, g_b_glu, g_attn_g, g_ssm_g,
     g_norm2, g_final) = _unpack(summed, [p.shape for p in pieces])
    g_lr, g_li, g_ls, g_b_re3, g_b_im3 = _ssm_param_bwd(lr3, li3, ls3, b_re3, b_im3, ga_re, ga_im, gbb_re, gbb_im)
    small_grads = [
        g_b_ada, g_norm1, g_sinks.reshape(1, -1), g_lr.reshape(1, n_grp, STATE), g_li.reshape(1, n_grp, STATE),
        g_ls[:, 0, 0].reshape(1, n_grp), g_b_re3.transpose(0, 2, 1)[None], g_b_im3.transpose(0, 2, 1)[None],
        g_c_re[None], g_c_im[None], g_d, g_b_glu, g_attn_g, g_ssm_g, g_norm2, g_final.reshape(-1)]
    small_w = [b_ada, norm1_g, sinks, ssm_lam_re, ssm_lam_im, ssm_log_step, ssm_b_re, ssm_b_im, ssm_c_re,
               ssm_c_im, ssm_d, b_glu, attn_out_g, ssm_out_g, norm2_g, final_g]
    small_m = [m_b_ada, m_norm1_g, m_sinks, m_ssm_lam_re, m_ssm_lam_im, m_ssm_log_step, m_ssm_b_re, m_ssm_b_im,
               m_ssm_c_re, m_ssm_c_im, m_ssm_d, m_b_glu, m_attn_out_g, m_ssm_out_g, m_norm2_g, m_final_g]
    small_v = [v_b_ada, v_norm1_g, v_sinks, v_ssm_lam_re, v_ssm_lam_im, v_ssm_log_step, v_ssm_b_re, v_ssm_b_im,
               v_ssm_c_re, v_ssm_c_im, v_ssm_d, v_b_glu, v_attn_out_g, v_ssm_out_g, v_norm2_g, v_final_g]
    small_grads = [g.reshape(w.shape) for g, w in zip(small_grads, small_w)]
    s_delta, s_m, s_v = _adamw(_pack(small_w), _pack(small_grads), _pack(small_m), _pack(small_v), "adamw_small")
    shapes = [w.shape for w in small_w]
    s_delta, s_m, s_v = _unpack(s_delta, shapes), _unpack(s_m, shapes), _unpack(s_v, shapes)

    dmod_rows = gathered.reshape(N_DEV, -1)[:, :dmod.shape[1]]
    dmod_sh = lax.dynamic_slice_in_dim(dmod_rows, chip * n_ada, n_ada, axis=1)
    g_w_ada = _matmul(c_act, dmod_sh, "tn", "mm_gw_ada", [F32], tk=N_DEV, precision=lax.Precision.HIGHEST)

    big_w = [w_in[0], w_glu[0], w_out[0], w_ff1[0], w_ff2[0]]
    views = [half_view(g, w) for g, w in zip([gw_in, gw_glu, gw_out], big_w)]
    pairs = _pair_exchange(views, "pair_exchange_rest")
    psums = [_pair_sum(v, p, ci, f"pair_sum_{i}") for i, (v, p) in enumerate(zip(views, pairs))]
    recvd = list(_chip_exchange(psums)) + [r_ff1, r_ff2]
    psums += [p_ff1, p_ff2]
    halves = [_chip_sum(p, r, chip, ci, f"chip_sum_{i}") for i, (p, r) in enumerate(zip(psums, recvd))]
    big_grads = [s.reshape(w.shape) for s, w in zip(_share_halves(halves), big_w)]

    big_names = ["w_ada", "w_in", "w_glu", "w_out", "w_ff1", "w_ff2"]
    big_w = [w_ada[0]] + big_w
    big_grads = [g_w_ada] + big_grads
    big_m = [m_w_ada[0], m_w_in[0], m_w_glu[0], m_w_out[0], m_w_ff1[0], m_w_ff2[0]]
    big_v = [v_w_ada[0], v_w_in[0], v_w_glu[0], v_w_out[0], v_w_ff1[0], v_w_ff2[0]]
    big_upd = {n: _adamw(w, g, m, v, "adamw_" + n) for n, w, g, m, v in zip(big_names, big_w, big_grads, big_m, big_v)}
    big_grad = dict(zip(big_names, big_grads))

    order = ["w_ada", "b_ada", "norm1_g", "w_in", "sinks", "ssm_lam_re", "ssm_lam_im", "ssm_log_step", "ssm_b_re",
             "ssm_b_im", "ssm_c_re", "ssm_c_im", "ssm_d", "w_glu", "b_glu", "attn_out_g", "ssm_out_g", "w_out",
             "norm2_g", "w_ff1", "w_ff2", "final_g"]
    small_names = [n for n in order if n not in big_names]
    grads, deltas, new_m, new_v = {}, {}, {}, {}
    for i, n in enumerate(small_names):
        grads[n], deltas[n], new_m[n], new_v[n] = small_grads[i], s_delta[i], s_m[i], s_v[i]
    for n in big_names:
        grads[n] = big_grad[n][None]
        deltas[n], new_m[n], new_v[n] = [a[None] for a in big_upd[n]]
    return (loss, grad_x[None], *[grads[n] for n in order], *[deltas[n] for n in order],
            *[new_m[n] for n in order], *[new_v[n] for n in order])
```

```python
import functools
import math

import jax
import jax.numpy as jnp
from jax import lax
from jax.experimental import pallas as pl
from jax.experimental.pallas import tpu as pltpu

F32 = jnp.float32
BF16 = jnp.bfloat16
MESH = pl.DeviceIdType.MESH

EPS = 1e-6
HEAD_DIM = 64
Q_PER_KV = 8
WINDOW = 128
ROPE_THETA = 10000.0
SSM_GROUP = 16
STATE = 64
GROUPS_PER_BLOCK = 16
N_MOD = 6
N_CHIPS = 4
N_DEV = 8
ADAM_LR = 0.001
ADAM_B1 = 0.9
ADAM_B2 = 0.999
ADAM_EPS = 1e-08
ADAM_WD = 0.01
ADAM_STEP = 10
LANES = 128
SUBLANES = 8
VMEM_LIMIT = 56 * 1024 * 1024
PACK_COLS = 512


def _pcall(body, **kw):
    return pl.pallas_call(body, **kw)


def _params(sem=None):
    return pltpu.CompilerParams(dimension_semantics=sem, vmem_limit_bytes=VMEM_LIMIT)


def _call(body, name, grid, in_specs, out_specs, out_shape, scratch, sem, operands, comm=None):
    if comm is None:
        res = _pcall(body, name=name, grid=grid, in_specs=in_specs, out_specs=out_specs, out_shape=out_shape,
                     scratch_shapes=scratch, compiler_params=_params(sem))(*operands)
        return tuple(res), ()
    n_in, n_out, n_ci, n_co = len(in_specs), len(out_specs), len(comm.ins), len(comm.outs)

    def carrying(*refs):
        ins, ci = refs[:n_in], refs[n_in:n_in + n_ci]
        outs = refs[n_in + n_ci:n_in + n_ci + n_out]
        co = refs[n_in + n_ci + n_out:n_in + n_ci + n_out + n_co]
        rest, send_sems, recv_sems = refs[n_in + n_ci + n_out + n_co:-2], refs[-2], refs[-1]
        ids = [pl.program_id(a) for a in range(len(grid))]
        first = functools.reduce(lambda p, q: p & q, [i == 0 for i in ids])
        last = functools.reduce(lambda p, q: p & q, [i == g - 1 for i, g in zip(ids, grid)])

        @pl.when(first)
        def _():
            comm.start(ci, co, send_sems, recv_sems)

        body(*ins, *outs, *rest)

        @pl.when(last)
        def _():
            comm.finish(ci, co, send_sems, recv_sems)

    any_spec = pl.BlockSpec(memory_space=pl.ANY)
    res = _pcall(
        carrying, name=name, grid=grid, in_specs=list(in_specs) + [any_spec] * n_ci,
        out_specs=list(out_specs) + [any_spec] * n_co, out_shape=list(out_shape) + list(comm.outs),
        input_output_aliases={n_in + ci: n_out + co for ci, co in comm.aliases.items()},
        scratch_shapes=list(scratch) + [pltpu.SemaphoreType.DMA((comm.n_sems,)), pltpu.SemaphoreType.DMA((comm.n_sems,))],
        compiler_params=_params(("arbitrary",) * len(grid)),
    )(*operands, *comm.ins)
    return tuple(res[:n_out]), tuple(res[n_out:])


def _tile(n, want, unit):
    if n <= want:
        return n
    t = (want // unit) * unit
    while t > unit and n % t:
        t -= unit
    assert n % t == 0, (n, want, unit)
    return t


_NN = (((1,), (0,)), ((), ()))
_NT = (((1,), (1,)), ((), ()))
_TN = (((0,), (0,)), ((), ()))


def _matmul(a, b, mode, name, out_dtypes, epilogue=None, extras=(), b_stacked=False, out_stacked=0,
            tm=1024, tn=1024, tk=4096, precision=None, comm=None):
    if mode == "nn":
        m, kdim = a.shape
        if b_stacked:
            s, _, nsh = b.shape
            n = s * nsh
            tn = _tile(nsh, tn, LANES)
        else:
            n = b.shape[1]
            tn = _tile(n, tn, LANES)
        tm, tk = _tile(m, tm, SUBLANES * 2), _tile(kdim, tk, LANES)
        a_spec = pl.BlockSpec((tm, tk), lambda i, j, k: (i, k))
        if b_stacked:
            npb = nsh // tn
            b_spec = pl.BlockSpec((None, tk, tn), lambda i, j, k: (j // npb, k, j % npb))
        else:
            b_spec = pl.BlockSpec((tk, tn), lambda i, j, k: (k, j))
        dims = _NN
    elif mode == "nt":
        m, kdim = a.shape
        if b_stacked:
            s, n, ksh = b.shape
            tk = _tile(ksh, tk, LANES)
            kpb = ksh // tk
            tn = _tile(n, tn, LANES)
            b_spec = pl.BlockSpec((None, tn, tk), lambda i, j, k: (k // kpb, j, k % kpb))
        else:
            n = b.shape[0]
            tk = _tile(kdim, tk, LANES)
            tn = _tile(n, tn, LANES)
            b_spec = pl.BlockSpec((tn, tk), lambda i, j, k: (j, k))
        tm = _tile(m, tm, SUBLANES * 2)
        a_spec = pl.BlockSpec((tm, tk), lambda i, j, k: (i, k))
        dims = _NT
    else:
        kdim, m = a.shape
        n = b.shape[1]
        tm = _tile(m, tm, LANES)
        tk = _tile(kdim, tk, SUBLANES * 2)
        if out_stacked:
            nsh = n // out_stacked
            tn = _tile(nsh, tn, LANES)
        else:
            tn = _tile(n, tn, LANES)
        a_spec = pl.BlockSpec((tk, tm), lambda i, j, k: (k, i))
        b_spec = pl.BlockSpec((tk, tn), lambda i, j, k: (k, j))
        dims = _TN
    nk = kdim // tk
    grid = (m // tm, n // tn, nk)
    if out_stacked:
        npo = (n // out_stacked) // tn
        o_spec = pl.BlockSpec((None, tm, tn), lambda i, j, k: (j // npo, i, j % npo))
        out_shape = [jax.ShapeDtypeStruct((out_stacked, m, n // out_stacked), dt) for dt in out_dtypes]
    else:
        o_spec = pl.BlockSpec((tm, tn), lambda i, j, k: (i, j))
        out_shape = [jax.ShapeDtypeStruct((m, n), dt) for dt in out_dtypes]
    x_spec = pl.BlockSpec((tm, tn), lambda i, j, k: (i, j))
    n_ex, n_out = len(extras), len(out_dtypes)

    def body(a_ref, b_ref, *rest):
        ex_refs, out_refs, acc_ref = rest[:n_ex], rest[n_ex:n_ex + n_out], rest[-1]
        k = pl.program_id(2)

        def finish(acc):
            outs = (acc,) if epilogue is None else epilogue(acc, *[r[...] for r in ex_refs])
            for r, o in zip(out_refs, outs):
                r[...] = o.astype(r.dtype)

        part = lax.dot_general(a_ref[...], b_ref[...], dims, precision=precision, preferred_element_type=F32)
        if nk == 1:
            finish(part)
        else:
            @pl.when(k == 0)
            def _():
                acc_ref[...] = part

            @pl.when(k > 0)
            def _():
                acc_ref[...] += part

            @pl.when(k == nk - 1)
            def _():
                finish(acc_ref[...])

    res, carried = _call(
        body, name, grid, [a_spec, b_spec] + [x_spec] * n_ex, [o_spec] * n_out, out_shape,
        [pltpu.VMEM((tm, tn) if nk > 1 else (SUBLANES, LANES), F32)], ("parallel", "parallel", "arbitrary"),
        (a, b, *extras), comm)
    main = res[0] if n_out == 1 else res
    return (main, carried) if comm else main


def _rowwise(body, name, rows, row_ins, vec_ins, row_outs, acc_outs, tr=128):
    tr = _tile(rows, tr, SUBLANES * 2)
    n_ri, n_vi, n_ro, n_ao = len(row_ins), len(vec_ins), len(row_outs), len(acc_outs)

    def kern(*refs):
        ri, vi = refs[:n_ri], refs[n_ri:n_ri + n_vi]
        ro = refs[n_ri + n_vi:n_ri + n_vi + n_ro]
        ao = refs[n_ri + n_vi + n_ro:]

        @pl.when(pl.program_id(0) == 0)
        def _():
            for r in ao:
                r[...] = jnp.zeros_like(r)

        body(ri, vi, ro, ao)

    in_specs = [pl.BlockSpec((tr, w), functools.partial(lambda i, cb: (i, cb), cb=cb)) for _, w, cb in row_ins]
    in_specs += [pl.BlockSpec(v.shape, lambda i: (0, 0)) for v in vec_ins]
    out_specs = [pl.BlockSpec((tr, w), lambda i: (i, 0)) for w, _ in row_outs]
    out_specs += [pl.BlockSpec((1, w), lambda i: (0, 0)) for w in acc_outs]
    out_shape = [jax.ShapeDtypeStruct((rows, w), dt) for w, dt in row_outs]
    out_shape += [jax.ShapeDtypeStruct((1, w), F32) for w in acc_outs]
    return _pcall(
        kern, name=name, grid=(rows // tr,), in_specs=in_specs, out_specs=out_specs, out_shape=out_shape,
        compiler_params=_params(("arbitrary",)),
    )(*[a for a, _, _ in row_ins], *vec_ins)


def _colsum(x):
    return jnp.sum(x, axis=0, keepdims=True)


def _rstd(x):
    return lax.rsqrt(jnp.mean(x * x, axis=-1, keepdims=True) + EPS)


def _norm_bwd(dxn, xn, r):
    return r * (dxn - xn * jnp.mean(dxn * xn, axis=-1, keepdims=True))


_SQRT_HALF = math.sqrt(0.5)
_INV_SQRT_2PI = 1.0 / math.sqrt(2.0 * math.pi)


def _gelu(y):
    return 0.5 * y * (1.0 + lax.erf(y * _SQRT_HALF))


def _gelu_grad(y):
    return 0.5 * (1.0 + lax.erf(y * _SQRT_HALF)) + y * jnp.exp(-0.5 * y * y) * _INV_SQRT_2PI


def _norm_mod(x, g, scale, shift):
    def body(ri, vi, ro, ao):
        xv = ri[0][...]
        h = xv * _rstd(xv) * vi[0][...] * (1.0 + vi[1][...]) + vi[2][...]
        ro[0][...] = h.astype(BF16)

    d = x.shape[1]
    return _rowwise(body, "norm_mod", x.shape[0], [(x, d, 0)], [g, scale, shift], [(d, BF16)], [])[0]


def _res_norm_mod(x, mo, gate, g, scale, shift):
    def body(ri, vi, ro, ao):
        x1 = ri[0][...] + vi[0][...] * ri[1][...]
        ro[0][...] = x1
        ro[1][...] = (x1 * _rstd(x1) * vi[1][...] * (1.0 + vi[2][...]) + vi[3][...]).astype(BF16)

    d = x.shape[1]
    return _rowwise(body, "res_norm_mod", x.shape[0], [(x, d, 0), (mo, d, 0)], [gate, g, scale, shift],
                    [(d, F32), (d, BF16)], [])


def _mix(attn, y0, gl, b_glu, ga, gs):
    da, ds = attn.shape[1], y0.shape[1]

    def body(ri, vi, ro, ao):
        at = ri[0][...]
        z = _gelu(ri[1][...])
        o = z * jax.nn.sigmoid(ri[2][...] + vi[0][...])
        ro[0][:, :da] = (at * _rstd(at) * vi[1][...]).astype(BF16)
        ro[0][:, da:] = (o * _rstd(o) * vi[2][...]).astype(BF16)

    return _rowwise(body, "mix", attn.shape[0], [(attn, da, 0), (y0, ds, 0), (gl, ds, 0)], [b_glu, ga, gs],
                    [(da + ds, BF16)], [])[0]


def _final(x1, ff, tgt, gate2, gf):
    d = x1.shape[1]

    def body(ri, vi, ro, ao):
        ffv = ri[1][...]
        x2 = ri[0][...] + vi[0][...] * ffv
        r = _rstd(x2)
        xn = x2 * r
        e = xn * vi[1][...] - ri[2][...]
        ao[0][...] += _colsum(e * e)
        dy = e * (1.0 / d)
        ao[1][...] += _colsum(dy * xn)
        dx2 = _norm_bwd(dy * vi[1][...], xn, r)
        ao[2][...] += _colsum(dx2 * ffv)
        ro[0][...] = dx2
        ro[1][...] = (dx2 * vi[0][...]).astype(BF16)

    return _rowwise(body, "final", x1.shape[0], [(x1, d, 0), (ff, d, 0), (tgt, d, 0)], [gate2, gf],
                    [(d, F32), (d, BF16)], [d, d, d])


def _bwd_norm2(x1, dh2, dx2, mo, g2, scale2, gate1):
    d = x1.shape[1]

    def body(ri, vi, ro, ao):
        xv, dh = ri[0][...], ri[1][...]
        r = _rstd(xv)
        xn = xv * r
        ao[0][...] += _colsum(dh)
        ao[1][...] += _colsum(dh * xn * vi[0][...])
        dn = dh * (1.0 + vi[1][...])
        ao[2][...] += _colsum(dn * xn)
        dx1 = ri[2][...] + _norm_bwd(dn * vi[0][...], xn, r)
        ao[3][...] += _colsum(dx1 * ri[3][...])
        ro[0][...] = dx1
        ro[1][...] = (dx1 * vi[2][...]).astype(BF16)

    return _rowwise(body, "bwd_norm2", x1.shape[0], [(x1, d, 0), (dh2, d, 0), (dx2, d, 0), (mo, d, 0)],
                    [g2, scale2, gate1], [(d, F32), (d, BF16)], [d, d, d, d])


def _bwd_mix(dmixed, attn, y0, gl, b_glu, ga, gs):
    da, ds = attn.shape[1], y0.shape[1]

    def body(ri, vi, ro, ao):
        dan, dsn = ri[0][:, :da], ri[0][:, da:]
        at = ri[1][...]
        ra = _rstd(at)
        an = at * ra
        ao[0][...] += _colsum(dan * an)
        ro[0][...] = _norm_bwd(dan * vi[1][...], an, ra)
        z = _gelu(ri[2][...])
        sg = jax.nn.sigmoid(ri[3][...] + vi[0][...])
        o = z * sg
        rs = _rstd(o)
        on = o * rs
        ao[1][...] += _colsum(dsn * on)
        do = _norm_bwd(dsn * vi[2][...], on, rs)
        ro[2][...] = do * sg
        dgl = do * z * sg * (1.0 - sg)
        ao[2][...] += _colsum(dgl)
        ro[1][...] = dgl.astype(BF16)

    return _rowwise(body, "bwd_mix", attn.shape[0],
                    [(dmixed, da + ds, 0), (attn, da, 0), (y0, ds, 0), (gl, ds, 0)], [b_glu, ga, gs],
                    [(da, F32), (ds, BF16), (ds, F32)], [da, ds, ds])


def _bwd_norm1(x, dh, dx1, g1, scale1):
    d = x.shape[1]

    def body(ri, vi, ro, ao):
        xv, dhv = ri[0][...], ri[1][...]
        r = _rstd(xv)
        xn = xv * r
        ao[0][...] += _colsum(dhv)
        ao[1][...] += _colsum(dhv * xn * vi[0][...])
        dn = dhv * (1.0 + vi[1][...])
        ao[2][...] += _colsum(dn * xn)
        ro[0][...] = ri[2][...] + _norm_bwd(dn * vi[0][...], xn, r)

    return _rowwise(body, "bwd_norm1", x.shape[0], [(x, d, 0), (dh, d, 0), (dx1, d, 0)], [g1, scale1],
                    [(d, F32)], [d, d, d])


def _rope_apply(x, cos, sin, sign):
    first = (lax.broadcasted_iota(jnp.int32, cos.shape, 1) % HEAD_DIM) < (HEAD_DIM // 2)
    outs = []
    for j in range(x.shape[1] // LANES):
        xc = x[:, j * LANES:(j + 1) * LANES]
        rot = jnp.where(first, -pltpu.roll(xc, LANES - HEAD_DIM // 2, 1), pltpu.roll(xc, HEAD_DIM // 2, 1))
        outs.append(xc * cos + sign * (rot * sin))
    return outs


def _rope_fwd(proj, cos, sin, d_attn, d_kv):
    scale = HEAD_DIM ** -0.5
    kcb, vcb = d_attn // d_kv, d_attn // d_kv + 1

    def body(ri, vi, ro, ao):
        c, s = ri[3][...], ri[4][...]
        for j, o in enumerate(_rope_apply(ri[0][...], c, s, 1.0)):
            ro[0][:, j * LANES:(j + 1) * LANES] = (o * scale).astype(BF16)
        for j, o in enumerate(_rope_apply(ri[1][...], c, s, 1.0)):
            ro[1][:, j * LANES:(j + 1) * LANES] = o.astype(BF16)
        ro[2][...] = ri[2][...].astype(BF16)

    return _rowwise(body, "rope_fwd", proj.shape[0],
                    [(proj, d_attn, 0), (proj, d_kv, kcb), (proj, d_kv, vcb), (cos, LANES, 0), (sin, LANES, 0)], [],
                    [(d_attn, BF16), (d_kv, BF16), (d_kv, BF16)], [])


def _rope_bwd(dqr, dkc, dkp, dvc, dvp, cos, sin):
    scale = HEAD_DIM ** -0.5
    d_attn, d_kv = dqr.shape[1], dkc.shape[1]

    def body(ri, vi, ro, ao):
        c, s = ri[5][...], ri[6][...]
        for j, o in enumerate(_rope_apply(ri[0][...], c, s, -1.0)):
            ro[0][:, j * LANES:(j + 1) * LANES] = (o * scale).astype(BF16)
        for j, o in enumerate(_rope_apply(ri[1][...] + ri[2][...], c, s, -1.0)):
            ro[1][:, j * LANES:(j + 1) * LANES] = o.astype(BF16)
        ro[2][...] = (ri[3][...] + ri[4][...]).astype(BF16)

    return _rowwise(body, "rope_bwd", dqr.shape[0],
                    [(dqr, d_attn, 0), (dkc, d_kv, 0), (dkp, d_kv, 0), (dvc, d_kv, 0), (dvp, d_kv, 0),
                     (cos, LANES, 0), (sin, LANES, 0)], [],
                    [(d_attn, BF16), (d_kv, BF16), (d_kv, BF16)], [])


def _attn_probs(q, k, sink_ref, g, n):
    rows = Q_PER_KV * WINDOW
    s = lax.dot_general(q, k, _NT, preferred_element_type=F32)
    qi = lax.broadcasted_iota(jnp.int32, (rows, 2 * WINDOW), 0) % WINDOW + WINDOW
    kj = lax.broadcasted_iota(jnp.int32, (rows, 2 * WINDOW), 1)
    rel = qi - kj
    mask = (rel >= 0) & (rel < WINDOW) & ((n > 0) | (kj >= WINDOW))
    s = jnp.where(mask, s, -1e30)
    sink = jnp.concatenate([jnp.full((WINDOW, 1), sink_ref[g * Q_PER_KV + j], F32) for j in range(Q_PER_KV)], axis=0)
    m = jnp.maximum(jnp.max(s, axis=-1, keepdims=True), sink)
    p = jnp.exp(s - m)
    es = jnp.exp(sink - m)
    l = jnp.sum(p, axis=-1, keepdims=True) + es
    return p, l, es


def _attn_specs(n_kv):
    qspec = pl.BlockSpec((Q_PER_KV, WINDOW, HEAD_DIM), lambda g, n: (g, n, 0))
    cur = pl.BlockSpec((1, WINDOW, HEAD_DIM), lambda g, n: (g, n, 0))
    prev = pl.BlockSpec((1, WINDOW, HEAD_DIM), lambda g, n: (g, jnp.maximum(n - 1, 0), 0))
    return qspec, cur, prev


def _attn_fwd(q, k, v, sinks, comm=None):
    n_kv, t = k.shape[0], k.shape[1]
    rows = Q_PER_KV * WINDOW

    def body(sink_ref, q_ref, kp_ref, kc_ref, vp_ref, vc_ref, o_ref):
        g, n = pl.program_id(0), pl.program_id(1)
        qv = q_ref[...].reshape(rows, HEAD_DIM)
        kv = jnp.concatenate([kp_ref[0], kc_ref[0]], axis=0)
        vv = jnp.concatenate([vp_ref[0], vc_ref[0]], axis=0)
        p, l, _ = _attn_probs(qv, kv, sink_ref, g, n)
        o = jnp.dot(p.astype(BF16), vv, preferred_element_type=F32) / l
        o_ref[...] = o.reshape(Q_PER_KV, WINDOW, HEAD_DIM)

    qspec, cur, prev = _attn_specs(n_kv)
    (out,), carried = _call(
        body, "attn_fwd", (n_kv, t // WINDOW),
        [pl.BlockSpec(memory_space=pltpu.SMEM), qspec, prev, cur, prev, cur], [qspec],
        [jax.ShapeDtypeStruct(q.shape, F32)], [], ("parallel", "arbitrary"), (sinks, q, k, k, v, v), comm)
    return out, carried


def _attn_bwd(q, k, v, o, do, sinks):
    n_kv, t = k.shape[0], k.shape[1]
    rows = Q_PER_KV * WINDOW

    def body(sink_ref, q_ref, kp_ref, kc_ref, vp_ref, vc_ref, o_ref, do_ref,
             dq_ref, dkc_ref, dkp_ref, dvc_ref, dvp_ref, ds_ref):
        g, n = pl.program_id(0), pl.program_id(1)
        qv = q_ref[...].reshape(rows, HEAD_DIM)
        kv = jnp.concatenate([kp_ref[0], kc_ref[0]], axis=0)
        vv = jnp.concatenate([vp_ref[0], vc_ref[0]], axis=0)
        p, l, es = _attn_probs(qv, kv, sink_ref, g, n)
        inv_l = 1.0 / l
        pn = p * inv_l
        dov = do_ref[...].reshape(rows, HEAD_DIM)
        delta = jnp.sum(dov * o_ref[...].reshape(rows, HEAD_DIM), axis=-1, keepdims=True)
        dob = dov.astype(BF16)
        dv = lax.dot_general(pn.astype(BF16), dob, _TN, preferred_element_type=F32)
        dp = lax.dot_general(dob, vv, _NT, preferred_element_type=F32)
        dsb = (pn * (dp - delta)).astype(BF16)
        dq_ref[...] = jnp.dot(dsb, kv, preferred_element_type=F32).reshape(Q_PER_KV, WINDOW, HEAD_DIM)
        dk = lax.dot_general(dsb, qv, _TN, preferred_element_type=F32)
        dkp_ref[0] = dk[:WINDOW]
        dkc_ref[0] = dk[WINDOW:]
        dvp_ref[0] = dv[:WINDOW]
        dvc_ref[0] = dv[WINDOW:]

        @pl.when(n == 0)
        def _():
            ds_ref[...] = jnp.zeros_like(ds_ref)

        ds_ref[...] += -(es * inv_l) * delta

    qspec, cur, prev = _attn_specs(n_kv)
    sspec = pl.BlockSpec((None, rows, 1), lambda g, n: (g, 0, 0))
    kshape = jax.ShapeDtypeStruct(k.shape, F32)
    return _pcall(
        body, name="attn_bwd", grid=(n_kv, t // WINDOW),
        in_specs=[pl.BlockSpec(memory_space=pltpu.SMEM), qspec, prev, cur, prev, cur, qspec, qspec],
        out_specs=[qspec, cur, cur, cur, cur, sspec],
        out_shape=[jax.ShapeDtypeStruct(q.shape, F32), kshape, kshape, kshape, kshape,
                   jax.ShapeDtypeStruct((n_kv, rows, 1), F32)],
        compiler_params=_params(("parallel", "arbitrary")),
    )(sinks, q, k, k, v, v, o, do)


def _cmul(ar, ai, br, bi):
    return ar * br - ai * bi, ar * bi + ai * br


def _scan_consts(ar, ai, half, reverse):
    row = lax.broadcasted_iota(jnp.int32, (SUBLANES, half), 0)
    a2 = _cmul(ar, ai, ar, ai)
    a4 = _cmul(*a2, *a2)
    steps = [(1, ar, ai), (2, *a2), (4, *a4)]
    pr, pi = ar, ai
    pwr = jnp.zeros((SUBLANES, half), F32)
    pwi = jnp.zeros((SUBLANES, half), F32)
    for r in range(SUBLANES):
        sel = row == (SUBLANES - 1 - r if reverse else r)
        pwr = jnp.where(sel, pr, pwr)
        pwi = jnp.where(sel, pi, pwi)
        pr, pi = _cmul(pr, pi, ar, ai)
    return row, steps, pwr, pwi


def _scan8(xr, xi, row, steps, pwr, pwi, cr, ci, reverse):
    for d, er, ei in steps:
        if reverse:
            keep, shift = row < SUBLANES - d, SUBLANES - d
        else:
            keep, shift = row >= d, d
        sr = jnp.where(keep, pltpu.roll(xr, shift, 0), 0.0)
        si = jnp.where(keep, pltpu.roll(xi, shift, 0), 0.0)
        tr, ti = _cmul(er, ei, sr, si)
        xr, xi = xr + tr, xi + ti
    tr, ti = _cmul(pwr, pwi, cr, ci)
    return xr + tr, xi + ti


def _ssm_fwd(proj, ucb0, bexp, cexp, a_cat, d_skip, tt=512, comm=None):
    t = proj.shape[0]
    ngb, cw, two_l = bexp.shape
    half = two_l // 2
    tt = _tile(t, tt, SUBLANES * 2)
    nt = t // tt

    def body(u_ref, b_ref, c_ref, a_ref, d_ref, y_ref, z_ref, st_ref, carry_ref):
        @pl.when(pl.program_id(1) == 0)
        def _():
            carry_ref[...] = jnp.zeros_like(carry_ref)

        u = u_ref[...]
        st_ref[...] = jnp.dot(u.astype(BF16), b_ref[...], preferred_element_type=F32)
        ar, ai = a_ref[:, :half], a_ref[:, half:]
        row, steps, pwr, pwi = _scan_consts(ar, ai, half, False)

        def tile(i, carry):
            base = pl.multiple_of(i * SUBLANES, SUBLANES)
            xr, xi = _scan8(st_ref[pl.ds(base, SUBLANES), :half], st_ref[pl.ds(base, SUBLANES), half:],
                            row, steps, pwr, pwi, carry[0], carry[1], False)
            st_ref[pl.ds(base, SUBLANES), :half] = xr
            st_ref[pl.ds(base, SUBLANES), half:] = xi
            return xr[SUBLANES - 1:, :], xi[SUBLANES - 1:, :]

        cr, ci = lax.fori_loop(0, tt // SUBLANES, tile, (carry_ref[0:1, :half], carry_ref[0:1, half:]))
        carry_ref[0:1, :half] = cr
        carry_ref[0:1, half:] = ci
        y = jnp.dot(st_ref[...].astype(BF16), c_ref[...], preferred_element_type=F32) + d_ref[...] * u
        y_ref[...] = y
        z_ref[...] = _gelu(y).astype(BF16)

    d_ssm = ngb * cw
    return _call(
        body, "ssm_fwd", (ngb, nt),
        [pl.BlockSpec((tt, cw), lambda g, i: (i, ucb0 + g)),
         pl.BlockSpec((None, cw, two_l), lambda g, i: (g, 0, 0)),
         pl.BlockSpec((None, two_l, cw), lambda g, i: (g, 0, 0)),
         pl.BlockSpec((None, 1, two_l), lambda g, i: (g, 0, 0)),
         pl.BlockSpec((1, cw), lambda g, i: (0, g))],
        [pl.BlockSpec((tt, cw), lambda g, i: (i, g)),
         pl.BlockSpec((tt, cw), lambda g, i: (i, g)),
         pl.BlockSpec((tt, two_l), lambda g, i: (i, g))],
        [jax.ShapeDtypeStruct((t, d_ssm), F32), jax.ShapeDtypeStruct((t, d_ssm), BF16),
         jax.ShapeDtypeStruct((t, ngb * two_l), F32)],
        [pltpu.VMEM((SUBLANES, two_l), F32)], ("parallel", "arbitrary"),
        (proj, bexp, cexp, a_cat, d_skip), comm)


def _ssm_bwd(dy0, proj, ucb0, states, bexp, cexp, a_cat, d_skip, tt=512, comm=None):
    t = dy0.shape[0]
    ngb, cw, two_l = bexp.shape
    half = two_l // 2
    tt = _tile(t, tt, SUBLANES * 2)
    nt = t // tt

    def body(dy_ref, u_ref, st_ref, b_ref, c_ref, a_ref, d_ref,
             du_ref, db_ref, dc_ref, da_ref, dd_ref, lam_ref, carry_ref, acc_ref):
        step = pl.program_id(1)

        @pl.when(step == 0)
        def _():
            carry_ref[...] = jnp.zeros_like(carry_ref)
            acc_ref[...] = jnp.zeros_like(acc_ref)
            db_ref[...] = jnp.zeros_like(db_ref)
            dc_ref[...] = jnp.zeros_like(dc_ref)
            dd_ref[...] = jnp.zeros_like(dd_ref)

        dy, u = dy_ref[...], u_ref[...]
        dyb = dy.astype(BF16)
        lam_ref[...] = lax.dot_general(dyb, c_ref[...], _NT, preferred_element_type=F32)
        ar, ai = a_ref[:, :half], -a_ref[:, half:]
        row, steps, pwr, pwi = _scan_consts(ar, ai, half, True)
        last = row == SUBLANES - 1

        def tile(i, carry):
            cr, ci, accr, acci = carry
            base = pl.multiple_of((tt // SUBLANES - 1 - i) * SUBLANES, SUBLANES)
            xr, xi = _scan8(lam_ref[pl.ds(base, SUBLANES), :half], lam_ref[pl.ds(base, SUBLANES), half:],
                            row, steps, pwr, pwi, cr, ci, True)
            lam_ref[pl.ds(base, SUBLANES), :half] = xr
            lam_ref[pl.ds(base, SUBLANES), half:] = xi
            nr = jnp.where(last, cr, pltpu.roll(xr, SUBLANES - 1, 0))
            ni = jnp.where(last, ci, pltpu.roll(xi, SUBLANES - 1, 0))
            sr, si = st_ref[pl.ds(base, SUBLANES), :half], st_ref[pl.ds(base, SUBLANES), half:]
            return xr[0:1, :], xi[0:1, :], accr + sr * nr + si * ni, acci + sr * ni - si * nr

        cr, ci, accr, acci = lax.fori_loop(
            0, tt // SUBLANES, tile,
            (carry_ref[0:1, :half], carry_ref[0:1, half:], acc_ref[:, :half], acc_ref[:, half:]))
        carry_ref[0:1, :half] = cr
        carry_ref[0:1, half:] = ci
        acc_ref[:, :half] = accr
        acc_ref[:, half:] = acci
        lamb = lam_ref[...].astype(BF16)
        du = lax.dot_general(lamb, b_ref[...], _NT, preferred_element_type=F32) + d_ref[...] * dy
        du_ref[...] = du.astype(BF16)
        db_ref[...] += lax.dot_general(u.astype(BF16), lamb, _TN, preferred_element_type=F32)
        dc_ref[...] += lax.dot_general(st_ref[...].astype(BF16), dyb, _TN, preferred_element_type=F32)
        dd_ref[...] += _colsum(dy * u)

        @pl.when(step == nt - 1)
        def _():
            da_ref[...] = _colsum(acc_ref[...])

    d_ssm = ngb * cw
    return _call(
        body, "ssm_bwd", (ngb, nt),
        [pl.BlockSpec((tt, cw), lambda g, i: (nt - 1 - i, g)),
         pl.BlockSpec((tt, cw), lambda g, i: (nt - 1 - i, ucb0 + g)),
         pl.BlockSpec((tt, two_l), lambda g, i: (nt - 1 - i, g)),
         pl.BlockSpec((None, cw, two_l), lambda g, i: (g, 0, 0)),
         pl.BlockSpec((None, two_l, cw), lambda g, i: (g, 0, 0)),
         pl.BlockSpec((None, 1, two_l), lambda g, i: (g, 0, 0)),
         pl.BlockSpec((1, cw), lambda g, i: (0, g))],
        [pl.BlockSpec((tt, cw), lambda g, i: (nt - 1 - i, g)),
         pl.BlockSpec((None, cw, two_l), lambda g, i: (g, 0, 0)),
         pl.BlockSpec((None, two_l, cw), lambda g, i: (g, 0, 0)),
         pl.BlockSpec((None, 1, two_l), lambda g, i: (g, 0, 0)),
         pl.BlockSpec((1, cw), lambda g, i: (0, g))],
        [jax.ShapeDtypeStruct((t, d_ssm), BF16),
         jax.ShapeDtypeStruct((ngb, cw, two_l), F32),
         jax.ShapeDtypeStruct((ngb, two_l, cw), F32),
         jax.ShapeDtypeStruct((ngb, 1, two_l), F32),
         jax.ShapeDtypeStruct((1, d_ssm), F32)],
        [pltpu.VMEM((tt, two_l), F32), pltpu.VMEM((SUBLANES, two_l), F32), pltpu.VMEM((SUBLANES, two_l), F32)],
        ("parallel", "arbitrary"), (dy0, proj, states, bexp, cexp, a_cat, d_skip), comm)


def _zoh(lr, li, ls):
    step = jnp.exp(ls)
    e = jnp.exp(lr * step)
    ar, ai = e * jnp.cos(li * step), e * jnp.sin(li * step)
    den = lr * lr + li * li
    cr = ((ar - 1.0) * lr + ai * li) / den
    ci = (ai * lr - (ar - 1.0) * li) / den
    return step, ar, ai, den, cr, ci


def _ssm_param_fwd(lr, li, ls, br, bi):
    def body(lr_ref, li_ref, ls_ref, br_ref, bi_ref, ar_ref, ai_ref, bbr_ref, bbi_ref):
        _, ar, ai, _, cr, ci = _zoh(lr_ref[...], li_ref[...], ls_ref[...])
        ar_ref[...] = ar
        ai_ref[...] = ai
        bbr, bbi = _cmul(cr, ci, br_ref[...], bi_ref[...])
        bbr_ref[...] = bbr
        bbi_ref[...] = bbi

    small, big = jax.ShapeDtypeStruct(lr.shape, F32), jax.ShapeDtypeStruct(br.shape, F32)
    return _pcall(body, name="ssm_param_fwd", out_shape=[small, small, big, big],
                  compiler_params=_params())(lr, li, ls, br, bi)


def _ssm_param_bwd(lr, li, ls, br, bi, gar, gai, gbr, gbi):
    def body(lr_ref, li_ref, ls_ref, br_ref, bi_ref, gar_ref, gai_ref, gbr_ref, gbi_ref,
             dlr_ref, dli_ref, dls_ref, dbr_ref, dbi_ref):
        lrv, liv = lr_ref[...], li_ref[...]
        step, ar, ai, den, cr, ci = _zoh(lrv, liv, ls_ref[...])
        brv, biv, gr, gi = br_ref[...], bi_ref[...], gbr_ref[...], gbi_ref[...]
        dbr_ref[...] = cr * gr + ci * gi
        dbi_ref[...] = cr * gi - ci * gr
        gcr = jnp.sum(brv * gr + biv * gi, axis=1, keepdims=True)
        gci = jnp.sum(brv * gi - biv * gr, axis=1, keepdims=True)
        gtr = gar_ref[...] + (lrv * gcr - liv * gci) / den
        gti = gai_ref[...] + (lrv * gci + liv * gcr) / den
        qr = (cr * lrv + ci * liv) / den
        qi = (ci * lrv - cr * liv) / den
        gzr = ar * gtr + ai * gti
        gzi = ar * gti - ai * gtr
        dlr_ref[...] = step * gzr - (qr * gcr + qi * gci)
        dli_ref[...] = step * gzi - (qr * gci - qi * gcr)
        gstep = jnp.sum(lrv * gzr + liv * gzi, axis=2, keepdims=True)
        dls_ref[...] = jnp.broadcast_to(step * gstep, step.shape)

    small, big = jax.ShapeDtypeStruct(lr.shape, F32), jax.ShapeDtypeStruct(br.shape, F32)
    return _pcall(body, name="ssm_param_bwd", out_shape=[small, small, small, big, big],
                  compiler_params=_params())(lr, li, ls, br, bi, gar, gai, gbr, gbi)


def _block_diag_in(bb):
    g, h, p = bb.shape
    nb = g // GROUPS_PER_BLOCK
    eye = jnp.eye(GROUPS_PER_BLOCK, dtype=bb.dtype)[None, :, None, :, None]
    e = bb.reshape(nb, GROUPS_PER_BLOCK, h, 1, p) * eye
    return e.reshape(nb, GROUPS_PER_BLOCK * h, GROUPS_PER_BLOCK * p)


def _block_diag_take(e, h, p):
    nb = e.shape[0]
    eye = jnp.eye(GROUPS_PER_BLOCK, dtype=e.dtype)[None, :, None, :, None]
    d = jnp.sum(e.reshape(nb, GROUPS_PER_BLOCK, h, GROUPS_PER_BLOCK, p) * eye, axis=3)
    return d.reshape(nb * GROUPS_PER_BLOCK, h, p)


def _ada_fwd(c_all, w_sh, b_sh, tn=512):
    bsz, d = c_all.shape
    nsh = w_sh.shape[1]
    tn = _tile(nsh, tn, LANES)

    def body(c_ref, w_ref, b_ref, mod_ref, act_ref):
        act = c_ref[...] * jax.nn.sigmoid(c_ref[...])
        act_ref[...] = act
        mod_ref[...] = jnp.dot(act.astype(BF16), w_ref[...].astype(BF16), preferred_element_type=F32) + b_ref[...]

    return _pcall(
        body, name="ada_fwd", grid=(nsh // tn,),
        in_specs=[pl.BlockSpec((bsz, d), lambda j: (0, 0)), pl.BlockSpec((d, tn), lambda j: (0, j)),
                  pl.BlockSpec((1, tn), lambda j: (0, j))],
        out_specs=[pl.BlockSpec((bsz, tn), lambda j: (0, j)), pl.BlockSpec((bsz, d), lambda j: (0, 0))],
        out_shape=[jax.ShapeDtypeStruct((bsz, nsh), F32), jax.ShapeDtypeStruct((bsz, d), F32)],
        compiler_params=_params(("arbitrary",)),
    )(c_all, w_sh, b_sh)


def _adamw(w, g, m, v, name, comm=None):
    r, c = w.shape
    tr = _tile(r, max(SUBLANES, (256 * 1024) // c // SUBLANES * SUBLANES), SUBLANES)
    c1, c2 = 1.0 / (1.0 - ADAM_B1 ** ADAM_STEP), 1.0 / (1.0 - ADAM_B2 ** ADAM_STEP)

    def body(w_ref, g_ref, m_ref, v_ref, d_ref, nm_ref, nv_ref):
        gv = g_ref[...]
        nm = ADAM_B1 * m_ref[...] + (1.0 - ADAM_B1) * gv
        nv = ADAM_B2 * v_ref[...] + (1.0 - ADAM_B2) * (gv * gv)
        nm_ref[...] = nm
        nv_ref[...] = nv
        d_ref[...] = -ADAM_LR * ((nm * c1) / (jnp.sqrt(nv * c2) + ADAM_EPS) + ADAM_WD * w_ref[...])

    spec = pl.BlockSpec((tr, c), lambda i: (i, 0))
    shp = jax.ShapeDtypeStruct((r, c), F32)
    res, carried = _call(body, name, (r // tr,), [spec] * 4, [spec] * 3, [shp] * 3, [], ("parallel",),
                         (w, g, m, v), comm)
    return (res, carried) if comm else res


def _sum_leading(arr, out_dtype, name):
    n, r, c = arr.shape
    tr = _tile(r, max(SUBLANES * 2, (512 * 1024) // (c * n) // (SUBLANES * 2) * (SUBLANES * 2)), SUBLANES * 2)

    def body(x_ref, o_ref):
        acc = x_ref[0].astype(F32)
        for k in range(1, n):
            acc = acc + x_ref[k].astype(F32)
        o_ref[...] = acc.astype(out_dtype)

    return _pcall(body, name=name, grid=(r // tr,),
                  in_specs=[pl.BlockSpec((n, tr, c), lambda i: (0, i, 0))],
                  out_specs=pl.BlockSpec((tr, c), lambda i: (i, 0)),
                  out_shape=jax.ShapeDtypeStruct((r, c), out_dtype),
                  compiler_params=_params(("parallel",)))(arr)


def _place():
    x, y, c = lax.axis_index("x"), lax.axis_index("y"), lax.axis_index("c")
    chips = [(1 - x, y), (x, 1 - y), (1 - x, 1 - y)]
    return x, y, c, chips


def _allgather8(v, name):
    m, n = v.shape

    def body(x_ref, out_ref, send_sems, recv_sems, local_sem):
        x, y, c, chips = _place()
        me, sibling = (x, y, c), (x, y, 1 - c)

        def slot(px, py, pc):
            return out_ref.at[4 * px + 2 * py + pc]

        def copy(k, block, to, src=None):
            return pltpu.make_async_remote_copy(
                src_ref=slot(*block) if src is None else src, dst_ref=slot(*block),
                send_sem=send_sems.at[k], recv_sem=recv_sems.at[k], device_id=to, device_id_type=MESH)

        mine = pltpu.make_async_copy(x_ref, slot(*me), local_sem)
        mine.start()
        first = [copy(0, me, sibling, src=x_ref)]
        first += [copy(1 + j, me, (*chip, c), src=x_ref) for j, chip in enumerate(chips)]
        for cp in first:
            cp.start()
        passed = [copy(4 + j, (*chip, c), sibling) for j, chip in enumerate(chips)]
        for j, chip in enumerate(chips):
            copy(1 + j, (*chip, c), me).wait_recv()
            passed[j].start()
        copy(0, sibling, me).wait_recv()
        for j, chip in enumerate(chips):
            copy(4 + j, (*chip, 1 - c), me).wait_recv()
        for cp in first + passed:
            cp.wait_send()
        mine.wait()

    return _pcall(
        body, name=name, out_shape=jax.ShapeDtypeStruct((N_DEV, m, n), F32),
        in_specs=[pl.BlockSpec(memory_space=pltpu.VMEM)], out_specs=pl.BlockSpec(memory_space=pltpu.VMEM),
        scratch_shapes=[pltpu.SemaphoreType.DMA((7,)), pltpu.SemaphoreType.DMA((7,)), pltpu.SemaphoreType.DMA],
        compiler_params=_params(),
    )(v)


def _scalars(*vals):
    return jnp.stack([jnp.asarray(v, jnp.int32) for v in vals])


def _cast_place(w, chip, name):
    r, cdim = w.shape
    tr = _tile(r, max(SUBLANES * 2, (512 * 1024) // cdim // (SUBLANES * 2) * (SUBLANES * 2)), SUBLANES * 2)

    def body(s_ref, w_ref, o_ref):
        o_ref[...] = w_ref[...].astype(BF16)

    grid_spec = pltpu.PrefetchScalarGridSpec(
        num_scalar_prefetch=1, grid=(r // tr,),
        in_specs=[pl.BlockSpec((tr, cdim), lambda i, s: (i, 0))],
        out_specs=pl.BlockSpec((None, tr, cdim), lambda i, s: (s[0], i, 0)))
    return _pcall(body, name=name, grid_spec=grid_spec, out_shape=jax.ShapeDtypeStruct((N_CHIPS, r, cdim), BF16),
                  compiler_params=_params(("parallel",)))(_scalars(chip), w)


def _gather_weights(bufs):
    nw = len(bufs)

    def body(*refs):
        outs = refs[nw:2 * nw]
        send_sems, recv_sems = refs[2 * nw:]
        x, y, c, chips = _place()
        me, sibling = (x, y, c), (x, y, 1 - c)

        def copy(w, k, chip, hc, to):
            h = outs[w].shape[1] // 2
            ref = outs[w].at[2 * chip[0] + chip[1], pl.ds(pl.multiple_of(hc * h, SUBLANES * 2), h)]
            return pltpu.make_async_remote_copy(
                src_ref=ref, dst_ref=ref, send_sem=send_sems.at[w, k], recv_sem=recv_sems.at[w, k],
                device_id=to, device_id_type=MESH)

        sent = []
        for w in range(nw):
            for k, chip in enumerate(chips):
                sent.append(copy(w, k, (x, y), c, (*chip, c)))
                sent[-1].start()
        for w in range(nw):
            for k, chip in enumerate(chips):
                copy(w, k, chip, c, me).wait_recv()
                sent.append(copy(w, 3 + k, chip, c, sibling))
                sent[-1].start()
        for w in range(nw):
            for k, chip in enumerate(chips):
                copy(w, 3 + k, chip, 1 - c, me).wait_recv()
        for cp in sent:
            cp.wait_send()

    any_spec = pl.BlockSpec(memory_space=pl.ANY)
    return _pcall(
        body, name="gather_weights",
        out_shape=[jax.ShapeDtypeStruct(b.shape, b.dtype) for b in bufs],
        in_specs=[any_spec] * nw, out_specs=[any_spec] * nw,
        input_output_aliases={w: w for w in range(nw)},
        scratch_shapes=[pltpu.SemaphoreType.DMA((nw, 6)), pltpu.SemaphoreType.DMA((nw, 6))],
        compiler_params=_params(),
    )(*bufs)


class _Comm:
    def __init__(self, ins, outs, aliases, n_sems, start, finish):
        self.ins, self.outs, self.aliases, self.n_sems = ins, outs, aliases, n_sems
        self.start, self.finish = start, finish


def _comm_gather_ici(bufs, part=0, n_parts=1):
    def copies(outs, send_sems, recv_sems, incoming):
        x, y, c, chips = _place()
        res = []
        for w, out in enumerate(outs):
            rows = out.shape[1] // 2 // n_parts
            for k, chip in enumerate(chips):
                blk = chip if incoming else (x, y)
                first = pl.multiple_of((c * n_parts + part) * rows, SUBLANES * 2)
                ref = out.at[2 * blk[0] + blk[1], pl.ds(first, rows)]
                res.append(pltpu.make_async_remote_copy(
                    src_ref=ref, dst_ref=ref, send_sem=send_sems.at[w * len(chips) + k],
                    recv_sem=recv_sems.at[w * len(chips) + k], device_id=(*chip, c), device_id_type=MESH))
        return res

    def start(ci, co, send_sems, recv_sems):
        for cp in copies(co, send_sems, recv_sems, False):
            cp.start()

    def finish(ci, co, send_sems, recv_sems):
        for cp in copies(co, send_sems, recv_sems, True):
            cp.wait_recv()
        for cp in copies(co, send_sems, recv_sems, False):
            cp.wait_send()

    return _Comm(list(bufs), [jax.ShapeDtypeStruct(b.shape, b.dtype) for b in bufs],
                 {w: w for w in range(len(bufs))}, (N_CHIPS - 1) * len(bufs), start, finish)


def _forward_halves(bufs, name):
    nw = len(bufs)

    def body(*refs):
        outs = refs[nw:2 * nw]
        send_sems, recv_sems = refs[2 * nw:]
        x, y, c, chips = _place()

        def copy(w, k, hc):
            chip = chips[k]
            h = outs[w].shape[1] // 2
            ref = outs[w].at[2 * chip[0] + chip[1], pl.ds(pl.multiple_of(hc * h, SUBLANES * 2), h)]
            return pltpu.make_async_remote_copy(
                src_ref=ref, dst_ref=ref, send_sem=send_sems.at[w, k], recv_sem=recv_sems.at[w, k],
                device_id=(x, y, 1 - c), device_id_type=MESH)

        pairs = [(w, k) for w in range(nw) for k in range(len(chips))]
        for w, k in pairs:
            copy(w, k, c).start()
        for w, k in pairs:
            copy(w, k, 1 - c).wait_recv()
        for w, k in pairs:
            copy(w, k, c).wait_send()

    any_spec = pl.BlockSpec(memory_space=pl.ANY)
    return _pcall(
        body, name=name,
        out_shape=[jax.ShapeDtypeStruct(b.shape, b.dtype) for b in bufs],
        in_specs=[any_spec] * nw, out_specs=[any_spec] * nw,
        input_output_aliases={w: w for w in range(nw)},
        scratch_shapes=[pltpu.SemaphoreType.DMA((nw, N_CHIPS - 1)), pltpu.SemaphoreType.DMA((nw, N_CHIPS - 1))],
        compiler_params=_params(),
    )(*bufs)


def _comm_chip_exchange(psums):
    def copies(srcs, dsts, send_sems, recv_sems):
        x, y, c, chips = _place()
        return [pltpu.make_async_remote_copy(
            src_ref=src.at[2 * chip[0] + chip[1]], dst_ref=dst.at[k], send_sem=send_sems.at[w * len(chips) + k],
            recv_sem=recv_sems.at[w * len(chips) + k], device_id=(*chip, c), device_id_type=MESH)
            for w, (src, dst) in enumerate(zip(srcs, dsts)) for k, chip in enumerate(chips)]

    def start(ci, co, send_sems, recv_sems):
        for cp in copies(ci, co, send_sems, recv_sems):
            cp.start()

    def finish(ci, co, send_sems, recv_sems):
        cps = copies(ci, co, send_sems, recv_sems)
        for cp in cps:
            cp.wait_recv()
        for cp in cps:
            cp.wait_send()

    outs = [jax.ShapeDtypeStruct((N_CHIPS - 1,) + p.shape[1:], p.dtype) for p in psums]
    return _Comm(list(psums), outs, {}, (N_CHIPS - 1) * len(psums), start, finish)


def _pair_exchange(grads, name):
    nw = len(grads)

    def body(*refs):
        ins, outs = refs[:nw], refs[nw:2 * nw]
        send_sems, recv_sems = refs[2 * nw:]
        x, y, c, _ = _place()
        sibling = (x, y, 1 - c)

        def copy(w, k):
            return pltpu.make_async_remote_copy(
                src_ref=ins[w].at[k, 1 - c], dst_ref=outs[w].at[k],
                send_sem=send_sems.at[w, k], recv_sem=recv_sems.at[w, k], device_id=sibling, device_id_type=MESH)

        copies = [copy(w, k) for w in range(nw) for k in range(N_CHIPS)]
        for cp in copies:
            cp.start()
        for cp in copies:
            cp.wait_recv()
        for cp in copies:
            cp.wait_send()

    any_spec = pl.BlockSpec(memory_space=pl.ANY)
    return _pcall(
        body, name=name,
        out_shape=[jax.ShapeDtypeStruct((N_CHIPS,) + g.shape[2:], g.dtype) for g in grads],
        in_specs=[any_spec] * nw, out_specs=[any_spec] * nw,
        scratch_shapes=[pltpu.SemaphoreType.DMA((nw, N_CHIPS)), pltpu.SemaphoreType.DMA((nw, N_CHIPS))],
        compiler_params=_params(),
    )(*grads)


def _pair_sum(view, recv, core, name):
    n, _, h, cdim = view.shape
    th = _tile(h, max(SUBLANES * 2, (512 * 1024) // cdim // (SUBLANES * 2) * (SUBLANES * 2)), SUBLANES * 2)

    def body(s_ref, a_ref, b_ref, o_ref):
        o_ref[...] = (a_ref[...].astype(F32) + b_ref[...].astype(F32)).astype(BF16)

    grid_spec = pltpu.PrefetchScalarGridSpec(
        num_scalar_prefetch=1, grid=(n, h // th),
        in_specs=[pl.BlockSpec((None, None, th, cdim), lambda k, i, s: (k, s[0], i, 0)),
                  pl.BlockSpec((None, th, cdim), lambda k, i, s: (k, i, 0))],
        out_specs=pl.BlockSpec((None, th, cdim), lambda k, i, s: (k, i, 0)))
    return _pcall(body, name=name, grid_spec=grid_spec, out_shape=jax.ShapeDtypeStruct((n, h, cdim), BF16),
                  compiler_params=_params(("parallel", "parallel")))(_scalars(core), view, recv)


def _chip_exchange(psums):
    nw = len(psums)

    def body(*refs):
        ins, outs = refs[:nw], refs[nw:2 * nw]
        send_sems, recv_sems = refs[2 * nw:]
        x, y, c, chips = _place()

        def copy(w, k):
            chip = chips[k]
            return pltpu.make_async_remote_copy(
                src_ref=ins[w].at[2 * chip[0] + chip[1]], dst_ref=outs[w].at[k],
                send_sem=send_sems.at[w, k], recv_sem=recv_sems.at[w, k], device_id=(*chip, c), device_id_type=MESH)

        copies = [copy(w, k) for w in range(nw) for k in range(len(chips))]
        for cp in copies:
            cp.start()
        for cp in copies:
            cp.wait_recv()
        for cp in copies:
            cp.wait_send()

    any_spec = pl.BlockSpec(memory_space=pl.ANY)
    return _pcall(
        body, name="grad_chip_exchange",
        out_shape=[jax.ShapeDtypeStruct((N_CHIPS - 1,) + p.shape[1:], p.dtype) for p in psums],
        in_specs=[any_spec] * nw, out_specs=[any_spec] * nw,
        scratch_shapes=[pltpu.SemaphoreType.DMA((nw, N_CHIPS - 1)), pltpu.SemaphoreType.DMA((nw, N_CHIPS - 1))],
        compiler_params=_params(),
    )(*psums)


def _chip_sum(psums, recv, chip, core, name):
    _, h, cdim = psums.shape
    th = _tile(h, max(SUBLANES * 2, (256 * 1024) // cdim // (SUBLANES * 2) * (SUBLANES * 2)), SUBLANES * 2)

    def body(chip_ref, core_ref, a_ref, b_ref, o_ref):
        acc = a_ref[...].astype(F32)
        for k in range(N_CHIPS - 1):
            acc = acc + b_ref[k].astype(F32)
        o_ref[...] = acc

    grid_spec = pltpu.PrefetchScalarGridSpec(
        num_scalar_prefetch=2, grid=(h // th,),
        in_specs=[pl.BlockSpec((None, th, cdim), lambda i, s, t: (s[0], i, 0)),
                  pl.BlockSpec((N_CHIPS - 1, th, cdim), lambda i, s, t: (0, i, 0))],
        out_specs=pl.BlockSpec((None, th, cdim), lambda i, s, t: (t[0], i, 0)))
    return _pcall(body, name=name, grid_spec=grid_spec, out_shape=jax.ShapeDtypeStruct((2, h, cdim), F32),
                  compiler_params=_params(("parallel",)))(_scalars(chip), _scalars(core), psums, recv)


def _share_halves(bufs):
    nw = len(bufs)

    def body(*refs):
        outs = refs[nw:2 * nw]
        send_sems, recv_sems = refs[2 * nw:]
        x, y, c, _ = _place()
        copies = [pltpu.make_async_remote_copy(
            src_ref=outs[w].at[c], dst_ref=outs[w].at[c], send_sem=send_sems.at[w], recv_sem=recv_sems.at[w],
            device_id=(x, y, 1 - c), device_id_type=MESH) for w in range(nw)]
        for cp in copies:
            cp.start()
        for w in range(nw):
            pltpu.make_async_remote_copy(
                src_ref=outs[w].at[1 - c], dst_ref=outs[w].at[1 - c], send_sem=send_sems.at[w],
                recv_sem=recv_sems.at[w], device_id=(x, y, 1 - c), device_id_type=MESH).wait_recv()
        for cp in copies:
            cp.wait_send()

    any_spec = pl.BlockSpec(memory_space=pl.ANY)
    return _pcall(
        body, name="grad_share_halves",
        out_shape=[jax.ShapeDtypeStruct(b.shape, b.dtype) for b in bufs],
        in_specs=[any_spec] * nw, out_specs=[any_spec] * nw,
        input_output_aliases={w: w for w in range(nw)},
        scratch_shapes=[pltpu.SemaphoreType.DMA((nw,)), pltpu.SemaphoreType.DMA((nw,))],
        compiler_params=_params(),
    )(*bufs)


def _pack(arrays):
    flat = jnp.concatenate([a.reshape(-1).astype(F32) for a in arrays])
    unit = 2 * SUBLANES * PACK_COLS
    pad = (-flat.shape[0]) % unit
    return jnp.pad(flat, (0, pad)).reshape(-1, PACK_COLS)


def _unpack(buf, shapes):
    flat, out, off = buf.reshape(-1), [], 0
    for s in shapes:
        n = math.prod(s)
        out.append(flat[off:off + n].reshape(s))
        off += n
    return out


def kernel(x, c, w_ada, b_ada, norm1_g, w_in, sinks, ssm_lam_re, ssm_lam_im, ssm_log_step, ssm_b_re, ssm_b_im, ssm_c_re, ssm_c_im, ssm_d, w_glu, b_glu, attn_out_g, ssm_out_g, w_out, norm2_g, w_ff1, w_ff2, final_g, loss_target, m_w_ada, m_b_ada, m_norm1_g, m_w_in, m_sinks, m_ssm_lam_re, m_ssm_lam_im, m_ssm_log_step, m_ssm_b_re, m_ssm_b_im, m_ssm_c_re, m_ssm_c_im, m_ssm_d, m_w_glu, m_b_glu, m_attn_out_g, m_ssm_out_g, m_w_out, m_norm2_g, m_w_ff1, m_w_ff2, m_final_g, v_w_ada, v_b_ada, v_norm1_g, v_w_in, v_sinks, v_ssm_lam_re, v_ssm_lam_im, v_ssm_log_step, v_ssm_b_re, v_ssm_b_im, v_ssm_c_re, v_ssm_c_im, v_ssm_d, v_w_glu, v_b_glu, v_attn_out_g, v_ssm_out_g, v_w_out, v_norm2_g, v_w_ff1, v_w_ff2, v_final_g):
    t, d = x.shape[1], x.shape[2]
    d_attn, d_ssm = attn_out_g.shape[1], ssm_d.shape[1]
    d_in = w_in.shape[2] * N_CHIPS
    d_kv = (d_in - d_attn - d_ssm) // 2
    n_q, n_kv = d_attn // HEAD_DIM, d_kv // HEAD_DIM
    n_grp = ssm_lam_re.shape[1]
    assert n_q == n_kv * Q_PER_KV and t % WINDOW == 0 and d_ssm == n_grp * SSM_GROUP
    assert d_kv % LANES == 0 and d_attn % d_kv == 0 and n_grp % GROUPS_PER_BLOCK == 0
    cw = GROUPS_PER_BLOCK * SSM_GROUP
    ucb0 = (d_attn + 2 * d_kv) // cw
    assert (d_attn + 2 * d_kv) % cw == 0
    xi, yi, ci = lax.axis_index("x"), lax.axis_index("y"), lax.axis_index("c")
    chip = 2 * xi + yi
    dev = 2 * chip + ci
    xs, tgt = x[0], loss_target[0]
    vec = lambda a: a.reshape(1, -1)

    n_ada = w_ada.shape[2]
    c_all = _allgather8(c.reshape(SUBLANES, d // SUBLANES), "gather_c").reshape(N_DEV, d)
    b_sh = lax.dynamic_slice_in_dim(b_ada, chip * n_ada, n_ada, axis=1)
    mod_sh, c_act = _ada_fwd(c_all, w_ada[0], b_sh)
    mod_all = _allgather8(mod_sh, "gather_mod")
    mod_me = lax.dynamic_index_in_dim(mod_all[0::2], dev, axis=1, keepdims=False)
    mod_me = mod_me.reshape(N_CHIPS * n_ada // d, 1, d)
    shift1, scale1, gate1, shift2, scale2, gate2 = [mod_me[i] for i in range(N_MOD)]

    own = {n: _cast_place(w[0], chip, "cast_" + n)
           for n, w in (("w_in", w_in), ("w_glu", w_glu), ("w_out", w_out), ("w_ff1", w_ff1), ("w_ff2", w_ff2))}
    win_s, = _gather_weights([own["w_in"]])
    half_view = lambda g, w: g.reshape(N_CHIPS, 2, w.shape[0] // 2, w.shape[1])

    g3 = lambda a: a.reshape(n_grp, 1, STATE)
    lr3, li3 = g3(ssm_lam_re[0]), g3(ssm_lam_im[0])
    ls3 = jnp.broadcast_to(ssm_log_step[0].reshape(n_grp, 1, 1), (n_grp, 1, STATE))
    b_re3, b_im3 = ssm_b_re[0].transpose(0, 2, 1), ssm_b_im[0].transpose(0, 2, 1)
    a_re, a_im, bb_re, bb_im = _ssm_param_fwd(lr3, li3, ls3, b_re3, b_im3)
    ngb = n_grp // GROUPS_PER_BLOCK
    a_cat = jnp.concatenate([a_re.reshape(ngb, 1, -1), a_im.reshape(ngb, 1, -1)], axis=-1)
    bexp = jnp.concatenate([_block_diag_in(bb_re), _block_diag_in(bb_im)], axis=-1).astype(BF16)
    cexp = jnp.concatenate([_block_diag_in(ssm_c_re[0]), -_block_diag_in(ssm_c_im[0])], axis=-1)
    cexp = cexp.transpose(0, 2, 1).astype(BF16)

    half = HEAD_DIM // 2
    inv_freq = ROPE_THETA ** (-jnp.arange(half, dtype=F32) / half)
    ang = jnp.arange(t, dtype=F32)[:, None] * inv_freq[None, :]
    cos = jnp.tile(jnp.cos(ang), (1, LANES // half))
    sin = jnp.tile(jnp.sin(ang), (1, LANES // half))

    h = _norm_mod(xs, norm1_g, scale1, shift1)
    proj, small_ici = _matmul(h, win_s, "nn", "mm_in", [F32], b_stacked=True, tn=win_s.shape[2],
                              comm=_comm_gather_ici([own["w_glu"], own["w_out"]]))
    qr, kr, vb = _rope_fwd(proj, cos, sin, d_attn, d_kv)
    heads = lambda a, n: a.reshape(t, n, HEAD_DIM).transpose(1, 0, 2)
    unheads = lambda a: a.transpose(1, 0, 2).reshape(t, -1)
    qh, kh, vh = heads(qr, n_q), heads(kr, n_kv), heads(vb, n_kv)
    oh, (wff1_ici,) = _attn_fwd(qh, kh, vh, sinks[0], comm=_comm_gather_ici([own["w_ff1"]], 0, 2))
    attn = unheads(oh)
    (y0, z, states), (wff1_ici,) = _ssm_fwd(proj, ucb0, bexp, cexp, a_cat, ssm_d,
                                            comm=_comm_gather_ici([wff1_ici], 1, 2))
    wglu_s, wout_s, wff1_s = _forward_halves([*small_ici, wff1_ici], "forward_halves_mix")
    wglu = wglu_s.reshape(d_ssm, d_ssm)
    wout = wout_s.reshape(d_attn + d_ssm, d)
    gl = _matmul(z, wglu, "nn", "mm_glu", [F32])
    mixed = _mix(attn, y0, gl, b_glu, attn_out_g, ssm_out_g)
    mo = _matmul(mixed, wout, "nn", "mm_out", [F32])
    x1, h2 = _res_norm_mod(xs, mo, gate1, norm2_g, scale2, shift2)

    def relu2(acc):
        r = jnp.maximum(acc, 0.0)
        return acc, r * r

    (a_act, rr), (wff2_ici,) = _matmul(h2, wff1_s, "nn", "mm_ff1", [BF16, BF16], epilogue=relu2, b_stacked=True,
                                       comm=_comm_gather_ici([own["w_ff2"]]))
    wff2 = _forward_halves([wff2_ici], "forward_halves_ff2")[0].reshape(-1, d)
    ff = _matmul(rr, wff2, "nn", "mm_ff2", [F32])
    dx2, dff, loss_cols, dgf, dgate2 = _final(x1, ff, tgt, gate2, vec(final_g))
    loss = lax.psum(0.5 * jnp.sum(loss_cols) / d, ("x", "y", "c"))

    d_relu2 = lambda acc, av: (acc * 2.0 * jnp.maximum(av.astype(F32), 0.0),)
    da = _matmul(dff, wff2, "nt", "mm_dff2", [BF16], epilogue=d_relu2, extras=(a_act,))
    v_ff2 = half_view(_matmul(rr, dff, "tn", "mm_gw_ff2", [BF16]), w_ff2[0])
    p_ff2 = _pair_sum(v_ff2, _pair_exchange([v_ff2], "pair_exchange_ff2")[0], ci, "pair_sum_ff2")
    gw_ff1, (r_ff2,) = _matmul(h2, da, "tn", "mm_gw_ff1", [BF16], out_stacked=N_CHIPS,
                               comm=_comm_chip_exchange([p_ff2]))
    v_ff1 = half_view(gw_ff1, w_ff1[0])
    p_ff1 = _pair_sum(v_ff1, _pair_exchange([v_ff1], "pair_exchange_ff1")[0], ci, "pair_sum_ff1")
    dh2, (r_ff1,) = _matmul(da, wff1_s, "nt", "mm_dff1", [F32], b_stacked=True, comm=_comm_chip_exchange([p_ff1]))
    dx1, dmo, dshift2, dscale2, dg2, dgate1 = _bwd_norm2(x1, dh2, dx2, mo, norm2_g, scale2, gate1)
    dmixed = _matmul(dmo, wout, "nt", "mm_dout", [F32])
    gw_out = _matmul(mixed, dmo, "tn", "mm_gw_out", [BF16])
    dattn, dgl, dzp, dga, dgs, dbglu = _bwd_mix(dmixed, attn, y0, gl, b_glu, attn_out_g, ssm_out_g)
    d_gelu = lambda acc, dz, yv: ((acc + dz) * _gelu_grad(yv),)
    dy0 = _matmul(dgl, wglu, "nt", "mm_dglu", [F32], epilogue=d_gelu, extras=(dzp, y0))
    gw_glu = _matmul(z, dgl, "tn", "mm_gw_glu", [BF16])
    v_mix = [half_view(gw_glu, w_glu[0]), half_view(gw_out, w_out[0])]
    p_mix = [_pair_sum(v, p, ci, "pair_sum_" + n)
             for n, v, p in zip(("glu", "out"), v_mix, _pair_exchange(v_mix, "pair_exchange_mix"))]
    (du, dbexp, dcexp, da_bar, dd), r_mix = _ssm_bwd(dy0, proj, ucb0, states, bexp, cexp, a_cat, ssm_d,
                                                     comm=_comm_chip_exchange(p_mix))
    doh = heads(dattn, n_q)
    dqh, dkc, dkp, dvc, dvp, dsink = _attn_bwd(qh, kh, vh, oh, doh, sinks[0])
    up = lambda a: jnp.concatenate([unheads(a)[WINDOW:], jnp.zeros((WINDOW, d_kv), F32)], axis=0)
    dq, dk, dv = _rope_bwd(unheads(dqh), unheads(dkc), up(dkp), unheads(dvc), up(dvp), cos, sin)
    dproj = jnp.concatenate([dq, dk, dv, du], axis=1)
    dh = _matmul(dproj, win_s, "nt", "mm_din", [F32], b_stacked=True, tk=win_s.shape[2])
    gw_in = _matmul(h, dproj, "tn", "mm_gw_in", [BF16], out_stacked=N_CHIPS, tn=win_s.shape[2])
    grad_x, dshift1, dscale1, dg1 = _bwd_norm1(xs, dh, dx1, norm1_g, scale1)

    half_l = GROUPS_PER_BLOCK * STATE
    ga_re = da_bar[:, 0, :half_l].reshape(n_grp, 1, STATE)
    ga_im = da_bar[:, 0, half_l:].reshape(n_grp, 1, STATE)
    gbb_re = _block_diag_take(dbexp[:, :, :half_l], SSM_GROUP, STATE)
    gbb_im = _block_diag_take(dbexp[:, :, half_l:], SSM_GROUP, STATE)
    dcexp_t = dcexp.transpose(0, 2, 1)
    gc_re = _block_diag_take(dcexp_t[:, :, :half_l], SSM_GROUP, STATE)
    gc_im = -_block_diag_take(dcexp_t[:, :, half_l:], SSM_GROUP, STATE)
    dmod = jnp.concatenate([dshift1, dscale1, dgate1, dshift2, dscale2, dgate2], axis=1)
    dsinks = dsink.reshape(n_q, WINDOW).sum(axis=1)
    pieces = [dmod, dg1, dsinks, ga_re, ga_im, gbb_re, gbb_im, gc_re, gc_im, dd, dbglu, dga, dgs, dg2, dgf]
    gathered = _allgather8(_pack(pieces), "gather_small")
    summed = _sum_leading(gathered, F32, "sum_small")
    (g_b_ada, g_norm1, g_sinks, ga_re, ga_im, gbb_re, gbb_im, g_c_re, g_c_im, g_d, g_b_glu, g_attn_g, g_ssm_g,
     g_norm2, g_final) = _unpack(summed, [p.shape for p in pieces])
    g_lr, g_li, g_ls, g_b_re3, g_b_im3 = _ssm_param_bwd(lr3, li3, ls3, b_re3, b_im3, ga_re, ga_im, gbb_re, gbb_im)
    small_grads = [
        g_b_ada, g_norm1, g_sinks.reshape(1, -1), g_lr.reshape(1, n_grp, STATE), g_li.reshape(1, n_grp, STATE),
        g_ls[:, 0, 0].reshape(1, n_grp), g_b_re3.transpose(0, 2, 1)[None], g_b_im3.transpose(0, 2, 1)[None],
        g_c_re[None], g_c_im[None], g_d, g_b_glu, g_attn_g, g_ssm_g, g_norm2, g_final.reshape(-1)]
    small_w = [b_ada, norm1_g, sinks, ssm_lam_re, ssm_lam_im, ssm_log_step, ssm_b_re, ssm_b_im, ssm_c_re,
               ssm_c_im, ssm_d, b_glu, attn_out_g, ssm_out_g, norm2_g, final_g]
    small_m = [m_b_ada, m_norm1_g, m_sinks, m_ssm_lam_re, m_ssm_lam_im, m_ssm_log_step, m_ssm_b_re, m_ssm_b_im,
               m_ssm_c_re, m_ssm_c_im, m_ssm_d, m_b_glu, m_attn_out_g, m_ssm_out_g, m_norm2_g, m_final_g]
    small_v = [v_b_ada, v_norm1_g, v_sinks, v_ssm_lam_re, v_ssm_lam_im, v_ssm_log_step, v_ssm_b_re, v_ssm_b_im,
               v_ssm_c_re, v_ssm_c_im, v_ssm_d, v_b_glu, v_attn_out_g, v_ssm_out_g, v_norm2_g, v_final_g]
    small_grads = [g.reshape(w.shape) for g, w in zip(small_grads, small_w)]
    s_delta, s_m, s_v = _adamw(_pack(small_w), _pack(small_grads), _pack(small_m), _pack(small_v), "adamw_small")
    shapes = [w.shape for w in small_w]
    s_delta, s_m, s_v = _unpack(s_delta, shapes), _unpack(s_m, shapes), _unpack(s_v, shapes)

    dmod_rows = gathered.reshape(N_DEV, -1)[:, :dmod.shape[1]]
    dmod_sh = lax.dynamic_slice_in_dim(dmod_rows, chip * n_ada, n_ada, axis=1)
    g_w_ada = _matmul(c_act, dmod_sh, "tn", "mm_gw_ada", [F32], tk=N_DEV, precision=lax.Precision.HIGHEST)

    v_in = half_view(gw_in, w_in[0])
    p_in = _pair_sum(v_in, _pair_exchange([v_in], "pair_exchange_in")[0], ci, "pair_sum_in")
    upd_ada, (r_in,) = _adamw(w_ada[0], g_w_ada, m_w_ada[0], v_w_ada[0], "adamw_w_ada",
                              comm=_comm_chip_exchange([p_in]))
    big_w = [w_in[0], w_glu[0], w_out[0], w_ff1[0], w_ff2[0]]
    psums = [p_in, *p_mix, p_ff1, p_ff2]
    recvd = [r_in, *r_mix, r_ff1, r_ff2]
    halves = [_chip_sum(p, r, chip, ci, f"chip_sum_{i}") for i, (p, r) in enumerate(zip(psums, recvd))]
    big_grads = [s.reshape(w.shape) for s, w in zip(_share_halves(halves), big_w)]

    big_names = ["w_in", "w_glu", "w_out", "w_ff1", "w_ff2"]
    big_m = [m_w_in[0], m_w_glu[0], m_w_out[0], m_w_ff1[0], m_w_ff2[0]]
    big_v = [v_w_in[0], v_w_glu[0], v_w_out[0], v_w_ff1[0], v_w_ff2[0]]
    big_upd = {n: _adamw(w, g, m, v, "adamw_" + n) for n, w, g, m, v in zip(big_names, big_w, big_grads, big_m, big_v)}
    big_upd["w_ada"] = upd_ada
    big_grad = dict(zip(big_names, big_grads), w_ada=g_w_ada)
    big_names = ["w_ada"] + big_names

    order = ["w_ada", "b_ada", "norm1_g", "w_in", "sinks", "ssm_lam_re", "ssm_lam_im", "ssm_log_step", "ssm_b_re",
             "ssm_b_im", "ssm_c_re", "ssm_c_im", "ssm_d", "w_glu", "b_glu", "attn_out_g", "ssm_out_g", "w_out",
             "norm2_g", "w_ff1", "w_ff2", "final_g"]
    small_names = [n for n in order if n not in big_names]
    grads, deltas, new_m, new_v = {}, {}, {}, {}
    for i, n in enumerate(small_names):
        grads[n], deltas[n], new_m[n], new_v[n] = small_grads[i], s_delta[i], s_m[i], s_v[i]
    for n in big_names:
        grads[n] = big_grad[n][None]
        deltas[n], new_m[n], new_v[n] = [a[None] for a in big_upd[n]]
    return (loss, grad_x[None], *[grads[n] for n in order], *[deltas[n] for n in order],
            *[new_m[n] for n in order], *[new_v[n] for n in order])
```

```python
import functools
import math

import jax
import jax.numpy as jnp
from jax import lax
from jax.experimental import pallas as pl
from jax.experimental.pallas import tpu as pltpu

F32 = jnp.float32
BF16 = jnp.bfloat16
MESH = pl.DeviceIdType.MESH

EPS = 1e-6
HEAD_DIM = 64
Q_PER_KV = 8
WINDOW = 128
ROPE_THETA = 10000.0
SSM_GROUP = 16
STATE = 64
GROUPS_PER_BLOCK = 16
SCAN_UNROLL = 2
N_MOD = 6
N_CHIPS = 4
N_DEV = 8
ADAM_LR = 0.001
ADAM_B1 = 0.9
ADAM_B2 = 0.999
ADAM_EPS = 1e-08
ADAM_WD = 0.01
ADAM_STEP = 10
LANES = 128
SUBLANES = 8
VMEM_LIMIT = 56 * 1024 * 1024
PACK_COLS = 512


def _pcall(body, **kw):
    return pl.pallas_call(body, **kw)


def _params(sem=None):
    return pltpu.CompilerParams(dimension_semantics=sem, vmem_limit_bytes=VMEM_LIMIT)


def _call(body, name, grid, in_specs, out_specs, out_shape, scratch, sem, operands, comm=None):
    if comm is None:
        res = _pcall(body, name=name, grid=grid, in_specs=in_specs, out_specs=out_specs, out_shape=out_shape,
                     scratch_shapes=scratch, compiler_params=_params(sem))(*operands)
        return tuple(res), ()
    n_in, n_out, n_ci, n_co = len(in_specs), len(out_specs), len(comm.ins), len(comm.outs)

    def carrying(*refs):
        ins, ci = refs[:n_in], refs[n_in:n_in + n_ci]
        outs = refs[n_in + n_ci:n_in + n_ci + n_out]
        co = refs[n_in + n_ci + n_out:n_in + n_ci + n_out + n_co]
        rest, send_sems, recv_sems = refs[n_in + n_ci + n_out + n_co:-2], refs[-2], refs[-1]
        ids = [pl.program_id(a) for a in range(len(grid))]
        first = functools.reduce(lambda p, q: p & q, [i == 0 for i in ids])
        last = functools.reduce(lambda p, q: p & q, [i == g - 1 for i, g in zip(ids, grid)])

        @pl.when(first)
        def _():
            comm.start(ci, co, send_sems, recv_sems)

        body(*ins, *outs, *rest)

        @pl.when(last)
        def _():
            comm.finish(ci, co, send_sems, recv_sems)

    any_spec = pl.BlockSpec(memory_space=pl.ANY)
    res = _pcall(
        carrying, name=name, grid=grid, in_specs=list(in_specs) + [any_spec] * n_ci,
        out_specs=list(out_specs) + [any_spec] * n_co, out_shape=list(out_shape) + list(comm.outs),
        input_output_aliases={n_in + ci: n_out + co for ci, co in comm.aliases.items()},
        scratch_shapes=list(scratch) + [pltpu.SemaphoreType.DMA((comm.n_sems,)), pltpu.SemaphoreType.DMA((comm.n_sems,))],
        compiler_params=_params(("arbitrary",) * len(grid)),
    )(*operands, *comm.ins)
    return tuple(res[:n_out]), tuple(res[n_out:])


def _tile(n, want, unit):
    if n <= want:
        return n
    t = (want // unit) * unit
    while t > unit and n % t:
        t -= unit
    assert n % t == 0, (n, want, unit)
    return t


_NN = (((1,), (0,)), ((), ()))
_NT = (((1,), (1,)), ((), ()))
_TN = (((0,), (0,)), ((), ()))


def _matmul(a, b, mode, name, out_dtypes, epilogue=None, extras=(), b_stacked=False, out_stacked=0,
            tm=1024, tn=1024, tk=4096, precision=None, comm=None):
    if mode == "nn":
        m, kdim = a.shape
        if b_stacked:
            s, _, nsh = b.shape
            n = s * nsh
            tn = _tile(nsh, tn, LANES)
        else:
            n = b.shape[1]
            tn = _tile(n, tn, LANES)
        tm, tk = _tile(m, tm, SUBLANES * 2), _tile(kdim, tk, LANES)
        a_spec = pl.BlockSpec((tm, tk), lambda i, j, k: (i, k))
        if b_stacked:
            npb = nsh // tn
            b_spec = pl.BlockSpec((None, tk, tn), lambda i, j, k: (j // npb, k, j % npb))
        else:
            b_spec = pl.BlockSpec((tk, tn), lambda i, j, k: (k, j))
        dims = _NN
    elif mode == "nt":
        m, kdim = a.shape
        if b_stacked:
            s, n, ksh = b.shape
            tk = _tile(ksh, tk, LANES)
            kpb = ksh // tk
            tn = _tile(n, tn, LANES)
            b_spec = pl.BlockSpec((None, tn, tk), lambda i, j, k: (k // kpb, j, k % kpb))
        else:
            n = b.shape[0]
            tk = _tile(kdim, tk, LANES)
            tn = _tile(n, tn, LANES)
            b_spec = pl.BlockSpec((tn, tk), lambda i, j, k: (j, k))
        tm = _tile(m, tm, SUBLANES * 2)
        a_spec = pl.BlockSpec((tm, tk), lambda i, j, k: (i, k))
        dims = _NT
    else:
        kdim, m = a.shape
        n = b.shape[1]
        tm = _tile(m, tm, LANES)
        tk = _tile(kdim, tk, SUBLANES * 2)
        if out_stacked:
            nsh = n // out_stacked
            tn = _tile(nsh, tn, LANES)
        else:
            tn = _tile(n, tn, LANES)
        a_spec = pl.BlockSpec((tk, tm), lambda i, j, k: (k, i))
        b_spec = pl.BlockSpec((tk, tn), lambda i, j, k: (k, j))
        dims = _TN
    nk = kdim // tk
    grid = (m // tm, n // tn, nk)
    if out_stacked:
        npo = (n // out_stacked) // tn
        o_spec = pl.BlockSpec((None, tm, tn), lambda i, j, k: (j // npo, i, j % npo))
        out_shape = [jax.ShapeDtypeStruct((out_stacked, m, n // out_stacked), dt) for dt in out_dtypes]
    else:
        o_spec = pl.BlockSpec((tm, tn), lambda i, j, k: (i, j))
        out_shape = [jax.ShapeDtypeStruct((m, n), dt) for dt in out_dtypes]
    x_spec = pl.BlockSpec((tm, tn), lambda i, j, k: (i, j))
    n_ex, n_out = len(extras), len(out_dtypes)

    def body(a_ref, b_ref, *rest):
        ex_refs, out_refs, acc_ref = rest[:n_ex], rest[n_ex:n_ex + n_out], rest[-1]
        k = pl.program_id(2)

        def finish(acc):
            outs = (acc,) if epilogue is None else epilogue(acc, *[r[...] for r in ex_refs])
            for r, o in zip(out_refs, outs):
                r[...] = o.astype(r.dtype)

        part = lax.dot_general(a_ref[...], b_ref[...], dims, precision=precision, preferred_element_type=F32)
        if nk == 1:
            finish(part)
        else:
            @pl.when(k == 0)
            def _():
                acc_ref[...] = part

            @pl.when(k > 0)
            def _():
                acc_ref[...] += part

            @pl.when(k == nk - 1)
            def _():
                finish(acc_ref[...])

    res, carried = _call(
        body, name, grid, [a_spec, b_spec] + [x_spec] * n_ex, [o_spec] * n_out, out_shape,
        [pltpu.VMEM((tm, tn) if nk > 1 else (SUBLANES, LANES), F32)], ("parallel", "parallel", "arbitrary"),
        (a, b, *extras), comm)
    main = res[0] if n_out == 1 else res
    return (main, carried) if comm else main


def _rowwise(body, name, rows, row_ins, vec_ins, row_outs, acc_outs, tr=128):
    tr = _tile(rows, tr, SUBLANES * 2)
    n_ri, n_vi, n_ro, n_ao = len(row_ins), len(vec_ins), len(row_outs), len(acc_outs)

    def kern(*refs):
        ri, vi = refs[:n_ri], refs[n_ri:n_ri + n_vi]
        ro = refs[n_ri + n_vi:n_ri + n_vi + n_ro]
        ao = refs[n_ri + n_vi + n_ro:]

        @pl.when(pl.program_id(0) == 0)
        def _():
            for r in ao:
                r[...] = jnp.zeros_like(r)

        body(ri, vi, ro, ao)

    in_specs = [pl.BlockSpec((tr, w), functools.partial(lambda i, cb: (i, cb), cb=cb)) for _, w, cb in row_ins]
    in_specs += [pl.BlockSpec(v.shape, lambda i: (0, 0)) for v in vec_ins]
    out_specs = [pl.BlockSpec((tr, w), lambda i: (i, 0)) for w, _ in row_outs]
    out_specs += [pl.BlockSpec((1, w), lambda i: (0, 0)) for w in acc_outs]
    out_shape = [jax.ShapeDtypeStruct((rows, w), dt) for w, dt in row_outs]
    out_shape += [jax.ShapeDtypeStruct((1, w), F32) for w in acc_outs]
    return _pcall(
        kern, name=name, grid=(rows // tr,), in_specs=in_specs, out_specs=out_specs, out_shape=out_shape,
        compiler_params=_params(("arbitrary",)),
    )(*[a for a, _, _ in row_ins], *vec_ins)


def _colsum(x):
    return jnp.sum(x, axis=0, keepdims=True)


def _rstd(x):
    return lax.rsqrt(jnp.mean(x * x, axis=-1, keepdims=True) + EPS)


def _norm_bwd(dxn, xn, r):
    return r * (dxn - xn * jnp.mean(dxn * xn, axis=-1, keepdims=True))


_SQRT_HALF = math.sqrt(0.5)
_INV_SQRT_2PI = 1.0 / math.sqrt(2.0 * math.pi)


def _gelu(y):
    return 0.5 * y * (1.0 + lax.erf(y * _SQRT_HALF))


def _gelu_grad(y):
    return 0.5 * (1.0 + lax.erf(y * _SQRT_HALF)) + y * jnp.exp(-0.5 * y * y) * _INV_SQRT_2PI


def _norm_mod(x, g, scale, shift):
    def body(ri, vi, ro, ao):
        xv = ri[0][...]
        h = xv * _rstd(xv) * vi[0][...] * (1.0 + vi[1][...]) + vi[2][...]
        ro[0][...] = h.astype(BF16)

    d = x.shape[1]
    return _rowwise(body, "norm_mod", x.shape[0], [(x, d, 0)], [g, scale, shift], [(d, BF16)], [])[0]


def _res_norm_mod(x, mo, gate, g, scale, shift):
    def body(ri, vi, ro, ao):
        x1 = ri[0][...] + vi[0][...] * ri[1][...]
        ro[0][...] = x1
        ro[1][...] = (x1 * _rstd(x1) * vi[1][...] * (1.0 + vi[2][...]) + vi[3][...]).astype(BF16)

    d = x.shape[1]
    return _rowwise(body, "res_norm_mod", x.shape[0], [(x, d, 0), (mo, d, 0)], [gate, g, scale, shift],
                    [(d, F32), (d, BF16)], [])


def _mix(attn, y0, gl, b_glu, ga, gs):
    da, ds = attn.shape[1], y0.shape[1]

    def body(ri, vi, ro, ao):
        at = ri[0][...]
        z = _gelu(ri[1][...])
        o = z * jax.nn.sigmoid(ri[2][...] + vi[0][...])
        ro[0][:, :da] = (at * _rstd(at) * vi[1][...]).astype(BF16)
        ro[0][:, da:] = (o * _rstd(o) * vi[2][...]).astype(BF16)

    return _rowwise(body, "mix", attn.shape[0], [(attn, da, 0), (y0, ds, 0), (gl, ds, 0)], [b_glu, ga, gs],
                    [(da + ds, BF16)], [])[0]


def _final(x1, ff, tgt, gate2, gf):
    d = x1.shape[1]

    def body(ri, vi, ro, ao):
        ffv = ri[1][...]
        x2 = ri[0][...] + vi[0][...] * ffv
        r = _rstd(x2)
        xn = x2 * r
        e = xn * vi[1][...] - ri[2][...]
        ao[0][...] += _colsum(e * e)
        dy = e * (1.0 / d)
        ao[1][...] += _colsum(dy * xn)
        dx2 = _norm_bwd(dy * vi[1][...], xn, r)
        ao[2][...] += _colsum(dx2 * ffv)
        ro[0][...] = dx2
        ro[1][...] = (dx2 * vi[0][...]).astype(BF16)

    return _rowwise(body, "final", x1.shape[0], [(x1, d, 0), (ff, d, 0), (tgt, d, 0)], [gate2, gf],
                    [(d, F32), (d, BF16)], [d, d, d])


def _bwd_norm2(x1, dh2, dx2, mo, g2, scale2, gate1):
    d = x1.shape[1]

    def body(ri, vi, ro, ao):
        xv, dh = ri[0][...], ri[1][...]
        r = _rstd(xv)
        xn = xv * r
        ao[0][...] += _colsum(dh)
        ao[1][...] += _colsum(dh * xn * vi[0][...])
        dn = dh * (1.0 + vi[1][...])
        ao[2][...] += _colsum(dn * xn)
        dx1 = ri[2][...] + _norm_bwd(dn * vi[0][...], xn, r)
        ao[3][...] += _colsum(dx1 * ri[3][...])
        ro[0][...] = dx1
        ro[1][...] = (dx1 * vi[2][...]).astype(BF16)

    return _rowwise(body, "bwd_norm2", x1.shape[0], [(x1, d, 0), (dh2, d, 0), (dx2, d, 0), (mo, d, 0)],
                    [g2, scale2, gate1], [(d, F32), (d, BF16)], [d, d, d, d])


def _bwd_mix(dmixed, attn, y0, gl, b_glu, ga, gs):
    da, ds = attn.shape[1], y0.shape[1]

    def body(ri, vi, ro, ao):
        dan, dsn = ri[0][:, :da], ri[0][:, da:]
        at = ri[1][...]
        ra = _rstd(at)
        an = at * ra
        ao[0][...] += _colsum(dan * an)
        ro[0][...] = _norm_bwd(dan * vi[1][...], an, ra)
        z = _gelu(ri[2][...])
        sg = jax.nn.sigmoid(ri[3][...] + vi[0][...])
        o = z * sg
        rs = _rstd(o)
        on = o * rs
        ao[1][...] += _colsum(dsn * on)
        do = _norm_bwd(dsn * vi[2][...], on, rs)
        ro[2][...] = do * sg
        dgl = do * z * sg * (1.0 - sg)
        ao[2][...] += _colsum(dgl)
        ro[1][...] = dgl.astype(BF16)

    return _rowwise(body, "bwd_mix", attn.shape[0],
                    [(dmixed, da + ds, 0), (attn, da, 0), (y0, ds, 0), (gl, ds, 0)], [b_glu, ga, gs],
                    [(da, F32), (ds, BF16), (ds, F32)], [da, ds, ds])


def _bwd_norm1(x, dh, dx1, g1, scale1):
    d = x.shape[1]

    def body(ri, vi, ro, ao):
        xv, dhv = ri[0][...], ri[1][...]
        r = _rstd(xv)
        xn = xv * r
        ao[0][...] += _colsum(dhv)
        ao[1][...] += _colsum(dhv * xn * vi[0][...])
        dn = dhv * (1.0 + vi[1][...])
        ao[2][...] += _colsum(dn * xn)
        ro[0][...] = ri[2][...] + _norm_bwd(dn * vi[0][...], xn, r)

    return _rowwise(body, "bwd_norm1", x.shape[0], [(x, d, 0), (dh, d, 0), (dx1, d, 0)], [g1, scale1],
                    [(d, F32)], [d, d, d])


def _rope_apply(x, cos, sin, sign):
    first = (lax.broadcasted_iota(jnp.int32, cos.shape, 1) % HEAD_DIM) < (HEAD_DIM // 2)
    outs = []
    for j in range(x.shape[1] // LANES):
        xc = x[:, j * LANES:(j + 1) * LANES]
        rot = jnp.where(first, -pltpu.roll(xc, LANES - HEAD_DIM // 2, 1), pltpu.roll(xc, HEAD_DIM // 2, 1))
        outs.append(xc * cos + sign * (rot * sin))
    return outs


def _rope_fwd(proj, cos, sin, d_attn, d_kv):
    scale = HEAD_DIM ** -0.5
    kcb, vcb = d_attn // d_kv, d_attn // d_kv + 1

    def body(ri, vi, ro, ao):
        c, s = ri[3][...], ri[4][...]
        for j, o in enumerate(_rope_apply(ri[0][...], c, s, 1.0)):
            ro[0][:, j * LANES:(j + 1) * LANES] = (o * scale).astype(BF16)
        for j, o in enumerate(_rope_apply(ri[1][...], c, s, 1.0)):
            ro[1][:, j * LANES:(j + 1) * LANES] = o.astype(BF16)
        ro[2][...] = ri[2][...].astype(BF16)

    return _rowwise(body, "rope_fwd", proj.shape[0],
                    [(proj, d_attn, 0), (proj, d_kv, kcb), (proj, d_kv, vcb), (cos, LANES, 0), (sin, LANES, 0)], [],
                    [(d_attn, BF16), (d_kv, BF16), (d_kv, BF16)], [])


def _rope_bwd(dqr, dkc, dkp, dvc, dvp, cos, sin):
    scale = HEAD_DIM ** -0.5
    d_attn, d_kv = dqr.shape[1], dkc.shape[1]

    def body(ri, vi, ro, ao):
        c, s = ri[5][...], ri[6][...]
        for j, o in enumerate(_rope_apply(ri[0][...], c, s, -1.0)):
            ro[0][:, j * LANES:(j + 1) * LANES] = (o * scale).astype(BF16)
        for j, o in enumerate(_rope_apply(ri[1][...] + ri[2][...], c, s, -1.0)):
            ro[1][:, j * LANES:(j + 1) * LANES] = o.astype(BF16)
        ro[2][...] = (ri[3][...] + ri[4][...]).astype(BF16)

    return _rowwise(body, "rope_bwd", dqr.shape[0],
                    [(dqr, d_attn, 0), (dkc, d_kv, 0), (dkp, d_kv, 0), (dvc, d_kv, 0), (dvp, d_kv, 0),
                     (cos, LANES, 0), (sin, LANES, 0)], [],
                    [(d_attn, BF16), (d_kv, BF16), (d_kv, BF16)], [])


def _attn_probs(q, k, sink_ref, g, n):
    rows = Q_PER_KV * WINDOW
    s = lax.dot_general(q, k, _NT, preferred_element_type=F32)
    qi = lax.broadcasted_iota(jnp.int32, (rows, 2 * WINDOW), 0) % WINDOW + WINDOW
    kj = lax.broadcasted_iota(jnp.int32, (rows, 2 * WINDOW), 1)
    rel = qi - kj
    mask = (rel >= 0) & (rel < WINDOW) & ((n > 0) | (kj >= WINDOW))
    s = jnp.where(mask, s, -1e30)
    sink = jnp.concatenate([jnp.full((WINDOW, 1), sink_ref[g * Q_PER_KV + j], F32) for j in range(Q_PER_KV)], axis=0)
    m = jnp.maximum(jnp.max(s, axis=-1, keepdims=True), sink)
    p = jnp.exp(s - m)
    es = jnp.exp(sink - m)
    l = jnp.sum(p, axis=-1, keepdims=True) + es
    return p, l, es


def _attn_specs(n_q, n_kv):
    qspec = pl.BlockSpec((n_q, WINDOW, HEAD_DIM), lambda n: (0, n, 0))
    cur = pl.BlockSpec((n_kv, WINDOW, HEAD_DIM), lambda n: (0, n, 0))
    prev = pl.BlockSpec((n_kv, WINDOW, HEAD_DIM), lambda n: (0, jnp.maximum(n - 1, 0), 0))
    return qspec, cur, prev


def _attn_fwd(q, k, v, sinks, comm=None):
    n_q, n_kv, t = q.shape[0], k.shape[0], k.shape[1]
    rows = Q_PER_KV * WINDOW

    def body(sink_ref, q_ref, kp_ref, kc_ref, vp_ref, vc_ref, o_ref):
        n = pl.program_id(0)
        for g in range(n_kv):
            hs = slice(g * Q_PER_KV, (g + 1) * Q_PER_KV)
            qv = q_ref[hs].reshape(rows, HEAD_DIM)
            kv = jnp.concatenate([kp_ref[g], kc_ref[g]], axis=0)
            vv = jnp.concatenate([vp_ref[g], vc_ref[g]], axis=0)
            p, l, _ = _attn_probs(qv, kv, sink_ref, g, n)
            o = jnp.dot(p.astype(BF16), vv, preferred_element_type=F32) / l
            o_ref[hs] = o.reshape(Q_PER_KV, WINDOW, HEAD_DIM)

    qspec, cur, prev = _attn_specs(n_q, n_kv)
    (out,), carried = _call(
        body, "attn_fwd", (t // WINDOW,),
        [pl.BlockSpec(memory_space=pltpu.SMEM), qspec, prev, cur, prev, cur], [qspec],
        [jax.ShapeDtypeStruct(q.shape, F32)], [], ("arbitrary",), (sinks, q, k, k, v, v), comm)
    return out, carried


def _attn_bwd(q, k, v, o, do, sinks, comm=None):
    n_q, n_kv, t = q.shape[0], k.shape[0], k.shape[1]
    rows = Q_PER_KV * WINDOW

    def body(sink_ref, q_ref, kp_ref, kc_ref, vp_ref, vc_ref, o_ref, do_ref,
             dq_ref, dkc_ref, dkp_ref, dvc_ref, dvp_ref, ds_ref):
        n = pl.program_id(0)

        @pl.when(n == 0)
        def _():
            ds_ref[...] = jnp.zeros_like(ds_ref)

        for g in range(n_kv):
            hs = slice(g * Q_PER_KV, (g + 1) * Q_PER_KV)
            qv = q_ref[hs].reshape(rows, HEAD_DIM)
            kv = jnp.concatenate([kp_ref[g], kc_ref[g]], axis=0)
            vv = jnp.concatenate([vp_ref[g], vc_ref[g]], axis=0)
            p, l, es = _attn_probs(qv, kv, sink_ref, g, n)
            inv_l = 1.0 / l
            pn = p * inv_l
            dov = do_ref[hs].reshape(rows, HEAD_DIM)
            delta = jnp.sum(dov * o_ref[hs].reshape(rows, HEAD_DIM), axis=-1, keepdims=True)
            dob = dov.astype(BF16)
            dv = lax.dot_general(pn.astype(BF16), dob, _TN, preferred_element_type=F32)
            dp = lax.dot_general(dob, vv, _NT, preferred_element_type=F32)
            dsb = (pn * (dp - delta)).astype(BF16)
            dq_ref[hs] = jnp.dot(dsb, kv, preferred_element_type=F32).reshape(Q_PER_KV, WINDOW, HEAD_DIM)
            dk = lax.dot_general(dsb, qv, _TN, preferred_element_type=F32)
            dkp_ref[g] = dk[:WINDOW]
            dkc_ref[g] = dk[WINDOW:]
            dvp_ref[g] = dv[:WINDOW]
            dvc_ref[g] = dv[WINDOW:]
            ds_ref[g] += -(es * inv_l) * delta

    qspec, cur, prev = _attn_specs(n_q, n_kv)
    sspec = pl.BlockSpec((n_kv, rows, 1), lambda n: (0, 0, 0))
    kshape = jax.ShapeDtypeStruct(k.shape, F32)
    return _call(
        body, "attn_bwd", (t // WINDOW,),
        [pl.BlockSpec(memory_space=pltpu.SMEM), qspec, prev, cur, prev, cur, qspec, qspec],
        [qspec, cur, cur, cur, cur, sspec],
        [jax.ShapeDtypeStruct(q.shape, F32), kshape, kshape, kshape, kshape,
         jax.ShapeDtypeStruct((n_kv, rows, 1), F32)],
        [], ("arbitrary",), (sinks, q, k, k, v, v, o, do), comm)


def _cmul(ar, ai, br, bi):
    return ar * br - ai * bi, ar * bi + ai * br


def _scan_consts(ar, ai, half, reverse):
    row = lax.broadcasted_iota(jnp.int32, (SUBLANES, half), 0)
    a2 = _cmul(ar, ai, ar, ai)
    a4 = _cmul(*a2, *a2)
    steps = [(1, ar, ai), (2, *a2), (4, *a4)]
    pr, pi = ar, ai
    pwr = jnp.zeros((SUBLANES, half), F32)
    pwi = jnp.zeros((SUBLANES, half), F32)
    for r in range(SUBLANES):
        sel = row == (SUBLANES - 1 - r if reverse else r)
        pwr = jnp.where(sel, pr, pwr)
        pwi = jnp.where(sel, pi, pwi)
        pr, pi = _cmul(pr, pi, ar, ai)
    return row, steps, pwr, pwi


def _scan8(xr, xi, row, steps, pwr, pwi, cr, ci, reverse):
    for d, er, ei in steps:
        if reverse:
            keep, shift = row < SUBLANES - d, SUBLANES - d
        else:
            keep, shift = row >= d, d
        sr = jnp.where(keep, pltpu.roll(xr, shift, 0), 0.0)
        si = jnp.where(keep, pltpu.roll(xi, shift, 0), 0.0)
        tr, ti = _cmul(er, ei, sr, si)
        xr, xi = xr + tr, xi + ti
    tr, ti = _cmul(pwr, pwi, cr, ci)
    return xr + tr, xi + ti


def _ssm_fwd(proj, ucb0, bexp, cexp, a_cat, d_skip, tt=512, comm=None):
    t = proj.shape[0]
    ngb, cw, two_l = bexp.shape
    half = two_l // 2
    tt = _tile(t, tt, SUBLANES * 2)
    nt = t // tt

    def body(u_ref, b_ref, c_ref, a_ref, d_ref, y_ref, z_ref, st_ref, carry_ref):
        @pl.when(pl.program_id(1) == 0)
        def _():
            carry_ref[...] = jnp.zeros_like(carry_ref)

        u = u_ref[...]
        st_ref[...] = jnp.dot(u.astype(BF16), b_ref[...], preferred_element_type=F32)
        ar, ai = a_ref[:, :half], a_ref[:, half:]
        row, steps, pwr, pwi = _scan_consts(ar, ai, half, False)

        def tile(i, carry):
            base = pl.multiple_of(i * SUBLANES, SUBLANES)
            xr, xi = _scan8(st_ref[pl.ds(base, SUBLANES), :half], st_ref[pl.ds(base, SUBLANES), half:],
                            row, steps, pwr, pwi, carry[0], carry[1], False)
            st_ref[pl.ds(base, SUBLANES), :half] = xr
            st_ref[pl.ds(base, SUBLANES), half:] = xi
            return xr[SUBLANES - 1:, :], xi[SUBLANES - 1:, :]

        cr, ci = lax.fori_loop(0, tt // SUBLANES, tile, (carry_ref[0:1, :half], carry_ref[0:1, half:]),
                               unroll=SCAN_UNROLL)
        carry_ref[0:1, :half] = cr
        carry_ref[0:1, half:] = ci
        y = jnp.dot(st_ref[...].astype(BF16), c_ref[...], preferred_element_type=F32) + d_ref[...] * u
        y_ref[...] = y
        z_ref[...] = _gelu(y).astype(BF16)

    d_ssm = ngb * cw
    return _call(
        body, "ssm_fwd", (ngb, nt),
        [pl.BlockSpec((tt, cw), lambda g, i: (i, ucb0 + g)),
         pl.BlockSpec((None, cw, two_l), lambda g, i: (g, 0, 0)),
         pl.BlockSpec((None, two_l, cw), lambda g, i: (g, 0, 0)),
         pl.BlockSpec((None, 1, two_l), lambda g, i: (g, 0, 0)),
         pl.BlockSpec((1, cw), lambda g, i: (0, g))],
        [pl.BlockSpec((tt, cw), lambda g, i: (i, g)),
         pl.BlockSpec((tt, cw), lambda g, i: (i, g)),
         pl.BlockSpec((tt, two_l), lambda g, i: (i, g))],
        [jax.ShapeDtypeStruct((t, d_ssm), F32), jax.ShapeDtypeStruct((t, d_ssm), BF16),
         jax.ShapeDtypeStruct((t, ngb * two_l), F32)],
        [pltpu.VMEM((SUBLANES, two_l), F32)], ("parallel", "arbitrary"),
        (proj, bexp, cexp, a_cat, d_skip), comm)


def _ssm_bwd(dy0, proj, ucb0, states, bexp, cexp, a_cat, d_skip, tt=512, comm=None):
    t = dy0.shape[0]
    ngb, cw, two_l = bexp.shape
    half = two_l // 2
    tt = _tile(t, tt, SUBLANES * 2)
    nt = t // tt

    def body(dy_ref, u_ref, st_ref, b_ref, c_ref, a_ref, d_ref,
             du_ref, db_ref, dc_ref, da_ref, dd_ref, lam_ref, carry_ref, acc_ref):
        step = pl.program_id(1)

        @pl.when(step == 0)
        def _():
            carry_ref[...] = jnp.zeros_like(carry_ref)
            acc_ref[...] = jnp.zeros_like(acc_ref)
            db_ref[...] = jnp.zeros_like(db_ref)
            dc_ref[...] = jnp.zeros_like(dc_ref)
            dd_ref[...] = jnp.zeros_like(dd_ref)

        dy, u = dy_ref[...], u_ref[...]
        dyb = dy.astype(BF16)
        lam_ref[...] = lax.dot_general(dyb, c_ref[...], _NT, preferred_element_type=F32)
        ar, ai = a_ref[:, :half], -a_ref[:, half:]
        row, steps, pwr, pwi = _scan_consts(ar, ai, half, True)
        last = row == SUBLANES - 1

        def tile(i, carry):
            cr, ci, accr, acci = carry
            base = pl.multiple_of((tt // SUBLANES - 1 - i) * SUBLANES, SUBLANES)
            xr, xi = _scan8(lam_ref[pl.ds(base, SUBLANES), :half], lam_ref[pl.ds(base, SUBLANES), half:],
                            row, steps, pwr, pwi, cr, ci, True)
            lam_ref[pl.ds(base, SUBLANES), :half] = xr
            lam_ref[pl.ds(base, SUBLANES), half:] = xi
            nr = jnp.where(last, cr, pltpu.roll(xr, SUBLANES - 1, 0))
            ni = jnp.where(last, ci, pltpu.roll(xi, SUBLANES - 1, 0))
            sr, si = st_ref[pl.ds(base, SUBLANES), :half], st_ref[pl.ds(base, SUBLANES), half:]
            return xr[0:1, :], xi[0:1, :], accr + sr * nr + si * ni, acci + sr * ni - si * nr

        cr, ci, accr, acci = lax.fori_loop(
            0, tt // SUBLANES, tile,
            (carry_ref[0:1, :half], carry_ref[0:1, half:], acc_ref[:, :half], acc_ref[:, half:]),
            unroll=SCAN_UNROLL)
        carry_ref[0:1, :half] = cr
        carry_ref[0:1, half:] = ci
        acc_ref[:, :half] = accr
        acc_ref[:, half:] = acci
        lamb = lam_ref[...].astype(BF16)
        du = lax.dot_general(lamb, b_ref[...], _NT, preferred_element_type=F32) + d_ref[...] * dy
        du_ref[...] = du.astype(BF16)
        db_ref[...] += lax.dot_general(u.astype(BF16), lamb, _TN, preferred_element_type=F32)
        dc_ref[...] += lax.dot_general(st_ref[...].astype(BF16), dyb, _TN, preferred_element_type=F32)
        dd_ref[...] += _colsum(dy * u)

        @pl.when(step == nt - 1)
        def _():
            da_ref[...] = _colsum(acc_ref[...])

    d_ssm = ngb * cw
    return _call(
        body, "ssm_bwd", (ngb, nt),
        [pl.BlockSpec((tt, cw), lambda g, i: (nt - 1 - i, g)),
         pl.BlockSpec((tt, cw), lambda g, i: (nt - 1 - i, ucb0 + g)),
         pl.BlockSpec((tt, two_l), lambda g, i: (nt - 1 - i, g)),
         pl.BlockSpec((None, cw, two_l), lambda g, i: (g, 0, 0)),
         pl.BlockSpec((None, two_l, cw), lambda g, i: (g, 0, 0)),
         pl.BlockSpec((None, 1, two_l), lambda g, i: (g, 0, 0)),
         pl.BlockSpec((1, cw), lambda g, i: (0, g))],
        [pl.BlockSpec((tt, cw), lambda g, i: (nt - 1 - i, g)),
         pl.BlockSpec((None, cw, two_l), lambda g, i: (g, 0, 0)),
         pl.BlockSpec((None, two_l, cw), lambda g, i: (g, 0, 0)),
         pl.BlockSpec((None, 1, two_l), lambda g, i: (g, 0, 0)),
         pl.BlockSpec((1, cw), lambda g, i: (0, g))],
        [jax.ShapeDtypeStruct((t, d_ssm), BF16),
         jax.ShapeDtypeStruct((ngb, cw, two_l), F32),
         jax.ShapeDtypeStruct((ngb, two_l, cw), F32),
         jax.ShapeDtypeStruct((ngb, 1, two_l), F32),
         jax.ShapeDtypeStruct((1, d_ssm), F32)],
        [pltpu.VMEM((tt, two_l), F32), pltpu.VMEM((SUBLANES, two_l), F32), pltpu.VMEM((SUBLANES, two_l), F32)],
        ("parallel", "arbitrary"), (dy0, proj, states, bexp, cexp, a_cat, d_skip), comm)


def _zoh(lr, li, ls):
    step = jnp.exp(ls)
    e = jnp.exp(lr * step)
    ar, ai = e * jnp.cos(li * step), e * jnp.sin(li * step)
    den = lr * lr + li * li
    cr = ((ar - 1.0) * lr + ai * li) / den
    ci = (ai * lr - (ar - 1.0) * li) / den
    return step, ar, ai, den, cr, ci


def _ssm_param_fwd(lr, li, ls, br, bi):
    def body(lr_ref, li_ref, ls_ref, br_ref, bi_ref, ar_ref, ai_ref, bbr_ref, bbi_ref):
        _, ar, ai, _, cr, ci = _zoh(lr_ref[...], li_ref[...], ls_ref[...])
        ar_ref[...] = ar
        ai_ref[...] = ai
        bbr, bbi = _cmul(cr, ci, br_ref[...], bi_ref[...])
        bbr_ref[...] = bbr
        bbi_ref[...] = bbi

    small, big = jax.ShapeDtypeStruct(lr.shape, F32), jax.ShapeDtypeStruct(br.shape, F32)
    return _pcall(body, name="ssm_param_fwd", out_shape=[small, small, big, big],
                  compiler_params=_params())(lr, li, ls, br, bi)


def _ssm_param_bwd(lr, li, ls, br, bi, gar, gai, gbr, gbi):
    def body(lr_ref, li_ref, ls_ref, br_ref, bi_ref, gar_ref, gai_ref, gbr_ref, gbi_ref,
             dlr_ref, dli_ref, dls_ref, dbr_ref, dbi_ref):
        lrv, liv = lr_ref[...], li_ref[...]
        step, ar, ai, den, cr, ci = _zoh(lrv, liv, ls_ref[...])
        brv, biv, gr, gi = br_ref[...], bi_ref[...], gbr_ref[...], gbi_ref[...]
        dbr_ref[...] = cr * gr + ci * gi
        dbi_ref[...] = cr * gi - ci * gr
        gcr = jnp.sum(brv * gr + biv * gi, axis=1, keepdims=True)
        gci = jnp.sum(brv * gi - biv * gr, axis=1, keepdims=True)
        gtr = gar_ref[...] + (lrv * gcr - liv * gci) / den
        gti = gai_ref[...] + (lrv * gci + liv * gcr) / den
        qr = (cr * lrv + ci * liv) / den
        qi = (ci * lrv - cr * liv) / den
        gzr = ar * gtr + ai * gti
        gzi = ar * gti - ai * gtr
        dlr_ref[...] = step * gzr - (qr * gcr + qi * gci)
        dli_ref[...] = step * gzi - (qr * gci - qi * gcr)
        gstep = jnp.sum(lrv * gzr + liv * gzi, axis=2, keepdims=True)
        dls_ref[...] = jnp.broadcast_to(step * gstep, step.shape)

    small, big = jax.ShapeDtypeStruct(lr.shape, F32), jax.ShapeDtypeStruct(br.shape, F32)
    return _pcall(body, name="ssm_param_bwd", out_shape=[small, small, small, big, big],
                  compiler_params=_params())(lr, li, ls, br, bi, gar, gai, gbr, gbi)


def _block_diag_in(bb):
    g, h, p = bb.shape
    nb, n = g // GROUPS_PER_BLOCK, GROUPS_PER_BLOCK
    b4 = bb.reshape(nb, n, h, p)
    rows = [jnp.pad(b4[:, k], ((0, 0), (0, 0), (k * p, (n - 1 - k) * p))) for k in range(n)]
    return jnp.concatenate(rows, axis=1)


def _block_diag_take(e, h, p):
    nb, n = e.shape[0], GROUPS_PER_BLOCK
    d = jnp.stack([e[:, k * h:(k + 1) * h, k * p:(k + 1) * p] for k in range(n)], axis=1)
    return d.reshape(nb * n, h, p)


def _ada_fwd(c_all, w_sh, b_sh, tn=512):
    bsz, d = c_all.shape
    nsh = w_sh.shape[1]
    tn = _tile(nsh, tn, LANES)

    def body(c_ref, w_ref, b_ref, mod_ref, act_ref):
        act = c_ref[...] * jax.nn.sigmoid(c_ref[...])
        act_ref[...] = act
        mod_ref[...] = jnp.dot(act.astype(BF16), w_ref[...].astype(BF16), preferred_element_type=F32) + b_ref[...]

    return _pcall(
        body, name="ada_fwd", grid=(nsh // tn,),
        in_specs=[pl.BlockSpec((bsz, d), lambda j: (0, 0)), pl.BlockSpec((d, tn), lambda j: (0, j)),
                  pl.BlockSpec((1, tn), lambda j: (0, j))],
        out_specs=[pl.BlockSpec((bsz, tn), lambda j: (0, j)), pl.BlockSpec((bsz, d), lambda j: (0, 0))],
        out_shape=[jax.ShapeDtypeStruct((bsz, nsh), F32), jax.ShapeDtypeStruct((bsz, d), F32)],
        compiler_params=_params(("arbitrary",)),
    )(c_all, w_sh, b_sh)


def _adamw(w, g, m, v, name, comm=None):
    r, c = w.shape
    tr = _tile(r, max(SUBLANES, (256 * 1024) // c // SUBLANES * SUBLANES), SUBLANES)
    c1, c2 = 1.0 / (1.0 - ADAM_B1 ** ADAM_STEP), 1.0 / (1.0 - ADAM_B2 ** ADAM_STEP)

    def body(w_ref, g_ref, m_ref, v_ref, d_ref, nm_ref, nv_ref):
        gv = g_ref[...]
        nm = ADAM_B1 * m_ref[...] + (1.0 - ADAM_B1) * gv
        nv = ADAM_B2 * v_ref[...] + (1.0 - ADAM_B2) * (gv * gv)
        nm_ref[...] = nm
        nv_ref[...] = nv
        d_ref[...] = -ADAM_LR * ((nm * c1) / (jnp.sqrt(nv * c2) + ADAM_EPS) + ADAM_WD * w_ref[...])

    spec = pl.BlockSpec((tr, c), lambda i: (i, 0))
    shp = jax.ShapeDtypeStruct((r, c), F32)
    res, carried = _call(body, name, (r // tr,), [spec] * 4, [spec] * 3, [shp] * 3, [], ("parallel",),
                         (w, g, m, v), comm)
    return (res, carried) if comm else res


def _sum_leading(arr, out_dtype, name):
    n, r, c = arr.shape
    tr = _tile(r, max(SUBLANES * 2, (512 * 1024) // (c * n) // (SUBLANES * 2) * (SUBLANES * 2)), SUBLANES * 2)

    def body(x_ref, o_ref):
        acc = x_ref[0].astype(F32)
        for k in range(1, n):
            acc = acc + x_ref[k].astype(F32)
        o_ref[...] = acc.astype(out_dtype)

    return _pcall(body, name=name, grid=(r // tr,),
                  in_specs=[pl.BlockSpec((n, tr, c), lambda i: (0, i, 0))],
                  out_specs=pl.BlockSpec((tr, c), lambda i: (i, 0)),
                  out_shape=jax.ShapeDtypeStruct((r, c), out_dtype),
                  compiler_params=_params(("parallel",)))(arr)


def _place():
    x, y, c = lax.axis_index("x"), lax.axis_index("y"), lax.axis_index("c")
    chips = [(1 - x, y), (x, 1 - y), (1 - x, 1 - y)]
    return x, y, c, chips


def _allgather8(v, name):
    m, n = v.shape

    def body(x_ref, out_ref, send_sems, recv_sems, local_sem):
        x, y, c, chips = _place()
        me, sibling = (x, y, c), (x, y, 1 - c)

        def slot(px, py, pc):
            return out_ref.at[4 * px + 2 * py + pc]

        def copy(k, block, to, src=None):
            return pltpu.make_async_remote_copy(
                src_ref=slot(*block) if src is None else src, dst_ref=slot(*block),
                send_sem=send_sems.at[k], recv_sem=recv_sems.at[k], device_id=to, device_id_type=MESH)

        mine = pltpu.make_async_copy(x_ref, slot(*me), local_sem)
        mine.start()
        first = [copy(0, me, sibling, src=x_ref)]
        first += [copy(1 + j, me, (*chip, c), src=x_ref) for j, chip in enumerate(chips)]
        for cp in first:
            cp.start()
        passed = [copy(4 + j, (*chip, c), sibling) for j, chip in enumerate(chips)]
        for j, chip in enumerate(chips):
            copy(1 + j, (*chip, c), me).wait_recv()
            passed[j].start()
        copy(0, sibling, me).wait_recv()
        for j, chip in enumerate(chips):
            copy(4 + j, (*chip, 1 - c), me).wait_recv()
        for cp in first + passed:
            cp.wait_send()
        mine.wait()

    return _pcall(
        body, name=name, out_shape=jax.ShapeDtypeStruct((N_DEV, m, n), F32),
        in_specs=[pl.BlockSpec(memory_space=pltpu.VMEM)], out_specs=pl.BlockSpec(memory_space=pltpu.VMEM),
        scratch_shapes=[pltpu.SemaphoreType.DMA((7,)), pltpu.SemaphoreType.DMA((7,)), pltpu.SemaphoreType.DMA],
        compiler_params=_params(),
    )(v)


def _scalars(*vals):
    return jnp.stack([jnp.asarray(v, jnp.int32) for v in vals])


def _cast_place(w, chip, name):
    r, cdim = w.shape
    tr = _tile(r, max(SUBLANES * 2, (512 * 1024) // cdim // (SUBLANES * 2) * (SUBLANES * 2)), SUBLANES * 2)

    def body(s_ref, w_ref, o_ref):
        o_ref[...] = w_ref[...].astype(BF16)

    grid_spec = pltpu.PrefetchScalarGridSpec(
        num_scalar_prefetch=1, grid=(r // tr,),
        in_specs=[pl.BlockSpec((tr, cdim), lambda i, s: (i, 0))],
        out_specs=pl.BlockSpec((None, tr, cdim), lambda i, s: (s[0], i, 0)))
    return _pcall(body, name=name, grid_spec=grid_spec, out_shape=jax.ShapeDtypeStruct((N_CHIPS, r, cdim), BF16),
                  compiler_params=_params(("parallel",)))(_scalars(chip), w)


def _gather_weights(bufs):
    nw = len(bufs)

    def body(*refs):
        outs = refs[nw:2 * nw]
        send_sems, recv_sems = refs[2 * nw:]
        x, y, c, chips = _place()
        me, sibling = (x, y, c), (x, y, 1 - c)

        def copy(w, k, chip, hc, to):
            h = outs[w].shape[1] // 2
            ref = outs[w].at[2 * chip[0] + chip[1], pl.ds(pl.multiple_of(hc * h, SUBLANES * 2), h)]
            return pltpu.make_async_remote_copy(
                src_ref=ref, dst_ref=ref, send_sem=send_sems.at[w, k], recv_sem=recv_sems.at[w, k],
                device_id=to, device_id_type=MESH)

        sent = []
        for w in range(nw):
            for k, chip in enumerate(chips):
                sent.append(copy(w, k, (x, y), c, (*chip, c)))
                sent[-1].start()
        for w in range(nw):
            for k, chip in enumerate(chips):
                copy(w, k, chip, c, me).wait_recv()
                sent.append(copy(w, 3 + k, chip, c, sibling))
                sent[-1].start()
        for w in range(nw):
            for k, chip in enumerate(chips):
                copy(w, 3 + k, chip, 1 - c, me).wait_recv()
        for cp in sent:
            cp.wait_send()

    any_spec = pl.BlockSpec(memory_space=pl.ANY)
    return _pcall(
        body, name="gather_weights",
        out_shape=[jax.ShapeDtypeStruct(b.shape, b.dtype) for b in bufs],
        in_specs=[any_spec] * nw, out_specs=[any_spec] * nw,
        input_output_aliases={w: w for w in range(nw)},
        scratch_shapes=[pltpu.SemaphoreType.DMA((nw, 6)), pltpu.SemaphoreType.DMA((nw, 6))],
        compiler_params=_params(),
    )(*bufs)


class _Comm:
    def __init__(self, ins, outs, aliases, n_sems, start, finish):
        self.ins, self.outs, self.aliases, self.n_sems = ins, outs, aliases, n_sems
        self.start, self.finish = start, finish


def _comm_gather_ici(bufs, spans=None):
    spans = spans or [(0, 1, 1)] * len(bufs)

    def copies(outs, send_sems, recv_sems, incoming):
        x, y, c, chips = _place()
        res = []
        for w, (out, (lo, count, n)) in enumerate(zip(outs, spans)):
            unit = out.shape[1] // 2 // n
            rows = unit * count
            for k, chip in enumerate(chips):
                blk = chip if incoming else (x, y)
                first = pl.multiple_of((c * n + lo) * unit, SUBLANES * 2)
                ref = out.at[2 * blk[0] + blk[1], pl.ds(first, rows)]
                res.append(pltpu.make_async_remote_copy(
                    src_ref=ref, dst_ref=ref, send_sem=send_sems.at[w * len(chips) + k],
                    recv_sem=recv_sems.at[w * len(chips) + k], device_id=(*chip, c), device_id_type=MESH))
        return res

    def start(ci, co, send_sems, recv_sems):
        for cp in copies(co, send_sems, recv_sems, False):
            cp.start()

    def finish(ci, co, send_sems, recv_sems):
        for cp in copies(co, send_sems, recv_sems, True):
            cp.wait_recv()
        for cp in copies(co, send_sems, recv_sems, False):
            cp.wait_send()

    return _Comm(list(bufs), [jax.ShapeDtypeStruct(b.shape, b.dtype) for b in bufs],
                 {w: w for w in range(len(bufs))}, (N_CHIPS - 1) * len(bufs), start, finish)


def _forward_halves(bufs, name):
    nw = len(bufs)

    def body(*refs):
        outs = refs[nw:2 * nw]
        send_sems, recv_sems = refs[2 * nw:]
        x, y, c, chips = _place()

        def copy(w, k, hc):
            chip = chips[k]
            h = outs[w].shape[1] // 2
            ref = outs[w].at[2 * chip[0] + chip[1], pl.ds(pl.multiple_of(hc * h, SUBLANES * 2), h)]
            return pltpu.make_async_remote_copy(
                src_ref=ref, dst_ref=ref, send_sem=send_sems.at[w, k], recv_sem=recv_sems.at[w, k],
                device_id=(x, y, 1 - c), device_id_type=MESH)

        pairs = [(w, k) for w in range(nw) for k in range(len(chips))]
        for w, k in pairs:
            copy(w, k, c).start()
        for w, k in pairs:
            copy(w, k, 1 - c).wait_recv()
        for w, k in pairs:
            copy(w, k, c).wait_send()

    any_spec = pl.BlockSpec(memory_space=pl.ANY)
    return _pcall(
        body, name=name,
        out_shape=[jax.ShapeDtypeStruct(b.shape, b.dtype) for b in bufs],
        in_specs=[any_spec] * nw, out_specs=[any_spec] * nw,
        input_output_aliases={w: w for w in range(nw)},
        scratch_shapes=[pltpu.SemaphoreType.DMA((nw, N_CHIPS - 1)), pltpu.SemaphoreType.DMA((nw, N_CHIPS - 1))],
        compiler_params=_params(),
    )(*bufs)


def _comm_chip_exchange(psums, spans=None, recvs=None):
    nw = len(psums)
    spans = spans or [(0, 1, 1)] * nw
    recvs = recvs or [None] * nw
    old = [w for w in range(nw) if recvs[w] is not None]
    new = [w for w in range(nw) if recvs[w] is None]

    def copies(ci, co, send_sems, recv_sems):
        x, y, c, chips = _place()
        dsts = {w: co[i] for i, w in enumerate(old + new)}
        res = []
        for w, (lo, count, n) in enumerate(spans):
            unit = psums[w].shape[1] // n
            rows = pl.ds(lo * unit, count * unit)
            for k, chip in enumerate(chips):
                res.append(pltpu.make_async_remote_copy(
                    src_ref=ci[w].at[2 * chip[0] + chip[1], rows], dst_ref=dsts[w].at[k, rows],
                    send_sem=send_sems.at[w * len(chips) + k], recv_sem=recv_sems.at[w * len(chips) + k],
                    device_id=(*chip, c), device_id_type=MESH))
        return res

    def start(ci, co, send_sems, recv_sems):
        for cp in copies(ci, co, send_sems, recv_sems):
            cp.start()

    def finish(ci, co, send_sems, recv_sems):
        cps = copies(ci, co, send_sems, recv_sems)
        for cp in cps:
            cp.wait_recv()
        for cp in cps:
            cp.wait_send()

    shape = lambda p: jax.ShapeDtypeStruct((N_CHIPS - 1,) + p.shape[1:], p.dtype)
    comm = _Comm(list(psums) + [recvs[w] for w in old], [shape(psums[w]) for w in old + new],
                 {nw + i: i for i in range(len(old))}, (N_CHIPS - 1) * nw, start, finish)
    comm.order = old + new
    return comm


class _Offset:
    class _At:
        def __init__(self, sems, base):
            self.sems, self.base = sems, base

        def __getitem__(self, k):
            return self.sems.at[self.base + k]

    def __init__(self, sems, base):
        self.at = _Offset._At(sems, base)


def _join(a, b):
    na_i, na_o = len(a.ins), len(a.outs)

    def both(fa, fb):
        def run(ci, co, send_sems, recv_sems):
            fa(ci[:na_i], co[:na_o], send_sems, recv_sems)
            fb(ci[na_i:], co[na_o:], _Offset(send_sems, a.n_sems), _Offset(recv_sems, a.n_sems))
        return run

    aliases = dict(a.aliases)
    aliases.update({na_i + i: na_o + o for i, o in b.aliases.items()})
    return _Comm(list(a.ins) + list(b.ins), list(a.outs) + list(b.outs), aliases, a.n_sems + b.n_sems,
                 both(a.start, b.start), both(a.finish, b.finish))


def _comm_pair_exchange(views):
    def copies(ci, co, send_sems, recv_sems):
        x, y, c, _ = _place()
        return [pltpu.make_async_remote_copy(
            src_ref=ci[w].at[k, 1 - c], dst_ref=co[w].at[k], send_sem=send_sems.at[w * N_CHIPS + k],
            recv_sem=recv_sems.at[w * N_CHIPS + k], device_id=(x, y, 1 - c), device_id_type=MESH)
            for w in range(len(views)) for k in range(N_CHIPS)]

    def start(ci, co, send_sems, recv_sems):
        for cp in copies(ci, co, send_sems, recv_sems):
            cp.start()

    def finish(ci, co, send_sems, recv_sems):
        cps = copies(ci, co, send_sems, recv_sems)
        for cp in cps:
            cp.wait_recv()
        for cp in cps:
            cp.wait_send()

    outs = [jax.ShapeDtypeStruct((N_CHIPS,) + v.shape[2:], v.dtype) for v in views]
    return _Comm(list(views), outs, {}, N_CHIPS * len(views), start, finish)


def _in_order(comm, carried):
    return [carried[comm.order.index(w)] for w in range(len(comm.order))]


def _pair_exchange(grads, name):
    nw = len(grads)

    def body(*refs):
        ins, outs = refs[:nw], refs[nw:2 * nw]
        send_sems, recv_sems = refs[2 * nw:]
        x, y, c, _ = _place()
        sibling = (x, y, 1 - c)

        def copy(w, k):
            return pltpu.make_async_remote_copy(
                src_ref=ins[w].at[k, 1 - c], dst_ref=outs[w].at[k],
                send_sem=send_sems.at[w, k], recv_sem=recv_sems.at[w, k], device_id=sibling, device_id_type=MESH)

        copies = [copy(w, k) for w in range(nw) for k in range(N_CHIPS)]
        for cp in copies:
            cp.start()
        for cp in copies:
            cp.wait_recv()
        for cp in copies:
            cp.wait_send()

    any_spec = pl.BlockSpec(memory_space=pl.ANY)
    return _pcall(
        body, name=name,
        out_shape=[jax.ShapeDtypeStruct((N_CHIPS,) + g.shape[2:], g.dtype) for g in grads],
        in_specs=[any_spec] * nw, out_specs=[any_spec] * nw,
        scratch_shapes=[pltpu.SemaphoreType.DMA((nw, N_CHIPS)), pltpu.SemaphoreType.DMA((nw, N_CHIPS))],
        compiler_params=_params(),
    )(*grads)


def _pair_sum(view, recv, core, name):
    n, _, h, cdim = view.shape
    th = _tile(h, max(SUBLANES * 2, (512 * 1024) // cdim // (SUBLANES * 2) * (SUBLANES * 2)), SUBLANES * 2)

    def body(s_ref, a_ref, b_ref, o_ref):
        o_ref[...] = (a_ref[...].astype(F32) + b_ref[...].astype(F32)).astype(BF16)

    grid_spec = pltpu.PrefetchScalarGridSpec(
        num_scalar_prefetch=1, grid=(n, h // th),
        in_specs=[pl.BlockSpec((None, None, th, cdim), lambda k, i, s: (k, s[0], i, 0)),
                  pl.BlockSpec((None, th, cdim), lambda k, i, s: (k, i, 0))],
        out_specs=pl.BlockSpec((None, th, cdim), lambda k, i, s: (k, i, 0)))
    return _pcall(body, name=name, grid_spec=grid_spec, out_shape=jax.ShapeDtypeStruct((n, h, cdim), BF16),
                  compiler_params=_params(("parallel", "parallel")))(_scalars(core), view, recv)


def _chip_sum(psums, recv, chip, core, name):
    _, h, cdim = psums.shape
    th = _tile(h, max(SUBLANES * 2, (256 * 1024) // cdim // (SUBLANES * 2) * (SUBLANES * 2)), SUBLANES * 2)

    def body(chip_ref, core_ref, a_ref, b_ref, o_ref):
        acc = a_ref[...].astype(F32)
        for k in range(N_CHIPS - 1):
            acc = acc + b_ref[k].astype(F32)
        o_ref[...] = acc

    grid_spec = pltpu.PrefetchScalarGridSpec(
        num_scalar_prefetch=2, grid=(h // th,),
        in_specs=[pl.BlockSpec((None, th, cdim), lambda i, s, t: (s[0], i, 0)),
                  pl.BlockSpec((N_CHIPS - 1, th, cdim), lambda i, s, t: (0, i, 0))],
        out_specs=pl.BlockSpec((None, th, cdim), lambda i, s, t: (t[0], i, 0)))
    return _pcall(body, name=name, grid_spec=grid_spec, out_shape=jax.ShapeDtypeStruct((2, h, cdim), F32),
                  compiler_params=_params(("parallel",)))(_scalars(chip), _scalars(core), psums, recv)


def _share_halves(bufs):
    nw = len(bufs)

    def body(*refs):
        outs = refs[nw:2 * nw]
        send_sems, recv_sems = refs[2 * nw:]
        x, y, c, _ = _place()
        copies = [pltpu.make_async_remote_copy(
            src_ref=outs[w].at[c], dst_ref=outs[w].at[c], send_sem=send_sems.at[w], recv_sem=recv_sems.at[w],
            device_id=(x, y, 1 - c), device_id_type=MESH) for w in range(nw)]
        for cp in copies:
            cp.start()
        for w in range(nw):
            pltpu.make_async_remote_copy(
                src_ref=outs[w].at[1 - c], dst_ref=outs[w].at[1 - c], send_sem=send_sems.at[w],
                recv_sem=recv_sems.at[w], device_id=(x, y, 1 - c), device_id_type=MESH).wait_recv()
        for cp in copies:
            cp.wait_send()

    any_spec = pl.BlockSpec(memory_space=pl.ANY)
    return _pcall(
        body, name="grad_share_halves",
        out_shape=[jax.ShapeDtypeStruct(b.shape, b.dtype) for b in bufs],
        in_specs=[any_spec] * nw, out_specs=[any_spec] * nw,
        input_output_aliases={w: w for w in range(nw)},
        scratch_shapes=[pltpu.SemaphoreType.DMA((nw,)), pltpu.SemaphoreType.DMA((nw,))],
        compiler_params=_params(),
    )(*bufs)


def _pack(arrays):
    flat = jnp.concatenate([a.reshape(-1).astype(F32) for a in arrays])
    unit = 2 * SUBLANES * PACK_COLS
    pad = (-flat.shape[0]) % unit
    return jnp.pad(flat, (0, pad)).reshape(-1, PACK_COLS)


def _unpack(buf, shapes):
    flat, out, off = buf.reshape(-1), [], 0
    for s in shapes:
        n = math.prod(s)
        out.append(flat[off:off + n].reshape(s))
        off += n
    return out


def kernel(x, c, w_ada, b_ada, norm1_g, w_in, sinks, ssm_lam_re, ssm_lam_im, ssm_log_step, ssm_b_re, ssm_b_im, ssm_c_re, ssm_c_im, ssm_d, w_glu, b_glu, attn_out_g, ssm_out_g, w_out, norm2_g, w_ff1, w_ff2, final_g, loss_target, m_w_ada, m_b_ada, m_norm1_g, m_w_in, m_sinks, m_ssm_lam_re, m_ssm_lam_im, m_ssm_log_step, m_ssm_b_re, m_ssm_b_im, m_ssm_c_re, m_ssm_c_im, m_ssm_d, m_w_glu, m_b_glu, m_attn_out_g, m_ssm_out_g, m_w_out, m_norm2_g, m_w_ff1, m_w_ff2, m_final_g, v_w_ada, v_b_ada, v_norm1_g, v_w_in, v_sinks, v_ssm_lam_re, v_ssm_lam_im, v_ssm_log_step, v_ssm_b_re, v_ssm_b_im, v_ssm_c_re, v_ssm_c_im, v_ssm_d, v_w_glu, v_b_glu, v_attn_out_g, v_ssm_out_g, v_w_out, v_norm2_g, v_w_ff1, v_w_ff2, v_final_g):
    t, d = x.shape[1], x.shape[2]
    d_attn, d_ssm = attn_out_g.shape[1], ssm_d.shape[1]
    d_in = w_in.shape[2] * N_CHIPS
    d_kv = (d_in - d_attn - d_ssm) // 2
    n_q, n_kv = d_attn // HEAD_DIM, d_kv // HEAD_DIM
    n_grp = ssm_lam_re.shape[1]
    assert n_q == n_kv * Q_PER_KV and t % WINDOW == 0 and d_ssm == n_grp * SSM_GROUP
    assert d_kv % LANES == 0 and d_attn % d_kv == 0 and n_grp % GROUPS_PER_BLOCK == 0
    cw = GROUPS_PER_BLOCK * SSM_GROUP
    ucb0 = (d_attn + 2 * d_kv) // cw
    assert (d_attn + 2 * d_kv) % cw == 0
    xi, yi, ci = lax.axis_index("x"), lax.axis_index("y"), lax.axis_index("c")
    chip = 2 * xi + yi
    dev = 2 * chip + ci
    xs, tgt = x[0], loss_target[0]
    vec = lambda a: a.reshape(1, -1)

    n_ada = w_ada.shape[2]
    c_all = _allgather8(c.reshape(SUBLANES, d // SUBLANES), "gather_c").reshape(N_DEV, d)
    b_sh = lax.dynamic_slice_in_dim(b_ada, chip * n_ada, n_ada, axis=1)
    mod_sh, c_act = _ada_fwd(c_all, w_ada[0], b_sh)
    mod_all = _allgather8(mod_sh, "gather_mod")
    mod_me = lax.dynamic_index_in_dim(mod_all[0::2], dev, axis=1, keepdims=False)
    mod_me = mod_me.reshape(N_CHIPS * n_ada // d, 1, d)
    shift1, scale1, gate1, shift2, scale2, gate2 = [mod_me[i] for i in range(N_MOD)]

    own = {n: _cast_place(w[0], chip, "cast_" + n)
           for n, w in (("w_in", w_in), ("w_glu", w_glu), ("w_out", w_out), ("w_ff1", w_ff1), ("w_ff2", w_ff2))}
    win_s, = _gather_weights([own["w_in"]])
    half_view = lambda g, w: g.reshape(N_CHIPS, 2, w.shape[0] // 2, w.shape[1])

    g3 = lambda a: a.reshape(n_grp, 1, STATE)
    lr3, li3 = g3(ssm_lam_re[0]), g3(ssm_lam_im[0])
    ls3 = jnp.broadcast_to(ssm_log_step[0].reshape(n_grp, 1, 1), (n_grp, 1, STATE))
    b_re3, b_im3 = ssm_b_re[0].transpose(0, 2, 1), ssm_b_im[0].transpose(0, 2, 1)
    a_re, a_im, bb_re, bb_im = _ssm_param_fwd(lr3, li3, ls3, b_re3, b_im3)
    ngb = n_grp // GROUPS_PER_BLOCK
    a_cat = jnp.concatenate([a_re.reshape(ngb, 1, -1), a_im.reshape(ngb, 1, -1)], axis=-1)
    bexp = jnp.concatenate([_block_diag_in(bb_re.astype(BF16)), _block_diag_in(bb_im.astype(BF16))], axis=-1)
    cexp = jnp.concatenate([_block_diag_in(ssm_c_re[0].astype(BF16)), _block_diag_in(-ssm_c_im[0].astype(BF16))],
                           axis=-1).transpose(0, 2, 1)

    half = HEAD_DIM // 2
    inv_freq = ROPE_THETA ** (-jnp.arange(half, dtype=F32) / half)
    ang = jnp.arange(t, dtype=F32)[:, None] * inv_freq[None, :]
    cos = jnp.tile(jnp.cos(ang), (1, LANES // half))
    sin = jnp.tile(jnp.sin(ang), (1, LANES // half))

    h = _norm_mod(xs, norm1_g, scale1, shift1)
    proj, (wglu_ici, wout_ici) = _matmul(h, win_s, "nn", "mm_in", [F32], b_stacked=True, tn=win_s.shape[2],
                                         comm=_comm_gather_ici([own["w_glu"], own["w_out"]]))
    qr, kr, vb = _rope_fwd(proj, cos, sin, d_attn, d_kv)
    heads = lambda a, n: a.reshape(t, n, HEAD_DIM).transpose(1, 0, 2)
    unheads = lambda a: a.transpose(1, 0, 2).reshape(t, -1)
    qh, kh, vh = heads(qr, n_q), heads(kr, n_kv), heads(vb, n_kv)
    oh, (wff1_ici,) = _attn_fwd(qh, kh, vh, sinks[0], comm=_comm_gather_ici([own["w_ff1"]], [(0, 2, 4)]))
    attn = unheads(oh)
    (y0, z, states), (wff1_ici, wff2_ici) = _ssm_fwd(
        proj, ucb0, bexp, cexp, a_cat, ssm_d,
        comm=_comm_gather_ici([wff1_ici, own["w_ff2"]], [(2, 2, 4), (0, 1, 8)]))
    wglu_s, wout_s, wff1_s = _forward_halves([wglu_ici, wout_ici, wff1_ici], "forward_halves_mix")
    wglu = wglu_s.reshape(d_ssm, d_ssm)
    wout = wout_s.reshape(d_attn + d_ssm, d)
    gl = _matmul(z, wglu, "nn", "mm_glu", [F32])
    mixed = _mix(attn, y0, gl, b_glu, attn_out_g, ssm_out_g)
    mo, (wff2_ici,) = _matmul(mixed, wout, "nn", "mm_out", [F32], comm=_comm_gather_ici([wff2_ici], [(1, 1, 8)]))
    x1, h2 = _res_norm_mod(xs, mo, gate1, norm2_g, scale2, shift2)

    def relu2(acc):
        r = jnp.maximum(acc, 0.0)
        return acc, r * r

    (a_act, rr), (wff2_ici,) = _matmul(h2, wff1_s, "nn", "mm_ff1", [BF16, BF16], epilogue=relu2, b_stacked=True,
                                       comm=_comm_gather_ici([wff2_ici], [(2, 6, 8)]))
    wff2 = _forward_halves([wff2_ici], "forward_halves_ff2")[0].reshape(-1, d)
    ff = _matmul(rr, wff2, "nn", "mm_ff2", [F32])
    dx2, dff, loss_cols, dgf, dgate2 = _final(x1, ff, tgt, gate2, vec(final_g))
    loss = lax.psum(0.5 * jnp.sum(loss_cols) / d, ("x", "y", "c"))

    d_relu2 = lambda acc, av: (acc * 2.0 * jnp.maximum(av.astype(F32), 0.0),)
    v_ff2 = half_view(_matmul(rr, dff, "tn", "mm_gw_ff2", [BF16]), w_ff2[0])
    da, (x_ff2,) = _matmul(dff, wff2, "nt", "mm_dff2", [BF16], epilogue=d_relu2, extras=(a_act,),
                           comm=_comm_pair_exchange([v_ff2]))
    p_ff2 = _pair_sum(v_ff2, x_ff2, ci, "pair_sum_ff2")
    gw_ff1, (r_ff2,) = _matmul(h2, da, "tn", "mm_gw_ff1", [BF16], out_stacked=N_CHIPS,
                               comm=_comm_chip_exchange([p_ff2], [(0, 3, 4)]))
    v_ff1 = half_view(gw_ff1, w_ff1[0])
    dh2, (r_ff2, x_ff1) = _matmul(
        da, wff1_s, "nt", "mm_dff1", [F32], b_stacked=True,
        comm=_join(_comm_chip_exchange([p_ff2], [(3, 1, 4)], [r_ff2]), _comm_pair_exchange([v_ff1])))
    p_ff1 = _pair_sum(v_ff1, x_ff1, ci, "pair_sum_ff1")
    dx1, dmo, dshift2, dscale2, dg2, dgate1 = _bwd_norm2(x1, dh2, dx2, mo, norm2_g, scale2, gate1)
    dmixed = _matmul(dmo, wout, "nt", "mm_dout", [F32])
    gw_out = _matmul(mixed, dmo, "tn", "mm_gw_out", [BF16])
    dattn, dgl, dzp, dga, dgs, dbglu = _bwd_mix(dmixed, attn, y0, gl, b_glu, attn_out_g, ssm_out_g)
    d_gelu = lambda acc, dz, yv: ((acc + dz) * _gelu_grad(yv),)
    dy0 = _matmul(dgl, wglu, "nt", "mm_dglu", [F32], epilogue=d_gelu, extras=(dzp, y0))
    gw_glu = _matmul(z, dgl, "tn", "mm_gw_glu", [BF16])
    v_mix = [half_view(gw_glu, w_glu[0]), half_view(gw_out, w_out[0])]
    p_mix = [_pair_sum(v, p, ci, "pair_sum_" + n)
             for n, v, p in zip(("glu", "out"), v_mix, _pair_exchange(v_mix, "pair_exchange_mix"))]
    (du, dbexp, dcexp, da_bar, dd), (r_ff1,) = _ssm_bwd(
        dy0, proj, ucb0, states, bexp, cexp, a_cat, ssm_d,
        comm=_comm_chip_exchange([p_ff1], [(0, 2, 4)]))
    doh = heads(dattn, n_q)
    (dqh, dkc, dkp, dvc, dvp, dsink), (r_ff1,) = _attn_bwd(
        qh, kh, vh, oh, doh, sinks[0], comm=_comm_chip_exchange([p_ff1], [(2, 2, 4)], [r_ff1]))
    up = lambda a: jnp.concatenate([unheads(a)[WINDOW:], jnp.zeros((WINDOW, d_kv), F32)], axis=0)
    dq, dk, dv = _rope_bwd(unheads(dqh), unheads(dkc), up(dkp), unheads(dvc), up(dvp), cos, sin)
    dproj = jnp.concatenate([dq, dk, dv, du], axis=1)
    gw_in, r_mix = _matmul(h, dproj, "tn", "mm_gw_in", [BF16], out_stacked=N_CHIPS, tn=win_s.shape[2],
                           comm=_comm_chip_exchange(p_mix))
    v_in = half_view(gw_in, w_in[0])
    p_in = _pair_sum(v_in, _pair_exchange([v_in], "pair_exchange_in")[0], ci, "pair_sum_in")
    dh, (r_in,) = _matmul(dproj, win_s, "nt", "mm_din", [F32], b_stacked=True, tk=win_s.shape[2],
                          comm=_comm_chip_exchange([p_in]))
    grad_x, dshift1, dscale1, dg1 = _bwd_norm1(xs, dh, dx1, norm1_g, scale1)

    half_l = GROUPS_PER_BLOCK * STATE
    ga_re = da_bar[:, 0, :half_l].reshape(n_grp, 1, STATE)
    ga_im = da_bar[:, 0, half_l:].reshape(n_grp, 1, STATE)
    gbb_re = _block_diag_take(dbexp[:, :, :half_l], SSM_GROUP, STATE)
    gbb_im = _block_diag_take(dbexp[:, :, half_l:], SSM_GROUP, STATE)
    dcexp_t = dcexp.transpose(0, 2, 1)
    gc_re = _block_diag_take(dcexp_t[:, :, :half_l], SSM_GROUP, STATE)
    gc_im = -_block_diag_take(dcexp_t[:, :, half_l:], SSM_GROUP, STATE)
    dmod = jnp.concatenate([dshift1, dscale1, dgate1, dshift2, dscale2, dgate2], axis=1)
    dsinks = dsink.reshape(n_q, WINDOW).sum(axis=1)
    pieces = [dmod, dg1, dsinks, ga_re, ga_im, gbb_re, gbb_im, gc_re, gc_im, dd, dbglu, dga, dgs, dg2, dgf]
    gathered = _allgather8(_pack(pieces), "gather_small")
    summed = _sum_leading(gathered, F32, "sum_small")
    (g_b_ada, g_norm1, g_sinks, ga_re, ga_im, gbb_re, gbb_im, g_c_re, g_c_im, g_d, g_b_glu, g_attn_g, g_ssm_g,
     g_norm2, g_final) = _unpack(summed, [p.shape for p in pieces])
    g_lr, g_li, g_ls, g_b_re3, g_b_im3 = _ssm_param_bwd(lr3, li3, ls3, b_re3, b_im3, ga_re, ga_im, gbb_re, gbb_im)
    small_grads = [
        g_b_ada, g_norm1, g_sinks.reshape(1, -1), g_lr.reshape(1, n_grp, STATE), g_li.reshape(1, n_grp, STATE),
        g_ls[:, 0, 0].reshape(1, n_grp), g_b_re3.transpose(0, 2, 1)[None], g_b_im3.transpose(0, 2, 1)[None],
        g_c_re[None], g_c_im[None], g_d, g_b_glu, g_attn_g, g_ssm_g, g_norm2, g_final.reshape(-1)]
    small_w = [b_ada, norm1_g, sinks, ssm_lam_re, ssm_lam_im, ssm_log_step, ssm_b_re, ssm_b_im, ssm_c_re,
               ssm_c_im, ssm_d, b_glu, attn_out_g, ssm_out_g, norm2_g, final_g]
    small_m = [m_b_ada, m_norm1_g, m_sinks, m_ssm_lam_re, m_ssm_lam_im, m_ssm_log_step, m_ssm_b_re, m_ssm_b_im,
               m_ssm_c_re, m_ssm_c_im, m_ssm_d, m_b_glu, m_attn_out_g, m_ssm_out_g, m_norm2_g, m_final_g]
    small_v = [v_b_ada, v_norm1_g, v_sinks, v_ssm_lam_re, v_ssm_lam_im, v_ssm_log_step, v_ssm_b_re, v_ssm_b_im,
               v_ssm_c_re, v_ssm_c_im, v_ssm_d, v_b_glu, v_attn_out_g, v_ssm_out_g, v_norm2_g, v_final_g]
    small_grads = [g.reshape(w.shape) for g, w in zip(small_grads, small_w)]
    s_delta, s_m, s_v = _adamw(_pack(small_w), _pack(small_grads), _pack(small_m), _pack(small_v), "adamw_small")
    shapes = [w.shape for w in small_w]
    s_delta, s_m, s_v = _unpack(s_delta, shapes), _unpack(s_m, shapes), _unpack(s_v, shapes)

    dmod_rows = gathered.reshape(N_DEV, -1)[:, :dmod.shape[1]]
    dmod_sh = lax.dynamic_slice_in_dim(dmod_rows, chip * n_ada, n_ada, axis=1)
    g_w_ada = _matmul(c_act, dmod_sh, "tn", "mm_gw_ada", [F32], tk=N_DEV, precision=lax.Precision.HIGHEST)

    upd_ada = _adamw(w_ada[0], g_w_ada, m_w_ada[0], v_w_ada[0], "adamw_w_ada")
    big_w = [w_in[0], w_glu[0], w_out[0], w_ff1[0], w_ff2[0]]
    psums = [p_in, *p_mix, p_ff1, p_ff2]
    recvd = [r_in, *r_mix, r_ff1, r_ff2]
    halves = [_chip_sum(p, r, chip, ci, f"chip_sum_{i}") for i, (p, r) in enumerate(zip(psums, recvd))]
    big_grads = [s.reshape(w.shape) for s, w in zip(_share_halves(halves), big_w)]

    big_names = ["w_in", "w_glu", "w_out", "w_ff1", "w_ff2"]
    big_m = [m_w_in[0], m_w_glu[0], m_w_out[0], m_w_ff1[0], m_w_ff2[0]]
    big_v = [v_w_in[0], v_w_glu[0], v_w_out[0], v_w_ff1[0], v_w_ff2[0]]
    big_upd = {n: _adamw(w, g, m, v, "adamw_" + n) for n, w, g, m, v in zip(big_names, big_w, big_grads, big_m, big_v)}
    big_upd["w_ada"] = upd_ada
    big_grad = dict(zip(big_names, big_grads), w_ada=g_w_ada)
    big_names = ["w_ada"] + big_names

    order = ["w_ada", "b_ada", "norm1_g", "w_in", "sinks", "ssm_lam_re", "ssm_lam_im", "ssm_log_step", "ssm_b_re",
             "ssm_b_im", "ssm_c_re", "ssm_c_im", "ssm_d", "w_glu", "b_glu", "attn_out_g", "ssm_out_g", "w_out",
             "norm2_g", "w_ff1", "w_ff2", "final_g"]
    small_names = [n for n in order if n not in big_names]
    grads, deltas, new_m, new_v = {}, {}, {}, {}
    for i, n in enumerate(small_names):
        grads[n], deltas[n], new_m[n], new_v[n] = small_grads[i], s_delta[i], s_m[i], s_v[i]
    for n in big_names:
        grads[n] = big_grad[n][None]
        deltas[n], new_m[n], new_v[n] = [a[None] for a in big_upd[n]]
    return (loss, grad_x[None], *[grads[n] for n in order], *[deltas[n] for n in order],
            *[new_m[n] for n in order], *[new_v[n] for n in order])
```

```python
import functools
import math

import jax
import jax.numpy as jnp
from jax import lax
from jax.experimental import pallas as pl
from jax.experimental.pallas import tpu as pltpu

F32 = jnp.float32
BF16 = jnp.bfloat16
MESH = pl.DeviceIdType.MESH

EPS = 1e-6
HEAD_DIM = 64
Q_PER_KV = 8
WINDOW = 128
ROPE_THETA = 10000.0
SSM_GROUP = 16
STATE = 64
GROUPS_PER_BLOCK = 16
SCAN_UNROLL = 2
N_MOD = 6
N_CHIPS = 4
N_DEV = 8
ADAM_LR = 0.001
ADAM_B1 = 0.9
ADAM_B2 = 0.999
ADAM_EPS = 1e-08
ADAM_WD = 0.01
ADAM_STEP = 10
LANES = 128
SUBLANES = 8
VMEM_LIMIT = 56 * 1024 * 1024
PACK_COLS = 512


def _pcall(body, **kw):
    return pl.pallas_call(body, **kw)


def _params(sem=None):
    return pltpu.CompilerParams(dimension_semantics=sem, vmem_limit_bytes=VMEM_LIMIT)


def _call(body, name, grid, in_specs, out_specs, out_shape, scratch, sem, operands, comm=None):
    if comm is None:
        res = _pcall(body, name=name, grid=grid, in_specs=in_specs, out_specs=out_specs, out_shape=out_shape,
                     scratch_shapes=scratch, compiler_params=_params(sem))(*operands)
        return tuple(res), ()
    n_in, n_out, n_ci, n_co = len(in_specs), len(out_specs), len(comm.ins), len(comm.outs)

    def carrying(*refs):
        ins, ci = refs[:n_in], refs[n_in:n_in + n_ci]
        outs = refs[n_in + n_ci:n_in + n_ci + n_out]
        co = refs[n_in + n_ci + n_out:n_in + n_ci + n_out + n_co]
        rest, send_sems, recv_sems = refs[n_in + n_ci + n_out + n_co:-2], refs[-2], refs[-1]
        ids = [pl.program_id(a) for a in range(len(grid))]
        first = functools.reduce(lambda p, q: p & q, [i == 0 for i in ids])
        last = functools.reduce(lambda p, q: p & q, [i == g - 1 for i, g in zip(ids, grid)])

        @pl.when(first)
        def _():
            comm.start(ci, co, send_sems, recv_sems)

        body(*ins, *outs, *rest)

        @pl.when(last)
        def _():
            comm.finish(ci, co, send_sems, recv_sems)

    any_spec = pl.BlockSpec(memory_space=pl.ANY)
    res = _pcall(
        carrying, name=name, grid=grid, in_specs=list(in_specs) + [any_spec] * n_ci,
        out_specs=list(out_specs) + [any_spec] * n_co, out_shape=list(out_shape) + list(comm.outs),
        input_output_aliases={n_in + ci: n_out + co for ci, co in comm.aliases.items()},
        scratch_shapes=list(scratch) + [pltpu.SemaphoreType.DMA((comm.n_sems,)), pltpu.SemaphoreType.DMA((comm.n_sems,))],
        compiler_params=_params(("arbitrary",) * len(grid)),
    )(*operands, *comm.ins)
    return tuple(res[:n_out]), tuple(res[n_out:])


def _tile(n, want, unit):
    if n <= want:
        return n
    t = (want // unit) * unit
    while t > unit and n % t:
        t -= unit
    assert n % t == 0, (n, want, unit)
    return t


_NN = (((1,), (0,)), ((), ()))
_NT = (((1,), (1,)), ((), ()))
_TN = (((0,), (0,)), ((), ()))


def _matmul(a, b, mode, name, out_dtypes, epilogue=None, extras=(), b_stacked=False, out_stacked=0,
            tm=1024, tn=1024, tk=4096, precision=None, comm=None):
    if mode == "nn":
        m, kdim = a.shape
        if b_stacked:
            s, _, nsh = b.shape
            n = s * nsh
            tn = _tile(nsh, tn, LANES)
        else:
            n = b.shape[1]
            tn = _tile(n, tn, LANES)
        tm, tk = _tile(m, tm, SUBLANES * 2), _tile(kdim, tk, LANES)
        a_spec = pl.BlockSpec((tm, tk), lambda i, j, k: (i, k))
        if b_stacked:
            npb = nsh // tn
            b_spec = pl.BlockSpec((None, tk, tn), lambda i, j, k: (j // npb, k, j % npb))
        else:
            b_spec = pl.BlockSpec((tk, tn), lambda i, j, k: (k, j))
        dims = _NN
    elif mode == "nt":
        m, kdim = a.shape
        if b_stacked:
            s, n, ksh = b.shape
            tk = _tile(ksh, tk, LANES)
            kpb = ksh // tk
            tn = _tile(n, tn, LANES)
            b_spec = pl.BlockSpec((None, tn, tk), lambda i, j, k: (k // kpb, j, k % kpb))
        else:
            n = b.shape[0]
            tk = _tile(kdim, tk, LANES)
            tn = _tile(n, tn, LANES)
            b_spec = pl.BlockSpec((tn, tk), lambda i, j, k: (j, k))
        tm = _tile(m, tm, SUBLANES * 2)
        a_spec = pl.BlockSpec((tm, tk), lambda i, j, k: (i, k))
        dims = _NT
    else:
        kdim, m = a.shape
        n = b.shape[1]
        tm = _tile(m, tm, LANES)
        tk = _tile(kdim, tk, SUBLANES * 2)
        if out_stacked:
            nsh = n // out_stacked
            tn = _tile(nsh, tn, LANES)
        else:
            tn = _tile(n, tn, LANES)
        a_spec = pl.BlockSpec((tk, tm), lambda i, j, k: (k, i))
        b_spec = pl.BlockSpec((tk, tn), lambda i, j, k: (k, j))
        dims = _TN
    nk = kdim // tk
    grid = (m // tm, n // tn, nk)
    if out_stacked:
        npo = (n // out_stacked) // tn
        o_spec = pl.BlockSpec((None, tm, tn), lambda i, j, k: (j // npo, i, j % npo))
        out_shape = [jax.ShapeDtypeStruct((out_stacked, m, n // out_stacked), dt) for dt in out_dtypes]
    else:
        o_spec = pl.BlockSpec((tm, tn), lambda i, j, k: (i, j))
        out_shape = [jax.ShapeDtypeStruct((m, n), dt) for dt in out_dtypes]
    x_spec = pl.BlockSpec((tm, tn), lambda i, j, k: (i, j))
    n_ex, n_out = len(extras), len(out_dtypes)

    def body(a_ref, b_ref, *rest):
        ex_refs, out_refs, acc_ref = rest[:n_ex], rest[n_ex:n_ex + n_out], rest[-1]
        k = pl.program_id(2)

        def finish(acc):
            outs = (acc,) if epilogue is None else epilogue(acc, *[r[...] for r in ex_refs])
            for r, o in zip(out_refs, outs):
                r[...] = o.astype(r.dtype)

        part = lax.dot_general(a_ref[...], b_ref[...], dims, precision=precision, preferred_element_type=F32)
        if nk == 1:
            finish(part)
        else:
            @pl.when(k == 0)
            def _():
                acc_ref[...] = part

            @pl.when(k > 0)
            def _():
                acc_ref[...] += part

            @pl.when(k == nk - 1)
            def _():
                finish(acc_ref[...])

    res, carried = _call(
        body, name, grid, [a_spec, b_spec] + [x_spec] * n_ex, [o_spec] * n_out, out_shape,
        [pltpu.VMEM((tm, tn) if nk > 1 else (SUBLANES, LANES), F32)], ("parallel", "parallel", "arbitrary"),
        (a, b, *extras), comm)
    main = res[0] if n_out == 1 else res
    return (main, carried) if comm else main


def _rowwise(body, name, rows, row_ins, vec_ins, row_outs, acc_outs, tr=128):
    tr = _tile(rows, tr, SUBLANES * 2)
    n_ri, n_vi, n_ro, n_ao = len(row_ins), len(vec_ins), len(row_outs), len(acc_outs)

    def kern(*refs):
        ri, vi = refs[:n_ri], refs[n_ri:n_ri + n_vi]
        ro = refs[n_ri + n_vi:n_ri + n_vi + n_ro]
        ao = refs[n_ri + n_vi + n_ro:]

        @pl.when(pl.program_id(0) == 0)
        def _():
            for r in ao:
                r[...] = jnp.zeros_like(r)

        body(ri, vi, ro, ao)

    in_specs = [pl.BlockSpec((tr, w), functools.partial(lambda i, cb: (i, cb), cb=cb)) for _, w, cb in row_ins]
    in_specs += [pl.BlockSpec(v.shape, lambda i: (0, 0)) for v in vec_ins]
    out_specs = [pl.BlockSpec((tr, w), lambda i: (i, 0)) for w, _ in row_outs]
    out_specs += [pl.BlockSpec((1, w), lambda i: (0, 0)) for w in acc_outs]
    out_shape = [jax.ShapeDtypeStruct((rows, w), dt) for w, dt in row_outs]
    out_shape += [jax.ShapeDtypeStruct((1, w), F32) for w in acc_outs]
    return _pcall(
        kern, name=name, grid=(rows // tr,), in_specs=in_specs, out_specs=out_specs, out_shape=out_shape,
        compiler_params=_params(("arbitrary",)),
    )(*[a for a, _, _ in row_ins], *vec_ins)


def _colsum(x):
    return jnp.sum(x, axis=0, keepdims=True)


def _rstd(x):
    return lax.rsqrt(jnp.mean(x * x, axis=-1, keepdims=True) + EPS)


def _norm_bwd(dxn, xn, r):
    return r * (dxn - xn * jnp.mean(dxn * xn, axis=-1, keepdims=True))


_SQRT_HALF = math.sqrt(0.5)
_INV_SQRT_2PI = 1.0 / math.sqrt(2.0 * math.pi)


def _gelu(y):
    return 0.5 * y * (1.0 + lax.erf(y * _SQRT_HALF))


def _gelu_grad(y):
    return 0.5 * (1.0 + lax.erf(y * _SQRT_HALF)) + y * jnp.exp(-0.5 * y * y) * _INV_SQRT_2PI


def _norm_mod(x, g, scale, shift):
    def body(ri, vi, ro, ao):
        xv = ri[0][...]
        h = xv * _rstd(xv) * vi[0][...] * (1.0 + vi[1][...]) + vi[2][...]
        ro[0][...] = h.astype(BF16)

    d = x.shape[1]
    return _rowwise(body, "norm_mod", x.shape[0], [(x, d, 0)], [g, scale, shift], [(d, BF16)], [])[0]


def _res_norm_mod(x, mo, gate, g, scale, shift):
    def body(ri, vi, ro, ao):
        x1 = ri[0][...] + vi[0][...] * ri[1][...]
        ro[0][...] = x1
        ro[1][...] = (x1 * _rstd(x1) * vi[1][...] * (1.0 + vi[2][...]) + vi[3][...]).astype(BF16)

    d = x.shape[1]
    return _rowwise(body, "res_norm_mod", x.shape[0], [(x, d, 0), (mo, d, 0)], [gate, g, scale, shift],
                    [(d, F32), (d, BF16)], [])


def _mix(attn, y0, gl, b_glu, ga, gs):
    da, ds = attn.shape[1], y0.shape[1]

    def body(ri, vi, ro, ao):
        at = ri[0][...]
        z = _gelu(ri[1][...])
        o = z * jax.nn.sigmoid(ri[2][...] + vi[0][...])
        ro[0][:, :da] = (at * _rstd(at) * vi[1][...]).astype(BF16)
        ro[0][:, da:] = (o * _rstd(o) * vi[2][...]).astype(BF16)

    return _rowwise(body, "mix", attn.shape[0], [(attn, da, 0), (y0, ds, 0), (gl, ds, 0)], [b_glu, ga, gs],
                    [(da + ds, BF16)], [])[0]


def _final(x1, ff, tgt, gate2, gf):
    d = x1.shape[1]

    def body(ri, vi, ro, ao):
        ffv = ri[1][...]
        x2 = ri[0][...] + vi[0][...] * ffv
        r = _rstd(x2)
        xn = x2 * r
        e = xn * vi[1][...] - ri[2][...]
        ao[0][...] += _colsum(e * e)
        dy = e * (1.0 / d)
        ao[1][...] += _colsum(dy * xn)
        dx2 = _norm_bwd(dy * vi[1][...], xn, r)
        ao[2][...] += _colsum(dx2 * ffv)
        ro[0][...] = dx2
        ro[1][...] = (dx2 * vi[0][...]).astype(BF16)

    return _rowwise(body, "final", x1.shape[0], [(x1, d, 0), (ff, d, 0), (tgt, d, 0)], [gate2, gf],
                    [(d, F32), (d, BF16)], [d, d, d])


def _bwd_norm2(x1, dh2, dx2, mo, g2, scale2, gate1):
    d = x1.shape[1]

    def body(ri, vi, ro, ao):
        xv, dh = ri[0][...], ri[1][...]
        r = _rstd(xv)
        xn = xv * r
        ao[0][...] += _colsum(dh)
        ao[1][...] += _colsum(dh * xn * vi[0][...])
        dn = dh * (1.0 + vi[1][...])
        ao[2][...] += _colsum(dn * xn)
        dx1 = ri[2][...] + _norm_bwd(dn * vi[0][...], xn, r)
        ao[3][...] += _colsum(dx1 * ri[3][...])
        ro[0][...] = dx1
        ro[1][...] = (dx1 * vi[2][...]).astype(BF16)

    return _rowwise(body, "bwd_norm2", x1.shape[0], [(x1, d, 0), (dh2, d, 0), (dx2, d, 0), (mo, d, 0)],
                    [g2, scale2, gate1], [(d, F32), (d, BF16)], [d, d, d, d])


def _bwd_mix(dmixed, attn, y0, gl, b_glu, ga, gs):
    da, ds = attn.shape[1], y0.shape[1]

    def body(ri, vi, ro, ao):
        dan, dsn = ri[0][:, :da], ri[0][:, da:]
        at = ri[1][...]
        ra = _rstd(at)
        an = at * ra
        ao[0][...] += _colsum(dan * an)
        ro[0][...] = _norm_bwd(dan * vi[1][...], an, ra)
        z = _gelu(ri[2][...])
        sg = jax.nn.sigmoid(ri[3][...] + vi[0][...])
        o = z * sg
        rs = _rstd(o)
        on = o * rs
        ao[1][...] += _colsum(dsn * on)
        do = _norm_bwd(dsn * vi[2][...], on, rs)
        ro[2][...] = do * sg
        dgl = do * z * sg * (1.0 - sg)
        ao[2][...] += _colsum(dgl)
        ro[1][...] = dgl.astype(BF16)

    return _rowwise(body, "bwd_mix", attn.shape[0],
                    [(dmixed, da + ds, 0), (attn, da, 0), (y0, ds, 0), (gl, ds, 0)], [b_glu, ga, gs],
                    [(da, F32), (ds, BF16), (ds, F32)], [da, ds, ds])


def _bwd_norm1(x, dh, dx1, g1, scale1):
    d = x.shape[1]

    def body(ri, vi, ro, ao):
        xv, dhv = ri[0][...], ri[1][...]
        r = _rstd(xv)
        xn = xv * r
        ao[0][...] += _colsum(dhv)
        ao[1][...] += _colsum(dhv * xn * vi[0][...])
        dn = dhv * (1.0 + vi[1][...])
        ao[2][...] += _colsum(dn * xn)
        ro[0][...] = ri[2][...] + _norm_bwd(dn * vi[0][...], xn, r)

    return _rowwise(body, "bwd_norm1", x.shape[0], [(x, d, 0), (dh, d, 0), (dx1, d, 0)], [g1, scale1],
                    [(d, F32)], [d, d, d])


def _rope_apply(x, cos, sin, sign):
    first = (lax.broadcasted_iota(jnp.int32, cos.shape, 1) % HEAD_DIM) < (HEAD_DIM // 2)
    outs = []
    for j in range(x.shape[1] // LANES):
        xc = x[:, j * LANES:(j + 1) * LANES]
        rot = jnp.where(first, -pltpu.roll(xc, LANES - HEAD_DIM // 2, 1), pltpu.roll(xc, HEAD_DIM // 2, 1))
        outs.append(xc * cos + sign * (rot * sin))
    return outs


def _rope_fwd(proj, cos, sin, d_attn, d_kv):
    scale = HEAD_DIM ** -0.5
    kcb, vcb = d_attn // d_kv, d_attn // d_kv + 1

    def body(ri, vi, ro, ao):
        c, s = ri[3][...], ri[4][...]
        for j, o in enumerate(_rope_apply(ri[0][...], c, s, 1.0)):
            ro[0][:, j * LANES:(j + 1) * LANES] = (o * scale).astype(BF16)
        for j, o in enumerate(_rope_apply(ri[1][...], c, s, 1.0)):
            ro[1][:, j * LANES:(j + 1) * LANES] = o.astype(BF16)
        ro[2][...] = ri[2][...].astype(BF16)

    return _rowwise(body, "rope_fwd", proj.shape[0],
                    [(proj, d_attn, 0), (proj, d_kv, kcb), (proj, d_kv, vcb), (cos, LANES, 0), (sin, LANES, 0)], [],
                    [(d_attn, BF16), (d_kv, BF16), (d_kv, BF16)], [])


def _rope_bwd(dqr, dkc, dkp, dvc, dvp, cos, sin):
    scale = HEAD_DIM ** -0.5
    d_attn, d_kv = dqr.shape[1], dkc.shape[1]

    def body(ri, vi, ro, ao):
        c, s = ri[5][...], ri[6][...]
        for j, o in enumerate(_rope_apply(ri[0][...], c, s, -1.0)):
            ro[0][:, j * LANES:(j + 1) * LANES] = (o * scale).astype(BF16)
        for j, o in enumerate(_rope_apply(ri[1][...] + ri[2][...], c, s, -1.0)):
            ro[1][:, j * LANES:(j + 1) * LANES] = o.astype(BF16)
        ro[2][...] = (ri[3][...] + ri[4][...]).astype(BF16)

    return _rowwise(body, "rope_bwd", dqr.shape[0],
                    [(dqr, d_attn, 0), (dkc, d_kv, 0), (dkp, d_kv, 0), (dvc, d_kv, 0), (dvp, d_kv, 0),
                     (cos, LANES, 0), (sin, LANES, 0)], [],
                    [(d_attn, BF16), (d_kv, BF16), (d_kv, BF16)], [])


def _attn_probs(q, k, sink_ref, g, n):
    rows = Q_PER_KV * WINDOW
    s = lax.dot_general(q, k, _NT, preferred_element_type=F32)
    qi = lax.broadcasted_iota(jnp.int32, (rows, 2 * WINDOW), 0) % WINDOW + WINDOW
    kj = lax.broadcasted_iota(jnp.int32, (rows, 2 * WINDOW), 1)
    rel = qi - kj
    mask = (rel >= 0) & (rel < WINDOW) & ((n > 0) | (kj >= WINDOW))
    s = jnp.where(mask, s, -1e30)
    sink = jnp.concatenate([jnp.full((WINDOW, 1), sink_ref[g * Q_PER_KV + j], F32) for j in range(Q_PER_KV)], axis=0)
    m = jnp.maximum(jnp.max(s, axis=-1, keepdims=True), sink)
    p = jnp.exp(s - m)
    es = jnp.exp(sink - m)
    l = jnp.sum(p, axis=-1, keepdims=True) + es
    return p, l, es


def _attn_specs(n_q, n_kv):
    qspec = pl.BlockSpec((n_q, WINDOW, HEAD_DIM), lambda n: (0, n, 0))
    cur = pl.BlockSpec((n_kv, WINDOW, HEAD_DIM), lambda n: (0, n, 0))
    prev = pl.BlockSpec((n_kv, WINDOW, HEAD_DIM), lambda n: (0, jnp.maximum(n - 1, 0), 0))
    return qspec, cur, prev


def _attn_fwd(q, k, v, sinks, comm=None):
    n_q, n_kv, t = q.shape[0], k.shape[0], k.shape[1]
    rows = Q_PER_KV * WINDOW

    def body(sink_ref, q_ref, kp_ref, kc_ref, vp_ref, vc_ref, o_ref):
        n = pl.program_id(0)
        for g in range(n_kv):
            hs = slice(g * Q_PER_KV, (g + 1) * Q_PER_KV)
            qv = q_ref[hs].reshape(rows, HEAD_DIM)
            kv = jnp.concatenate([kp_ref[g], kc_ref[g]], axis=0)
            vv = jnp.concatenate([vp_ref[g], vc_ref[g]], axis=0)
            p, l, _ = _attn_probs(qv, kv, sink_ref, g, n)
            o = jnp.dot(p.astype(BF16), vv, preferred_element_type=F32) / l
            o_ref[hs] = o.reshape(Q_PER_KV, WINDOW, HEAD_DIM)

    qspec, cur, prev = _attn_specs(n_q, n_kv)
    (out,), carried = _call(
        body, "attn_fwd", (t // WINDOW,),
        [pl.BlockSpec(memory_space=pltpu.SMEM), qspec, prev, cur, prev, cur], [qspec],
        [jax.ShapeDtypeStruct(q.shape, F32)], [], ("arbitrary",), (sinks, q, k, k, v, v), comm)
    return out, carried


def _attn_bwd(q, k, v, o, do, sinks, comm=None):
    n_q, n_kv, t = q.shape[0], k.shape[0], k.shape[1]
    rows = Q_PER_KV * WINDOW

    def body(sink_ref, q_ref, kp_ref, kc_ref, vp_ref, vc_ref, o_ref, do_ref,
             dq_ref, dkc_ref, dkp_ref, dvc_ref, dvp_ref, ds_ref):
        n = pl.program_id(0)

        @pl.when(n == 0)
        def _():
            ds_ref[...] = jnp.zeros_like(ds_ref)

        for g in range(n_kv):
            hs = slice(g * Q_PER_KV, (g + 1) * Q_PER_KV)
            qv = q_ref[hs].reshape(rows, HEAD_DIM)
            kv = jnp.concatenate([kp_ref[g], kc_ref[g]], axis=0)
            vv = jnp.concatenate([vp_ref[g], vc_ref[g]], axis=0)
            p, l, es = _attn_probs(qv, kv, sink_ref, g, n)
            inv_l = 1.0 / l
            pn = p * inv_l
            dov = do_ref[hs].reshape(rows, HEAD_DIM)
            delta = jnp.sum(dov * o_ref[hs].reshape(rows, HEAD_DIM), axis=-1, keepdims=True)
            dob = dov.astype(BF16)
            dv = lax.dot_general(pn.astype(BF16), dob, _TN, preferred_element_type=F32)
            dp = lax.dot_general(dob, vv, _NT, preferred_element_type=F32)
            dsb = (pn * (dp - delta)).astype(BF16)
            dq_ref[hs] = jnp.dot(dsb, kv, preferred_element_type=F32).reshape(Q_PER_KV, WINDOW, HEAD_DIM)
            dk = lax.dot_general(dsb, qv, _TN, preferred_element_type=F32)
            dkp_ref[g] = dk[:WINDOW]
            dkc_ref[g] = dk[WINDOW:]
            dvp_ref[g] = dv[:WINDOW]
            dvc_ref[g] = dv[WINDOW:]
            ds_ref[g] += -(es * inv_l) * delta

    qspec, cur, prev = _attn_specs(n_q, n_kv)
    sspec = pl.BlockSpec((n_kv, rows, 1), lambda n: (0, 0, 0))
    kshape = jax.ShapeDtypeStruct(k.shape, F32)
    return _call(
        body, "attn_bwd", (t // WINDOW,),
        [pl.BlockSpec(memory_space=pltpu.SMEM), qspec, prev, cur, prev, cur, qspec, qspec],
        [qspec, cur, cur, cur, cur, sspec],
        [jax.ShapeDtypeStruct(q.shape, F32), kshape, kshape, kshape, kshape,
         jax.ShapeDtypeStruct((n_kv, rows, 1), F32)],
        [], ("arbitrary",), (sinks, q, k, k, v, v, o, do), comm)


def _cmul(ar, ai, br, bi):
    return ar * br - ai * bi, ar * bi + ai * br


def _scan_consts(ar, ai, half, reverse):
    row = lax.broadcasted_iota(jnp.int32, (SUBLANES, half), 0)
    a2 = _cmul(ar, ai, ar, ai)
    a4 = _cmul(*a2, *a2)
    steps = [(1, ar, ai), (2, *a2), (4, *a4)]
    pr, pi = ar, ai
    pwr = jnp.zeros((SUBLANES, half), F32)
    pwi = jnp.zeros((SUBLANES, half), F32)
    for r in range(SUBLANES):
        sel = row == (SUBLANES - 1 - r if reverse else r)
        pwr = jnp.where(sel, pr, pwr)
        pwi = jnp.where(sel, pi, pwi)
        pr, pi = _cmul(pr, pi, ar, ai)
    return row, steps, pwr, pwi


def _scan8(xr, xi, row, steps, pwr, pwi, cr, ci, reverse):
    for d, er, ei in steps:
        if reverse:
            keep, shift = row < SUBLANES - d, SUBLANES - d
        else:
            keep, shift = row >= d, d
        sr = jnp.where(keep, pltpu.roll(xr, shift, 0), 0.0)
        si = jnp.where(keep, pltpu.roll(xi, shift, 0), 0.0)
        tr, ti = _cmul(er, ei, sr, si)
        xr, xi = xr + tr, xi + ti
    tr, ti = _cmul(pwr, pwi, cr, ci)
    return xr + tr, xi + ti


def _ssm_fwd(proj, ucb0, bexp, cexp, a_cat, d_skip, tt=512, comm=None):
    t = proj.shape[0]
    ngb, cw, two_l = bexp.shape
    half = two_l // 2
    tt = _tile(t, tt, SUBLANES * 2)
    nt = t // tt

    def body(u_ref, b_ref, c_ref, a_ref, d_ref, y_ref, z_ref, st_ref, carry_ref):
        @pl.when(pl.program_id(1) == 0)
        def _():
            carry_ref[...] = jnp.zeros_like(carry_ref)

        u = u_ref[...]
        st_ref[...] = jnp.dot(u.astype(BF16), b_ref[...], preferred_element_type=F32)
        ar, ai = a_ref[:, :half], a_ref[:, half:]
        row, steps, pwr, pwi = _scan_consts(ar, ai, half, False)

        def tile(i, carry):
            base = pl.multiple_of(i * SUBLANES, SUBLANES)
            xr, xi = _scan8(st_ref[pl.ds(base, SUBLANES), :half], st_ref[pl.ds(base, SUBLANES), half:],
                            row, steps, pwr, pwi, carry[0], carry[1], False)
            st_ref[pl.ds(base, SUBLANES), :half] = xr
            st_ref[pl.ds(base, SUBLANES), half:] = xi
            return xr[SUBLANES - 1:, :], xi[SUBLANES - 1:, :]

        cr, ci = lax.fori_loop(0, tt // SUBLANES, tile, (carry_ref[0:1, :half], carry_ref[0:1, half:]),
                               unroll=SCAN_UNROLL)
        carry_ref[0:1, :half] = cr
        carry_ref[0:1, half:] = ci
        y = jnp.dot(st_ref[...].astype(BF16), c_ref[...], preferred_element_type=F32) + d_ref[...] * u
        y_ref[...] = y
        z_ref[...] = _gelu(y).astype(BF16)

    d_ssm = ngb * cw
    return _call(
        body, "ssm_fwd", (ngb, nt),
        [pl.BlockSpec((tt, cw), lambda g, i: (i, ucb0 + g)),
         pl.BlockSpec((None, cw, two_l), lambda g, i: (g, 0, 0)),
         pl.BlockSpec((None, two_l, cw), lambda g, i: (g, 0, 0)),
         pl.BlockSpec((None, 1, two_l), lambda g, i: (g, 0, 0)),
         pl.BlockSpec((1, cw), lambda g, i: (0, g))],
        [pl.BlockSpec((tt, cw), lambda g, i: (i, g)),
         pl.BlockSpec((tt, cw), lambda g, i: (i, g)),
         pl.BlockSpec((tt, two_l), lambda g, i: (i, g))],
        [jax.ShapeDtypeStruct((t, d_ssm), F32), jax.ShapeDtypeStruct((t, d_ssm), BF16),
         jax.ShapeDtypeStruct((t, ngb * two_l), F32)],
        [pltpu.VMEM((SUBLANES, two_l), F32)], ("parallel", "arbitrary"),
        (proj, bexp, cexp, a_cat, d_skip), comm)


def _ssm_bwd(dy0, proj, ucb0, states, bexp, cexp, a_cat, d_skip, tt=512, comm=None):
    t = dy0.shape[0]
    ngb, cw, two_l = bexp.shape
    half = two_l // 2
    tt = _tile(t, tt, SUBLANES * 2)
    nt = t // tt

    def body(dy_ref, u_ref, st_ref, b_ref, c_ref, a_ref, d_ref,
             du_ref, db_ref, dc_ref, da_ref, dd_ref, lam_ref, carry_ref, acc_ref):
        step = pl.program_id(1)

        @pl.when(step == 0)
        def _():
            carry_ref[...] = jnp.zeros_like(carry_ref)
            acc_ref[...] = jnp.zeros_like(acc_ref)
            db_ref[...] = jnp.zeros_like(db_ref)
            dc_ref[...] = jnp.zeros_like(dc_ref)
            dd_ref[...] = jnp.zeros_like(dd_ref)

        dy, u = dy_ref[...], u_ref[...]
        dyb = dy.astype(BF16)
        lam_ref[...] = lax.dot_general(dyb, c_ref[...], _NT, preferred_element_type=F32)
        ar, ai = a_ref[:, :half], -a_ref[:, half:]
        row, steps, pwr, pwi = _scan_consts(ar, ai, half, True)
        last = row == SUBLANES - 1

        def tile(i, carry):
            cr, ci, accr, acci = carry
            base = pl.multiple_of((tt // SUBLANES - 1 - i) * SUBLANES, SUBLANES)
            xr, xi = _scan8(lam_ref[pl.ds(base, SUBLANES), :half], lam_ref[pl.ds(base, SUBLANES), half:],
                            row, steps, pwr, pwi, cr, ci, True)
            lam_ref[pl.ds(base, SUBLANES), :half] = xr
            lam_ref[pl.ds(base, SUBLANES), half:] = xi
            nr = jnp.where(last, cr, pltpu.roll(xr, SUBLANES - 1, 0))
            ni = jnp.where(last, ci, pltpu.roll(xi, SUBLANES - 1, 0))
            sr, si = st_ref[pl.ds(base, SUBLANES), :half], st_ref[pl.ds(base, SUBLANES), half:]
            return xr[0:1, :], xi[0:1, :], accr + sr * nr + si * ni, acci + sr * ni - si * nr

        cr, ci, accr, acci = lax.fori_loop(
            0, tt // SUBLANES, tile,
            (carry_ref[0:1, :half], carry_ref[0:1, half:], acc_ref[:, :half], acc_ref[:, half:]),
            unroll=SCAN_UNROLL)
        carry_ref[0:1, :half] = cr
        carry_ref[0:1, half:] = ci
        acc_ref[:, :half] = accr
        acc_ref[:, half:] = acci
        lamb = lam_ref[...].astype(BF16)
        du = lax.dot_general(lamb, b_ref[...], _NT, preferred_element_type=F32) + d_ref[...] * dy
        du_ref[...] = du.astype(BF16)
        db_ref[...] += lax.dot_general(u.astype(BF16), lamb, _TN, preferred_element_type=F32)
        dc_ref[...] += lax.dot_general(st_ref[...].astype(BF16), dyb, _TN, preferred_element_type=F32)
        dd_ref[...] += _colsum(dy * u)

        @pl.when(step == nt - 1)
        def _():
            da_ref[...] = _colsum(acc_ref[...])

    d_ssm = ngb * cw
    return _call(
        body, "ssm_bwd", (ngb, nt),
        [pl.BlockSpec((tt, cw), lambda g, i: (nt - 1 - i, g)),
         pl.BlockSpec((tt, cw), lambda g, i: (nt - 1 - i, ucb0 + g)),
         pl.BlockSpec((tt, two_l), lambda g, i: (nt - 1 - i, g)),
         pl.BlockSpec((None, cw, two_l), lambda g, i: (g, 0, 0)),
         pl.BlockSpec((None, two_l, cw), lambda g, i: (g, 0, 0)),
         pl.BlockSpec((None, 1, two_l), lambda g, i: (g, 0, 0)),
         pl.BlockSpec((1, cw), lambda g, i: (0, g))],
        [pl.BlockSpec((tt, cw), lambda g, i: (nt - 1 - i, g)),
         pl.BlockSpec((None, cw, two_l), lambda g, i: (g, 0, 0)),
         pl.BlockSpec((None, two_l, cw), lambda g, i: (g, 0, 0)),
         pl.BlockSpec((None, 1, two_l), lambda g, i: (g, 0, 0)),
         pl.BlockSpec((1, cw), lambda g, i: (0, g))],
        [jax.ShapeDtypeStruct((t, d_ssm), BF16),
         jax.ShapeDtypeStruct((ngb, cw, two_l), F32),
         jax.ShapeDtypeStruct((ngb, two_l, cw), F32),
         jax.ShapeDtypeStruct((ngb, 1, two_l), F32),
         jax.ShapeDtypeStruct((1, d_ssm), F32)],
        [pltpu.VMEM((tt, two_l), F32), pltpu.VMEM((SUBLANES, two_l), F32), pltpu.VMEM((SUBLANES, two_l), F32)],
        ("parallel", "arbitrary"), (dy0, proj, states, bexp, cexp, a_cat, d_skip), comm)


def _zoh(lr, li, ls):
    step = jnp.exp(ls)
    e = jnp.exp(lr * step)
    ar, ai = e * jnp.cos(li * step), e * jnp.sin(li * step)
    den = lr * lr + li * li
    cr = ((ar - 1.0) * lr + ai * li) / den
    ci = (ai * lr - (ar - 1.0) * li) / den
    return step, ar, ai, den, cr, ci


def _ssm_param_fwd(lr, li, ls, br, bi):
    def body(lr_ref, li_ref, ls_ref, br_ref, bi_ref, ar_ref, ai_ref, bbr_ref, bbi_ref):
        _, ar, ai, _, cr, ci = _zoh(lr_ref[...], li_ref[...], ls_ref[...])
        ar_ref[...] = ar
        ai_ref[...] = ai
        bbr, bbi = _cmul(cr, ci, br_ref[...], bi_ref[...])
        bbr_ref[...] = bbr
        bbi_ref[...] = bbi

    small, big = jax.ShapeDtypeStruct(lr.shape, F32), jax.ShapeDtypeStruct(br.shape, F32)
    return _pcall(body, name="ssm_param_fwd", out_shape=[small, small, big, big],
                  compiler_params=_params())(lr, li, ls, br, bi)


def _ssm_param_bwd(lr, li, ls, br, bi, gar, gai, gbr, gbi):
    def body(lr_ref, li_ref, ls_ref, br_ref, bi_ref, gar_ref, gai_ref, gbr_ref, gbi_ref,
             dlr_ref, dli_ref, dls_ref, dbr_ref, dbi_ref):
        lrv, liv = lr_ref[...], li_ref[...]
        step, ar, ai, den, cr, ci = _zoh(lrv, liv, ls_ref[...])
        brv, biv, gr, gi = br_ref[...], bi_ref[...], gbr_ref[...], gbi_ref[...]
        dbr_ref[...] = cr * gr + ci * gi
        dbi_ref[...] = cr * gi - ci * gr
        gcr = jnp.sum(brv * gr + biv * gi, axis=1, keepdims=True)
        gci = jnp.sum(brv * gi - biv * gr, axis=1, keepdims=True)
        gtr = gar_ref[...] + (lrv * gcr - liv * gci) / den
        gti = gai_ref[...] + (lrv * gci + liv * gcr) / den
        qr = (cr * lrv + ci * liv) / den
        qi = (ci * lrv - cr * liv) / den
        gzr = ar * gtr + ai * gti
        gzi = ar * gti - ai * gtr
        dlr_ref[...] = step * gzr - (qr * gcr + qi * gci)
        dli_ref[...] = step * gzi - (qr * gci - qi * gcr)
        gstep = jnp.sum(lrv * gzr + liv * gzi, axis=2, keepdims=True)
        dls_ref[...] = jnp.broadcast_to(step * gstep, step.shape)

    small, big = jax.ShapeDtypeStruct(lr.shape, F32), jax.ShapeDtypeStruct(br.shape, F32)
    return _pcall(body, name="ssm_param_bwd", out_shape=[small, small, small, big, big],
                  compiler_params=_params())(lr, li, ls, br, bi, gar, gai, gbr, gbi)


def _block_diag_in(bb):
    g, h, p = bb.shape
    nb, n = g // GROUPS_PER_BLOCK, GROUPS_PER_BLOCK
    b4 = bb.reshape(nb, n, h, p)
    rows = [jnp.pad(b4[:, k], ((0, 0), (0, 0), (k * p, (n - 1 - k) * p))) for k in range(n)]
    return jnp.concatenate(rows, axis=1)


def _block_diag_take(e, h, p):
    nb, n = e.shape[0], GROUPS_PER_BLOCK
    d = jnp.stack([e[:, k * h:(k + 1) * h, k * p:(k + 1) * p] for k in range(n)], axis=1)
    return d.reshape(nb * n, h, p)


def _ada_fwd(c_all, w_sh, b_sh, tn=512):
    bsz, d = c_all.shape
    nsh = w_sh.shape[1]
    tn = _tile(nsh, tn, LANES)

    def body(c_ref, w_ref, b_ref, mod_ref, act_ref):
        act = c_ref[...] * jax.nn.sigmoid(c_ref[...])
        act_ref[...] = act
        mod_ref[...] = jnp.dot(act.astype(BF16), w_ref[...].astype(BF16), preferred_element_type=F32) + b_ref[...]

    return _pcall(
        body, name="ada_fwd", grid=(nsh // tn,),
        in_specs=[pl.BlockSpec((bsz, d), lambda j: (0, 0)), pl.BlockSpec((d, tn), lambda j: (0, j)),
                  pl.BlockSpec((1, tn), lambda j: (0, j))],
        out_specs=[pl.BlockSpec((bsz, tn), lambda j: (0, j)), pl.BlockSpec((bsz, d), lambda j: (0, 0))],
        out_shape=[jax.ShapeDtypeStruct((bsz, nsh), F32), jax.ShapeDtypeStruct((bsz, d), F32)],
        compiler_params=_params(("arbitrary",)),
    )(c_all, w_sh, b_sh)


def _adamw(w, g, m, v, name, outer=None):
    r, c = w.shape
    tr = _tile(r, max(SUBLANES, (256 * 1024) // c // SUBLANES * SUBLANES), SUBLANES)
    c1, c2 = 1.0 / (1.0 - ADAM_B1 ** ADAM_STEP), 1.0 / (1.0 - ADAM_B2 ** ADAM_STEP)
    n_g = 1 if outer is None else 2

    def body(w_ref, m_ref, v_ref, *rest):
        g_refs, (go_ref, d_ref, nm_ref, nv_ref) = rest[:n_g], rest[n_g:]
        if outer is None:
            gv = g_refs[0][...]
        else:
            gv = jnp.dot(g_refs[0][...], g_refs[1][...], precision=lax.Precision.HIGHEST,
                         preferred_element_type=F32)
        nm = ADAM_B1 * m_ref[...] + (1.0 - ADAM_B1) * gv
        nv = ADAM_B2 * v_ref[...] + (1.0 - ADAM_B2) * (gv * gv)
        go_ref[...] = gv
        nm_ref[...] = nm
        nv_ref[...] = nv
        d_ref[...] = -ADAM_LR * ((nm * c1) / (jnp.sqrt(nv * c2) + ADAM_EPS) + ADAM_WD * w_ref[...])

    spec = pl.BlockSpec((tr, c), lambda i: (i, 0))
    if outer is None:
        g_specs, g_ops = [spec], (g,)
    else:
        a, b = outer
        g_specs = [pl.BlockSpec((tr, a.shape[1]), lambda i: (i, 0)), pl.BlockSpec(b.shape, lambda i: (0, 0))]
        g_ops = (a, b)
    shp = jax.ShapeDtypeStruct((r, c), F32)
    res, _ = _call(body, name, (r // tr,), [spec] * 3 + g_specs, [spec] * 4, [shp] * 4, [], ("parallel",),
                   (w, m, v, *g_ops))
    return res


def _sum_leading(arr, out_dtype, name):
    n, r, c = arr.shape
    tr = _tile(r, max(SUBLANES * 2, (512 * 1024) // (c * n) // (SUBLANES * 2) * (SUBLANES * 2)), SUBLANES * 2)

    def body(x_ref, o_ref):
        acc = x_ref[0].astype(F32)
        for k in range(1, n):
            acc = acc + x_ref[k].astype(F32)
        o_ref[...] = acc.astype(out_dtype)

    return _pcall(body, name=name, grid=(r // tr,),
                  in_specs=[pl.BlockSpec((n, tr, c), lambda i: (0, i, 0))],
                  out_specs=pl.BlockSpec((tr, c), lambda i: (i, 0)),
                  out_shape=jax.ShapeDtypeStruct((r, c), out_dtype),
                  compiler_params=_params(("parallel",)))(arr)


def _place():
    x, y, c = lax.axis_index("x"), lax.axis_index("y"), lax.axis_index("c")
    chips = [(1 - x, y), (x, 1 - y), (1 - x, 1 - y)]
    return x, y, c, chips


def _allgather8(v, name):
    m, n = v.shape

    def body(x_ref, out_ref, send_sems, recv_sems, local_sem):
        x, y, c, chips = _place()
        me, sibling = (x, y, c), (x, y, 1 - c)

        def slot(px, py, pc):
            return out_ref.at[4 * px + 2 * py + pc]

        def copy(k, block, to, src=None):
            return pltpu.make_async_remote_copy(
                src_ref=slot(*block) if src is None else src, dst_ref=slot(*block),
                send_sem=send_sems.at[k], recv_sem=recv_sems.at[k], device_id=to, device_id_type=MESH)

        mine = pltpu.make_async_copy(x_ref, slot(*me), local_sem)
        mine.start()
        first = [copy(0, me, sibling, src=x_ref)]
        first += [copy(1 + j, me, (*chip, c), src=x_ref) for j, chip in enumerate(chips)]
        for cp in first:
            cp.start()
        passed = [copy(4 + j, (*chip, c), sibling) for j, chip in enumerate(chips)]
        for j, chip in enumerate(chips):
            copy(1 + j, (*chip, c), me).wait_recv()
            passed[j].start()
        copy(0, sibling, me).wait_recv()
        for j, chip in enumerate(chips):
            copy(4 + j, (*chip, 1 - c), me).wait_recv()
        for cp in first + passed:
            cp.wait_send()
        mine.wait()

    return _pcall(
        body, name=name, out_shape=jax.ShapeDtypeStruct((N_DEV, m, n), F32),
        in_specs=[pl.BlockSpec(memory_space=pltpu.VMEM)], out_specs=pl.BlockSpec(memory_space=pltpu.VMEM),
        scratch_shapes=[pltpu.SemaphoreType.DMA((7,)), pltpu.SemaphoreType.DMA((7,)), pltpu.SemaphoreType.DMA],
        compiler_params=_params(),
    )(v)


def _scalars(*vals):
    return jnp.stack([jnp.asarray(v, jnp.int32) for v in vals])


def _cast_place(w, chip, name):
    r, cdim = w.shape
    tr = _tile(r, max(SUBLANES * 2, (512 * 1024) // cdim // (SUBLANES * 2) * (SUBLANES * 2)), SUBLANES * 2)

    def body(s_ref, w_ref, o_ref):
        o_ref[...] = w_ref[...].astype(BF16)

    grid_spec = pltpu.PrefetchScalarGridSpec(
        num_scalar_prefetch=1, grid=(r // tr,),
        in_specs=[pl.BlockSpec((tr, cdim), lambda i, s: (i, 0))],
        out_specs=pl.BlockSpec((None, tr, cdim), lambda i, s: (s[0], i, 0)))
    return _pcall(body, name=name, grid_spec=grid_spec, out_shape=jax.ShapeDtypeStruct((N_CHIPS, r, cdim), BF16),
                  compiler_params=_params(("parallel",)))(_scalars(chip), w)


def _gather_weights(bufs):
    nw = len(bufs)

    def body(*refs):
        outs = refs[nw:2 * nw]
        send_sems, recv_sems = refs[2 * nw:]
        x, y, c, chips = _place()
        me, sibling = (x, y, c), (x, y, 1 - c)

        def copy(w, k, chip, hc, to):
            h = outs[w].shape[1] // 2
            ref = outs[w].at[2 * chip[0] + chip[1], pl.ds(pl.multiple_of(hc * h, SUBLANES * 2), h)]
            return pltpu.make_async_remote_copy(
                src_ref=ref, dst_ref=ref, send_sem=send_sems.at[w, k], recv_sem=recv_sems.at[w, k],
                device_id=to, device_id_type=MESH)

        sent = []
        for w in range(nw):
            for k, chip in enumerate(chips):
                sent.append(copy(w, k, (x, y), c, (*chip, c)))
                sent[-1].start()
        for w in range(nw):
            for k, chip in enumerate(chips):
                copy(w, k, chip, c, me).wait_recv()
                sent.append(copy(w, 3 + k, chip, c, sibling))
                sent[-1].start()
        for w in range(nw):
            for k, chip in enumerate(chips):
                copy(w, 3 + k, chip, 1 - c, me).wait_recv()
        for cp in sent:
            cp.wait_send()

    any_spec = pl.BlockSpec(memory_space=pl.ANY)
    return _pcall(
        body, name="gather_weights",
        out_shape=[jax.ShapeDtypeStruct(b.shape, b.dtype) for b in bufs],
        in_specs=[any_spec] * nw, out_specs=[any_spec] * nw,
        input_output_aliases={w: w for w in range(nw)},
        scratch_shapes=[pltpu.SemaphoreType.DMA((nw, 6)), pltpu.SemaphoreType.DMA((nw, 6))],
        compiler_params=_params(),
    )(*bufs)


class _Comm:
    def __init__(self, ins, outs, aliases, n_sems, start, finish):
        self.ins, self.outs, self.aliases, self.n_sems = ins, outs, aliases, n_sems
        self.start, self.finish = start, finish


def _comm_gather_ici(bufs, spans=None, forwards=None):
    nw = len(bufs)
    spans = spans or [(0, 1, 1)] * nw
    forwards = forwards or [None] * nw
    per = 2 * (N_CHIPS - 1)

    def rows(out, span, hc):
        lo, count, n = span
        unit = out.shape[1] // 2 // n
        return pl.ds(pl.multiple_of((hc * n + lo) * unit, SUBLANES * 2), unit * count)

    def copies(outs, send_sems, recv_sems, incoming):
        x, y, c, chips = _place()
        res = []
        for w, out in enumerate(outs):
            for k, chip in enumerate(chips):
                if spans[w] is not None:
                    blk = chip if incoming else (x, y)
                    ref = out.at[2 * blk[0] + blk[1], rows(out, spans[w], c)]
                    res.append(pltpu.make_async_remote_copy(
                        src_ref=ref, dst_ref=ref, send_sem=send_sems.at[w * per + k],
                        recv_sem=recv_sems.at[w * per + k], device_id=(*chip, c), device_id_type=MESH))
                if forwards[w] is not None:
                    ref = out.at[2 * chip[0] + chip[1], rows(out, forwards[w], 1 - c if incoming else c)]
                    res.append(pltpu.make_async_remote_copy(
                        src_ref=ref, dst_ref=ref, send_sem=send_sems.at[w * per + N_CHIPS - 1 + k],
                        recv_sem=recv_sems.at[w * per + N_CHIPS - 1 + k], device_id=(x, y, 1 - c),
                        device_id_type=MESH))
        return res

    def start(ci, co, send_sems, recv_sems):
        for cp in copies(co, send_sems, recv_sems, False):
            cp.start()

    def finish(ci, co, send_sems, recv_sems):
        for cp in copies(co, send_sems, recv_sems, True):
            cp.wait_recv()
        for cp in copies(co, send_sems, recv_sems, False):
            cp.wait_send()

    return _Comm(list(bufs), [jax.ShapeDtypeStruct(b.shape, b.dtype) for b in bufs],
                 {w: w for w in range(nw)}, per * nw, start, finish)


def _forward_halves(bufs, name, spans=None):
    nw = len(bufs)
    spans = spans or [(0, 1, 1)] * nw

    def body(*refs):
        outs = refs[nw:2 * nw]
        send_sems, recv_sems = refs[2 * nw:]
        x, y, c, chips = _place()

        def copy(w, k, hc):
            chip = chips[k]
            lo, count, n = spans[w]
            unit = outs[w].shape[1] // 2 // n
            ref = outs[w].at[2 * chip[0] + chip[1],
                             pl.ds(pl.multiple_of((hc * n + lo) * unit, SUBLANES * 2), unit * count)]
            return pltpu.make_async_remote_copy(
                src_ref=ref, dst_ref=ref, send_sem=send_sems.at[w, k], recv_sem=recv_sems.at[w, k],
                device_id=(x, y, 1 - c), device_id_type=MESH)

        pairs = [(w, k) for w in range(nw) for k in range(len(chips))]
        for w, k in pairs:
            copy(w, k, c).start()
        for w, k in pairs:
            copy(w, k, 1 - c).wait_recv()
        for w, k in pairs:
            copy(w, k, c).wait_send()

    any_spec = pl.BlockSpec(memory_space=pl.ANY)
    return _pcall(
        body, name=name,
        out_shape=[jax.ShapeDtypeStruct(b.shape, b.dtype) for b in bufs],
        in_specs=[any_spec] * nw, out_specs=[any_spec] * nw,
        input_output_aliases={w: w for w in range(nw)},
        scratch_shapes=[pltpu.SemaphoreType.DMA((nw, N_CHIPS - 1)), pltpu.SemaphoreType.DMA((nw, N_CHIPS - 1))],
        compiler_params=_params(),
    )(*bufs)


def _comm_chip_exchange(psums, spans=None, recvs=None):
    nw = len(psums)
    spans = spans or [(0, 1, 1)] * nw
    recvs = recvs or [None] * nw
    old = [w for w in range(nw) if recvs[w] is not None]
    new = [w for w in range(nw) if recvs[w] is None]

    def copies(ci, co, send_sems, recv_sems):
        x, y, c, chips = _place()
        dsts = {w: co[i] for i, w in enumerate(old + new)}
        res = []
        for w, (lo, count, n) in enumerate(spans):
            unit = psums[w].shape[1] // n
            rows = pl.ds(lo * unit, count * unit)
            for k, chip in enumerate(chips):
                res.append(pltpu.make_async_remote_copy(
                    src_ref=ci[w].at[2 * chip[0] + chip[1], rows], dst_ref=dsts[w].at[k, rows],
                    send_sem=send_sems.at[w * len(chips) + k], recv_sem=recv_sems.at[w * len(chips) + k],
                    device_id=(*chip, c), device_id_type=MESH))
        return res

    def start(ci, co, send_sems, recv_sems):
        for cp in copies(ci, co, send_sems, recv_sems):
            cp.start()

    def finish(ci, co, send_sems, recv_sems):
        cps = copies(ci, co, send_sems, recv_sems)
        for cp in cps:
            cp.wait_recv()
        for cp in cps:
            cp.wait_send()

    shape = lambda p: jax.ShapeDtypeStruct((N_CHIPS - 1,) + p.shape[1:], p.dtype)
    return _Comm(list(psums) + [recvs[w] for w in old], [shape(psums[w]) for w in old + new],
                 {nw + i: i for i in range(len(old))}, (N_CHIPS - 1) * nw, start, finish)


class _Offset:
    class _At:
        def __init__(self, sems, base):
            self.sems, self.base = sems, base

        def __getitem__(self, k):
            return self.sems.at[self.base + k]

    def __init__(self, sems, base):
        self.at = _Offset._At(sems, base)


def _join(a, b):
    na_i, na_o = len(a.ins), len(a.outs)

    def both(fa, fb):
        def run(ci, co, send_sems, recv_sems):
            fa(ci[:na_i], co[:na_o], send_sems, recv_sems)
            fb(ci[na_i:], co[na_o:], _Offset(send_sems, a.n_sems), _Offset(recv_sems, a.n_sems))
        return run

    aliases = dict(a.aliases)
    aliases.update({na_i + i: na_o + o for i, o in b.aliases.items()})
    return _Comm(list(a.ins) + list(b.ins), list(a.outs) + list(b.outs), aliases, a.n_sems + b.n_sems,
                 both(a.start, b.start), both(a.finish, b.finish))


def _comm_pair_exchange(views):
    def copies(ci, co, send_sems, recv_sems):
        x, y, c, _ = _place()
        return [pltpu.make_async_remote_copy(
            src_ref=ci[w].at[k, 1 - c], dst_ref=co[w].at[k], send_sem=send_sems.at[w * N_CHIPS + k],
            recv_sem=recv_sems.at[w * N_CHIPS + k], device_id=(x, y, 1 - c), device_id_type=MESH)
            for w in range(len(views)) for k in range(N_CHIPS)]

    def start(ci, co, send_sems, recv_sems):
        for cp in copies(ci, co, send_sems, recv_sems):
            cp.start()

    def finish(ci, co, send_sems, recv_sems):
        cps = copies(ci, co, send_sems, recv_sems)
        for cp in cps:
            cp.wait_recv()
        for cp in cps:
            cp.wait_send()

    outs = [jax.ShapeDtypeStruct((N_CHIPS,) + v.shape[2:], v.dtype) for v in views]
    return _Comm(list(views), outs, {}, N_CHIPS * len(views), start, finish)


def _pair_exchange(grads, name):
    nw = len(grads)

    def body(*refs):
        ins, outs = refs[:nw], refs[nw:2 * nw]
        send_sems, recv_sems = refs[2 * nw:]
        x, y, c, _ = _place()
        sibling = (x, y, 1 - c)

        def copy(w, k):
            return pltpu.make_async_remote_copy(
                src_ref=ins[w].at[k, 1 - c], dst_ref=outs[w].at[k],
                send_sem=send_sems.at[w, k], recv_sem=recv_sems.at[w, k], device_id=sibling, device_id_type=MESH)

        copies = [copy(w, k) for w in range(nw) for k in range(N_CHIPS)]
        for cp in copies:
            cp.start()
        for cp in copies:
            cp.wait_recv()
        for cp in copies:
            cp.wait_send()

    any_spec = pl.BlockSpec(memory_space=pl.ANY)
    return _pcall(
        body, name=name,
        out_shape=[jax.ShapeDtypeStruct((N_CHIPS,) + g.shape[2:], g.dtype) for g in grads],
        in_specs=[any_spec] * nw, out_specs=[any_spec] * nw,
        scratch_shapes=[pltpu.SemaphoreType.DMA((nw, N_CHIPS)), pltpu.SemaphoreType.DMA((nw, N_CHIPS))],
        compiler_params=_params(),
    )(*grads)


def _pair_sum(view, recv, core, name):
    n, _, h, cdim = view.shape
    th = _tile(h, max(SUBLANES * 2, (512 * 1024) // cdim // (SUBLANES * 2) * (SUBLANES * 2)), SUBLANES * 2)

    def body(s_ref, a_ref, b_ref, o_ref):
        o_ref[...] = (a_ref[...].astype(F32) + b_ref[...].astype(F32)).astype(BF16)

    grid_spec = pltpu.PrefetchScalarGridSpec(
        num_scalar_prefetch=1, grid=(n, h // th),
        in_specs=[pl.BlockSpec((None, None, th, cdim), lambda k, i, s: (k, s[0], i, 0)),
                  pl.BlockSpec((None, th, cdim), lambda k, i, s: (k, i, 0))],
        out_specs=pl.BlockSpec((None, th, cdim), lambda k, i, s: (k, i, 0)))
    return _pcall(body, name=name, grid_spec=grid_spec, out_shape=jax.ShapeDtypeStruct((n, h, cdim), BF16),
                  compiler_params=_params(("parallel", "parallel")))(_scalars(core), view, recv)


def _chip_sum(psums, recv, chip, core, name):
    _, h, cdim = psums.shape
    th = _tile(h, max(SUBLANES * 2, (256 * 1024) // cdim // (SUBLANES * 2) * (SUBLANES * 2)), SUBLANES * 2)

    def body(chip_ref, core_ref, a_ref, b_ref, o_ref):
        acc = a_ref[...].astype(F32)
        for k in range(N_CHIPS - 1):
            acc = acc + b_ref[k].astype(F32)
        o_ref[...] = acc

    grid_spec = pltpu.PrefetchScalarGridSpec(
        num_scalar_prefetch=2, grid=(h // th,),
        in_specs=[pl.BlockSpec((None, th, cdim), lambda i, s, t: (s[0], i, 0)),
                  pl.BlockSpec((N_CHIPS - 1, th, cdim), lambda i, s, t: (0, i, 0))],
        out_specs=pl.BlockSpec((None, th, cdim), lambda i, s, t: (t[0], i, 0)))
    return _pcall(body, name=name, grid_spec=grid_spec, out_shape=jax.ShapeDtypeStruct((2, h, cdim), F32),
                  compiler_params=_params(("parallel",)))(_scalars(chip), _scalars(core), psums, recv)


def _share_halves(bufs):
    nw = len(bufs)

    def body(*refs):
        outs = refs[nw:2 * nw]
        send_sems, recv_sems = refs[2 * nw:]
        x, y, c, _ = _place()
        copies = [pltpu.make_async_remote_copy(
            src_ref=outs[w].at[c], dst_ref=outs[w].at[c], send_sem=send_sems.at[w], recv_sem=recv_sems.at[w],
            device_id=(x, y, 1 - c), device_id_type=MESH) for w in range(nw)]
        for cp in copies:
            cp.start()
        for w in range(nw):
            pltpu.make_async_remote_copy(
                src_ref=outs[w].at[1 - c], dst_ref=outs[w].at[1 - c], send_sem=send_sems.at[w],
                recv_sem=recv_sems.at[w], device_id=(x, y, 1 - c), device_id_type=MESH).wait_recv()
        for cp in copies:
            cp.wait_send()

    any_spec = pl.BlockSpec(memory_space=pl.ANY)
    return _pcall(
        body, name="grad_share_halves",
        out_shape=[jax.ShapeDtypeStruct(b.shape, b.dtype) for b in bufs],
        in_specs=[any_spec] * nw, out_specs=[any_spec] * nw,
        input_output_aliases={w: w for w in range(nw)},
        scratch_shapes=[pltpu.SemaphoreType.DMA((nw,)), pltpu.SemaphoreType.DMA((nw,))],
        compiler_params=_params(),
    )(*bufs)


def _pack(arrays):
    flat = jnp.concatenate([a.reshape(-1).astype(F32) for a in arrays])
    unit = 2 * SUBLANES * PACK_COLS
    pad = (-flat.shape[0]) % unit
    return jnp.pad(flat, (0, pad)).reshape(-1, PACK_COLS)


def _unpack(buf, shapes):
    flat, out, off = buf.reshape(-1), [], 0
    for s in shapes:
        n = math.prod(s)
        out.append(flat[off:off + n].reshape(s))
        off += n
    return out


def kernel(x, c, w_ada, b_ada, norm1_g, w_in, sinks, ssm_lam_re, ssm_lam_im, ssm_log_step, ssm_b_re, ssm_b_im, ssm_c_re, ssm_c_im, ssm_d, w_glu, b_glu, attn_out_g, ssm_out_g, w_out, norm2_g, w_ff1, w_ff2, final_g, loss_target, m_w_ada, m_b_ada, m_norm1_g, m_w_in, m_sinks, m_ssm_lam_re, m_ssm_lam_im, m_ssm_log_step, m_ssm_b_re, m_ssm_b_im, m_ssm_c_re, m_ssm_c_im, m_ssm_d, m_w_glu, m_b_glu, m_attn_out_g, m_ssm_out_g, m_w_out, m_norm2_g, m_w_ff1, m_w_ff2, m_final_g, v_w_ada, v_b_ada, v_norm1_g, v_w_in, v_sinks, v_ssm_lam_re, v_ssm_lam_im, v_ssm_log_step, v_ssm_b_re, v_ssm_b_im, v_ssm_c_re, v_ssm_c_im, v_ssm_d, v_w_glu, v_b_glu, v_attn_out_g, v_ssm_out_g, v_w_out, v_norm2_g, v_w_ff1, v_w_ff2, v_final_g):
    t, d = x.shape[1], x.shape[2]
    d_attn, d_ssm = attn_out_g.shape[1], ssm_d.shape[1]
    d_in = w_in.shape[2] * N_CHIPS
    d_kv = (d_in - d_attn - d_ssm) // 2
    n_q, n_kv = d_attn // HEAD_DIM, d_kv // HEAD_DIM
    n_grp = ssm_lam_re.shape[1]
    assert n_q == n_kv * Q_PER_KV and t % WINDOW == 0 and d_ssm == n_grp * SSM_GROUP
    assert d_kv % LANES == 0 and d_attn % d_kv == 0 and n_grp % GROUPS_PER_BLOCK == 0
    cw = GROUPS_PER_BLOCK * SSM_GROUP
    ucb0 = (d_attn + 2 * d_kv) // cw
    assert (d_attn + 2 * d_kv) % cw == 0
    xi, yi, ci = lax.axis_index("x"), lax.axis_index("y"), lax.axis_index("c")
    chip = 2 * xi + yi
    dev = 2 * chip + ci
    xs, tgt = x[0], loss_target[0]
    vec = lambda a: a.reshape(1, -1)

    n_ada = w_ada.shape[2]
    c_all = _allgather8(c.reshape(SUBLANES, d // SUBLANES), "gather_c").reshape(N_DEV, d)
    b_sh = lax.dynamic_slice_in_dim(b_ada, chip * n_ada, n_ada, axis=1)
    mod_sh, c_act = _ada_fwd(c_all, w_ada[0], b_sh)
    mod_all = _allgather8(mod_sh, "gather_mod")
    mod_me = lax.dynamic_index_in_dim(mod_all[0::2], dev, axis=1, keepdims=False)
    mod_me = mod_me.reshape(N_CHIPS * n_ada // d, 1, d)
    shift1, scale1, gate1, shift2, scale2, gate2 = [mod_me[i] for i in range(N_MOD)]

    own = {n: _cast_place(w[0], chip, "cast_" + n)
           for n, w in (("w_in", w_in), ("w_glu", w_glu), ("w_out", w_out), ("w_ff1", w_ff1), ("w_ff2", w_ff2))}
    win_s, = _gather_weights([own["w_in"]])
    half_view = lambda g, w: g.reshape(N_CHIPS, 2, w.shape[0] // 2, w.shape[1])

    g3 = lambda a: a.reshape(n_grp, 1, STATE)
    lr3, li3 = g3(ssm_lam_re[0]), g3(ssm_lam_im[0])
    ls3 = jnp.broadcast_to(ssm_log_step[0].reshape(n_grp, 1, 1), (n_grp, 1, STATE))
    b_re3, b_im3 = ssm_b_re[0].transpose(0, 2, 1), ssm_b_im[0].transpose(0, 2, 1)
    a_re, a_im, bb_re, bb_im = _ssm_param_fwd(lr3, li3, ls3, b_re3, b_im3)
    ngb = n_grp // GROUPS_PER_BLOCK
    a_cat = jnp.concatenate([a_re.reshape(ngb, 1, -1), a_im.reshape(ngb, 1, -1)], axis=-1)
    bexp = jnp.concatenate([_block_diag_in(bb_re.astype(BF16)), _block_diag_in(bb_im.astype(BF16))], axis=-1)
    cexp = jnp.concatenate([_block_diag_in(ssm_c_re[0].astype(BF16)), _block_diag_in(-ssm_c_im[0].astype(BF16))],
                           axis=-1).transpose(0, 2, 1)

    half = HEAD_DIM // 2
    inv_freq = ROPE_THETA ** (-jnp.arange(half, dtype=F32) / half)
    ang = jnp.arange(t, dtype=F32)[:, None] * inv_freq[None, :]
    cos = jnp.tile(jnp.cos(ang), (1, LANES // half))
    sin = jnp.tile(jnp.sin(ang), (1, LANES // half))

    h = _norm_mod(xs, norm1_g, scale1, shift1)
    proj, (wglu_ici, wout_ici) = _matmul(h, win_s, "nn", "mm_in", [F32], b_stacked=True, tn=win_s.shape[2],
                                         comm=_comm_gather_ici([own["w_glu"], own["w_out"]]))
    qr, kr, vb = _rope_fwd(proj, cos, sin, d_attn, d_kv)
    heads = lambda a, n: a.reshape(t, n, HEAD_DIM).transpose(1, 0, 2)
    unheads = lambda a: a.transpose(1, 0, 2).reshape(t, -1)
    qh, kh, vh = heads(qr, n_q), heads(kr, n_kv), heads(vb, n_kv)
    whole = (0, 1, 1)
    oh, (wff1_ici, wglu_s, wout_s) = _attn_fwd(
        qh, kh, vh, sinks[0],
        comm=_comm_gather_ici([own["w_ff1"], wglu_ici, wout_ici], [(0, 2, 4), None, None], [None, whole, whole]))
    attn = unheads(oh)
    (y0, z, states), (wff1_ici, wff2_ici) = _ssm_fwd(
        proj, ucb0, bexp, cexp, a_cat, ssm_d,
        comm=_comm_gather_ici([wff1_ici, own["w_ff2"]], [(2, 2, 4), (0, 1, 8)], [(0, 2, 4), None]))
    wglu = wglu_s.reshape(d_ssm, d_ssm)
    wout = wout_s.reshape(d_attn + d_ssm, d)
    gl = _matmul(z, wglu, "nn", "mm_glu", [F32])
    mixed = _mix(attn, y0, gl, b_glu, attn_out_g, ssm_out_g)
    mo, (wff2_ici, wff1_s) = _matmul(
        mixed, wout, "nn", "mm_out", [F32],
        comm=_comm_gather_ici([wff2_ici, wff1_ici], [(1, 1, 8), None], [(0, 1, 8), (2, 2, 4)]))
    x1, h2 = _res_norm_mod(xs, mo, gate1, norm2_g, scale2, shift2)

    def relu2(acc):
        r = jnp.maximum(acc, 0.0)
        return acc, r * r

    (a_act, rr), (wff2_ici,) = _matmul(h2, wff1_s, "nn", "mm_ff1", [BF16, BF16], epilogue=relu2, b_stacked=True,
                                       comm=_comm_gather_ici([wff2_ici], [(2, 6, 8)], [(1, 1, 8)]))
    wff2 = _forward_halves([wff2_ici], "forward_halves_ff2", [(2, 6, 8)])[0].reshape(-1, d)
    ff = _matmul(rr, wff2, "nn", "mm_ff2", [F32])
    dx2, dff, loss_cols, dgf, dgate2 = _final(x1, ff, tgt, gate2, vec(final_g))
    loss = lax.psum(0.5 * jnp.sum(loss_cols) / d, ("x", "y", "c"))

    d_relu2 = lambda acc, av: (acc * 2.0 * jnp.maximum(av.astype(F32), 0.0),)
    v_ff2 = half_view(_matmul(rr, dff, "tn", "mm_gw_ff2", [BF16]), w_ff2[0])
    da, (x_ff2,) = _matmul(dff, wff2, "nt", "mm_dff2", [BF16], epilogue=d_relu2, extras=(a_act,),
                           comm=_comm_pair_exchange([v_ff2]))
    p_ff2 = _pair_sum(v_ff2, x_ff2, ci, "pair_sum_ff2")
    gw_ff1, (r_ff2,) = _matmul(h2, da, "tn", "mm_gw_ff1", [BF16], out_stacked=N_CHIPS,
                               comm=_comm_chip_exchange([p_ff2], [(0, 2, 4)]))
    v_ff1 = half_view(gw_ff1, w_ff1[0])
    dh2, (r_ff2, x_ff1) = _matmul(
        da, wff1_s, "nt", "mm_dff1", [F32], b_stacked=True,
        comm=_join(_comm_chip_exchange([p_ff2], [(2, 2, 4)], [r_ff2]), _comm_pair_exchange([v_ff1])))
    p_ff1 = _pair_sum(v_ff1, x_ff1, ci, "pair_sum_ff1")
    dx1, dmo, dshift2, dscale2, dg2, dgate1 = _bwd_norm2(x1, dh2, dx2, mo, norm2_g, scale2, gate1)
    dmixed, (r_ff1,) = _matmul(dmo, wout, "nt", "mm_dout", [F32], comm=_comm_chip_exchange([p_ff1], [(0, 1, 8)]))
    gw_out, (r_ff1,) = _matmul(mixed, dmo, "tn", "mm_gw_out", [BF16],
                               comm=_comm_chip_exchange([p_ff1], [(1, 1, 8)], [r_ff1]))
    dattn, dgl, dzp, dga, dgs, dbglu = _bwd_mix(dmixed, attn, y0, gl, b_glu, attn_out_g, ssm_out_g)
    d_gelu = lambda acc, dz, yv: ((acc + dz) * _gelu_grad(yv),)
    dy0 = _matmul(dgl, wglu, "nt", "mm_dglu", [F32], epilogue=d_gelu, extras=(dzp, y0))
    gw_glu = _matmul(z, dgl, "tn", "mm_gw_glu", [BF16])
    v_mix = [half_view(gw_glu, w_glu[0]), half_view(gw_out, w_out[0])]
    p_mix = [_pair_sum(v, p, ci, "pair_sum_" + n)
             for n, v, p in zip(("glu", "out"), v_mix, _pair_exchange(v_mix, "pair_exchange_mix"))]
    (du, dbexp, dcexp, da_bar, dd), (r_ff1, *r_mix) = _ssm_bwd(
        dy0, proj, ucb0, states, bexp, cexp, a_cat, ssm_d,
        comm=_join(_comm_chip_exchange([p_ff1], [(2, 2, 8)], [r_ff1]), _comm_chip_exchange(p_mix)))
    doh = heads(dattn, n_q)
    (dqh, dkc, dkp, dvc, dvp, dsink), (r_ff1,) = _attn_bwd(
        qh, kh, vh, oh, doh, sinks[0], comm=_comm_chip_exchange([p_ff1], [(4, 4, 8)], [r_ff1]))
    up = lambda a: jnp.concatenate([unheads(a)[WINDOW:], jnp.zeros((WINDOW, d_kv), F32)], axis=0)
    dq, dk, dv = _rope_bwd(unheads(dqh), unheads(dkc), up(dkp), unheads(dvc), up(dvp), cos, sin)
    dproj = jnp.concatenate([dq, dk, dv, du], axis=1)
    v_in = half_view(_matmul(h, dproj, "tn", "mm_gw_in", [BF16], out_stacked=N_CHIPS, tn=win_s.shape[2]), w_in[0])
    p_in = _pair_sum(v_in, _pair_exchange([v_in], "pair_exchange_in")[0], ci, "pair_sum_in")
    dh, (r_in,) = _matmul(dproj, win_s, "nt", "mm_din", [F32], b_stacked=True, tk=win_s.shape[2],
                          comm=_comm_chip_exchange([p_in]))
    grad_x, dshift1, dscale1, dg1 = _bwd_norm1(xs, dh, dx1, norm1_g, scale1)

    half_l = GROUPS_PER_BLOCK * STATE
    ga_re = da_bar[:, 0, :half_l].reshape(n_grp, 1, STATE)
    ga_im = da_bar[:, 0, half_l:].reshape(n_grp, 1, STATE)
    gbb_re = _block_diag_take(dbexp[:, :, :half_l], SSM_GROUP, STATE)
    gbb_im = _block_diag_take(dbexp[:, :, half_l:], SSM_GROUP, STATE)
    dcexp_t = dcexp.transpose(0, 2, 1)
    gc_re = _block_diag_take(dcexp_t[:, :, :half_l], SSM_GROUP, STATE)
    gc_im = -_block_diag_take(dcexp_t[:, :, half_l:], SSM_GROUP, STATE)
    dmod = jnp.concatenate([dshift1, dscale1, dgate1, dshift2, dscale2, dgate2], axis=1)
    dsinks = dsink.reshape(n_q, WINDOW).sum(axis=1)
    pieces = [dmod, dg1, dsinks, ga_re, ga_im, gbb_re, gbb_im, gc_re, gc_im, dd, dbglu, dga, dgs, dg2, dgf]
    gathered = _allgather8(_pack(pieces), "gather_small")
    summed = _sum_leading(gathered, F32, "sum_small")
    (g_b_ada, g_norm1, g_sinks, ga_re, ga_im, gbb_re, gbb_im, g_c_re, g_c_im, g_d, g_b_glu, g_attn_g, g_ssm_g,
     g_norm2, g_final) = _unpack(summed, [p.shape for p in pieces])
    g_lr, g_li, g_ls, g_b_re3, g_b_im3 = _ssm_param_bwd(lr3, li3, ls3, b_re3, b_im3, ga_re, ga_im, gbb_re, gbb_im)
    small_grads = [
        g_b_ada, g_norm1, g_sinks.reshape(1, -1), g_lr.reshape(1, n_grp, STATE), g_li.reshape(1, n_grp, STATE),
        g_ls[:, 0, 0].reshape(1, n_grp), g_b_re3.transpose(0, 2, 1)[None], g_b_im3.transpose(0, 2, 1)[None],
        g_c_re[None], g_c_im[None], g_d, g_b_glu, g_attn_g, g_ssm_g, g_norm2, g_final.reshape(-1)]
    small_w = [b_ada, norm1_g, sinks, ssm_lam_re, ssm_lam_im, ssm_log_step, ssm_b_re, ssm_b_im, ssm_c_re,
               ssm_c_im, ssm_d, b_glu, attn_out_g, ssm_out_g, norm2_g, final_g]
    small_m = [m_b_ada, m_norm1_g, m_sinks, m_ssm_lam_re, m_ssm_lam_im, m_ssm_log_step, m_ssm_b_re, m_ssm_b_im,
               m_ssm_c_re, m_ssm_c_im, m_ssm_d, m_b_glu, m_attn_out_g, m_ssm_out_g, m_norm2_g, m_final_g]
    small_v = [v_b_ada, v_norm1_g, v_sinks, v_ssm_lam_re, v_ssm_lam_im, v_ssm_log_step, v_ssm_b_re, v_ssm_b_im,
               v_ssm_c_re, v_ssm_c_im, v_ssm_d, v_b_glu, v_attn_out_g, v_ssm_out_g, v_norm2_g, v_final_g]
    small_grads = [g.reshape(w.shape) for g, w in zip(small_grads, small_w)]
    _, s_delta, s_m, s_v = _adamw(_pack(small_w), _pack(small_grads), _pack(small_m), _pack(small_v), "adamw_small")
    shapes = [w.shape for w in small_w]
    s_delta, s_m, s_v = _unpack(s_delta, shapes), _unpack(s_m, shapes), _unpack(s_v, shapes)

    dmod_rows = gathered.reshape(N_DEV, -1)[:, :dmod.shape[1]]
    dmod_sh = lax.dynamic_slice_in_dim(dmod_rows, chip * n_ada, n_ada, axis=1)
    upd_ada = _adamw(w_ada[0], None, m_w_ada[0], v_w_ada[0], "adamw_w_ada", outer=(c_act.T, dmod_sh))

    big_w = [w_in[0], w_glu[0], w_out[0], w_ff1[0], w_ff2[0]]
    psums = [p_in, *p_mix, p_ff1, p_ff2]
    recvd = [r_in, *r_mix, r_ff1, r_ff2]
    halves = [_chip_sum(p, r, chip, ci, f"chip_sum_{i}") for i, (p, r) in enumerate(zip(psums, recvd))]
    big_grads = [s.reshape(w.shape) for s, w in zip(_share_halves(halves), big_w)]

    big_names = ["w_in", "w_glu", "w_out", "w_ff1", "w_ff2"]
    big_m = [m_w_in[0], m_w_glu[0], m_w_out[0], m_w_ff1[0], m_w_ff2[0]]
    big_v = [v_w_in[0], v_w_glu[0], v_w_out[0], v_w_ff1[0], v_w_ff2[0]]
    big_upd = {n: _adamw(w, g, m, v, "adamw_" + n) for n, w, g, m, v in zip(big_names, big_w, big_grads, big_m, big_v)}
    big_upd["w_ada"] = upd_ada
    big_names = ["w_ada"] + big_names

    order = ["w_ada", "b_ada", "norm1_g", "w_in", "sinks", "ssm_lam_re", "ssm_lam_im", "ssm_log_step", "ssm_b_re",
             "ssm_b_im", "ssm_c_re", "ssm_c_im", "ssm_d", "w_glu", "b_glu", "attn_out_g", "ssm_out_g", "w_out",
             "norm2_g", "w_ff1", "w_ff2", "final_g"]
    small_names = [n for n in order if n not in big_names]
    grads, deltas, new_m, new_v = {}, {}, {}, {}
    for i, n in enumerate(small_names):
        grads[n], deltas[n], new_m[n], new_v[n] = small_grads[i], s_delta[i], s_m[i], s_v[i]
    for n in big_names:
        grads[n], deltas[n], new_m[n], new_v[n] = [a[None] for a in big_upd[n]]
    return (loss, grad_x[None], *[grads[n] for n in order], *[deltas[n] for n in order],
            *[new_m[n] for n in order], *[new_v[n] for n in order])
```

```python
import functools
import math

import jax
import jax.numpy as jnp
from jax import lax
from jax.experimental import pallas as pl
from jax.experimental.pallas import tpu as pltpu

F32 = jnp.float32
BF16 = jnp.bfloat16
MESH = pl.DeviceIdType.MESH

EPS = 1e-6
HEAD_DIM = 64
Q_PER_KV = 8
WINDOW = 128
ROPE_THETA = 10000.0
SSM_GROUP = 16
STATE = 64
GROUPS_PER_BLOCK = 16
SCAN_UNROLL = 2
N_MOD = 6
N_CHIPS = 4
N_DEV = 8
ADAM_LR = 0.001
ADAM_B1 = 0.9
ADAM_B2 = 0.999
ADAM_EPS = 1e-08
ADAM_WD = 0.01
ADAM_STEP = 10
LANES = 128
SUBLANES = 8
VMEM_LIMIT = 56 * 1024 * 1024
PACK_COLS = 512


def _pcall(body, **kw):
    return pl.pallas_call(body, **kw)


def _params(sem=None):
    return pltpu.CompilerParams(dimension_semantics=sem, vmem_limit_bytes=VMEM_LIMIT)


def _call(body, name, grid, in_specs, out_specs, out_shape, scratch, sem, operands, comm=None):
    if comm is None:
        res = _pcall(body, name=name, grid=grid, in_specs=in_specs, out_specs=out_specs, out_shape=out_shape,
                     scratch_shapes=scratch, compiler_params=_params(sem))(*operands)
        return tuple(res), ()
    n_in, n_out, n_ci, n_co = len(in_specs), len(out_specs), len(comm.ins), len(comm.outs)

    def carrying(*refs):
        ins, ci = refs[:n_in], refs[n_in:n_in + n_ci]
        outs = refs[n_in + n_ci:n_in + n_ci + n_out]
        co = refs[n_in + n_ci + n_out:n_in + n_ci + n_out + n_co]
        rest, send_sems, recv_sems = refs[n_in + n_ci + n_out + n_co:-2], refs[-2], refs[-1]
        ids = [pl.program_id(a) for a in range(len(grid))]
        first = functools.reduce(lambda p, q: p & q, [i == 0 for i in ids])
        last = functools.reduce(lambda p, q: p & q, [i == g - 1 for i, g in zip(ids, grid)])

        @pl.when(first)
        def _():
            comm.start(ci, co, send_sems, recv_sems)

        body(*ins, *outs, *rest)

        @pl.when(last)
        def _():
            comm.finish(ci, co, send_sems, recv_sems)

    any_spec = pl.BlockSpec(memory_space=pl.ANY)
    res = _pcall(
        carrying, name=name, grid=grid, in_specs=list(in_specs) + [any_spec] * n_ci,
        out_specs=list(out_specs) + [any_spec] * n_co, out_shape=list(out_shape) + list(comm.outs),
        input_output_aliases={n_in + ci: n_out + co for ci, co in comm.aliases.items()},
        scratch_shapes=list(scratch) + [pltpu.SemaphoreType.DMA((comm.n_sems,)), pltpu.SemaphoreType.DMA((comm.n_sems,))],
        compiler_params=_params(("arbitrary",) * len(grid)),
    )(*operands, *comm.ins)
    return tuple(res[:n_out]), tuple(res[n_out:])


def _tile(n, want, unit):
    if n <= want:
        return n
    t = (want // unit) * unit
    while t > unit and n % t:
        t -= unit
    assert n % t == 0, (n, want, unit)
    return t


_NN = (((1,), (0,)), ((), ()))
_NT = (((1,), (1,)), ((), ()))
_TN = (((0,), (0,)), ((), ()))


def _matmul(a, b, mode, name, out_dtypes, epilogue=None, extras=(), b_stacked=False, out_stacked=0,
            tm=1024, tn=1024, tk=4096, precision=None, comm=None):
    if mode == "nn":
        m, kdim = a.shape
        if b_stacked:
            s, _, nsh = b.shape
            n = s * nsh
            tn = _tile(nsh, tn, LANES)
        else:
            n = b.shape[1]
            tn = _tile(n, tn, LANES)
        tm, tk = _tile(m, tm, SUBLANES * 2), _tile(kdim, tk, LANES)
        a_spec = pl.BlockSpec((tm, tk), lambda i, j, k: (i, k))
        if b_stacked:
            npb = nsh // tn
            b_spec = pl.BlockSpec((None, tk, tn), lambda i, j, k: (j // npb, k, j % npb))
        else:
            b_spec = pl.BlockSpec((tk, tn), lambda i, j, k: (k, j))
        dims = _NN
    elif mode == "nt":
        m, kdim = a.shape
        if b_stacked:
            s, n, ksh = b.shape
            tk = _tile(ksh, tk, LANES)
            kpb = ksh // tk
            tn = _tile(n, tn, LANES)
            b_spec = pl.BlockSpec((None, tn, tk), lambda i, j, k: (k // kpb, j, k % kpb))
        else:
            n = b.shape[0]
            tk = _tile(kdim, tk, LANES)
            tn = _tile(n, tn, LANES)
            b_spec = pl.BlockSpec((tn, tk), lambda i, j, k: (j, k))
        tm = _tile(m, tm, SUBLANES * 2)
        a_spec = pl.BlockSpec((tm, tk), lambda i, j, k: (i, k))
        dims = _NT
    else:
        kdim, m = a.shape
        n = b.shape[1]
        tm = _tile(m, tm, LANES)
        tk = _tile(kdim, tk, SUBLANES * 2)
        if out_stacked:
            nsh = n // out_stacked
            tn = _tile(nsh, tn, LANES)
        else:
            tn = _tile(n, tn, LANES)
        a_spec = pl.BlockSpec((tk, tm), lambda i, j, k: (k, i))
        b_spec = pl.BlockSpec((tk, tn), lambda i, j, k: (k, j))
        dims = _TN
    nk = kdim // tk
    grid = (m // tm, n // tn, nk)
    if out_stacked:
        npo = (n // out_stacked) // tn
        o_spec = pl.BlockSpec((None, tm, tn), lambda i, j, k: (j // npo, i, j % npo))
        out_shape = [jax.ShapeDtypeStruct((out_stacked, m, n // out_stacked), dt) for dt in out_dtypes]
    else:
        o_spec = pl.BlockSpec((tm, tn), lambda i, j, k: (i, j))
        out_shape = [jax.ShapeDtypeStruct((m, n), dt) for dt in out_dtypes]
    x_spec = pl.BlockSpec((tm, tn), lambda i, j, k: (i, j))
    n_ex, n_out = len(extras), len(out_dtypes)

    def body(a_ref, b_ref, *rest):
        ex_refs, out_refs, acc_ref = rest[:n_ex], rest[n_ex:n_ex + n_out], rest[-1]
        k = pl.program_id(2)

        def finish(acc):
            outs = (acc,) if epilogue is None else epilogue(acc, *[r[...] for r in ex_refs])
            for r, o in zip(out_refs, outs):
                r[...] = o.astype(r.dtype)

        part = lax.dot_general(a_ref[...], b_ref[...], dims, precision=precision, preferred_element_type=F32)
        if nk == 1:
            finish(part)
        else:
            @pl.when(k == 0)
            def _():
                acc_ref[...] = part

            @pl.when(k > 0)
            def _():
                acc_ref[...] += part

            @pl.when(k == nk - 1)
            def _():
                finish(acc_ref[...])

    res, carried = _call(
        body, name, grid, [a_spec, b_spec] + [x_spec] * n_ex, [o_spec] * n_out, out_shape,
        [pltpu.VMEM((tm, tn) if nk > 1 else (SUBLANES, LANES), F32)], ("parallel", "parallel", "arbitrary"),
        (a, b, *extras), comm)
    main = res[0] if n_out == 1 else res
    return (main, carried) if comm else main


def _rowwise(body, name, rows, row_ins, vec_ins, row_outs, acc_outs, tr=128, comm=None):
    tr = _tile(rows, tr, SUBLANES * 2)
    n_ri, n_vi, n_ro, n_ao = len(row_ins), len(vec_ins), len(row_outs), len(acc_outs)

    def kern(*refs):
        ri, vi = refs[:n_ri], refs[n_ri:n_ri + n_vi]
        ro = refs[n_ri + n_vi:n_ri + n_vi + n_ro]
        ao = refs[n_ri + n_vi + n_ro:]

        @pl.when(pl.program_id(0) == 0)
        def _():
            for r in ao:
                r[...] = jnp.zeros_like(r)

        body(ri, vi, ro, ao)

    in_specs = [pl.BlockSpec((tr, w), functools.partial(lambda i, cb: (i, cb), cb=cb)) for _, w, cb in row_ins]
    in_specs += [pl.BlockSpec(v.shape, lambda i: (0, 0)) for v in vec_ins]
    out_specs = [pl.BlockSpec((tr, w), lambda i: (i, 0)) for w, _ in row_outs]
    out_specs += [pl.BlockSpec((1, w), lambda i: (0, 0)) for w in acc_outs]
    out_shape = [jax.ShapeDtypeStruct((rows, w), dt) for w, dt in row_outs]
    out_shape += [jax.ShapeDtypeStruct((1, w), F32) for w in acc_outs]
    res, carried = _call(kern, name, (rows // tr,), in_specs, out_specs, out_shape, [], ("arbitrary",),
                         (*[a for a, _, _ in row_ins], *vec_ins), comm)
    return (list(res), carried) if comm else list(res)


def _colsum(x):
    return jnp.sum(x, axis=0, keepdims=True)


def _rstd(x):
    return lax.rsqrt(jnp.mean(x * x, axis=-1, keepdims=True) + EPS)


def _norm_bwd(dxn, xn, r):
    return r * (dxn - xn * jnp.mean(dxn * xn, axis=-1, keepdims=True))


_SQRT_HALF = math.sqrt(0.5)
_INV_SQRT_2PI = 1.0 / math.sqrt(2.0 * math.pi)


def _gelu(y):
    return 0.5 * y * (1.0 + lax.erf(y * _SQRT_HALF))


def _gelu_grad(y):
    return 0.5 * (1.0 + lax.erf(y * _SQRT_HALF)) + y * jnp.exp(-0.5 * y * y) * _INV_SQRT_2PI


def _norm_mod(x, g, scale, shift, comm):
    def body(ri, vi, ro, ao):
        xv = ri[0][...]
        h = xv * _rstd(xv) * vi[0][...] * (1.0 + vi[1][...]) + vi[2][...]
        ro[0][...] = h.astype(BF16)

    d = x.shape[1]
    (h,), carried = _rowwise(body, "norm_mod", x.shape[0], [(x, d, 0)], [g, scale, shift], [(d, BF16)], [], comm=comm)
    return h, carried


def _res_norm_mod(x, mo, gate, g, scale, shift):
    def body(ri, vi, ro, ao):
        x1 = ri[0][...] + vi[0][...] * ri[1][...]
        ro[0][...] = x1
        ro[1][...] = (x1 * _rstd(x1) * vi[1][...] * (1.0 + vi[2][...]) + vi[3][...]).astype(BF16)

    d = x.shape[1]
    return _rowwise(body, "res_norm_mod", x.shape[0], [(x, d, 0), (mo, d, 0)], [gate, g, scale, shift],
                    [(d, F32), (d, BF16)], [])


def _mix(attn, y0, gl, b_glu, ga, gs):
    da, ds = attn.shape[1], y0.shape[1]

    def body(ri, vi, ro, ao):
        at = ri[0][...]
        z = _gelu(ri[1][...])
        o = z * jax.nn.sigmoid(ri[2][...] + vi[0][...])
        ro[0][:, :da] = (at * _rstd(at) * vi[1][...]).astype(BF16)
        ro[0][:, da:] = (o * _rstd(o) * vi[2][...]).astype(BF16)

    return _rowwise(body, "mix", attn.shape[0], [(attn, da, 0), (y0, ds, 0), (gl, ds, 0)], [b_glu, ga, gs],
                    [(da + ds, BF16)], [])[0]


def _final(x1, ff, tgt, gate2, gf):
    d = x1.shape[1]

    def body(ri, vi, ro, ao):
        ffv = ri[1][...]
        x2 = ri[0][...] + vi[0][...] * ffv
        r = _rstd(x2)
        xn = x2 * r
        e = xn * vi[1][...] - ri[2][...]
        ao[0][...] += _colsum(e * e)
        dy = e * (1.0 / d)
        ao[1][...] += _colsum(dy * xn)
        dx2 = _norm_bwd(dy * vi[1][...], xn, r)
        ao[2][...] += _colsum(dx2 * ffv)
        ro[0][...] = dx2
        ro[1][...] = (dx2 * vi[0][...]).astype(BF16)

    return _rowwise(body, "final", x1.shape[0], [(x1, d, 0), (ff, d, 0), (tgt, d, 0)], [gate2, gf],
                    [(d, F32), (d, BF16)], [d, d, d])


def _bwd_norm2(x1, dh2, dx2, mo, g2, scale2, gate1):
    d = x1.shape[1]

    def body(ri, vi, ro, ao):
        xv, dh = ri[0][...], ri[1][...]
        r = _rstd(xv)
        xn = xv * r
        ao[0][...] += _colsum(dh)
        ao[1][...] += _colsum(dh * xn * vi[0][...])
        dn = dh * (1.0 + vi[1][...])
        ao[2][...] += _colsum(dn * xn)
        dx1 = ri[2][...] + _norm_bwd(dn * vi[0][...], xn, r)
        ao[3][...] += _colsum(dx1 * ri[3][...])
        ro[0][...] = dx1
        ro[1][...] = (dx1 * vi[2][...]).astype(BF16)

    return _rowwise(body, "bwd_norm2", x1.shape[0], [(x1, d, 0), (dh2, d, 0), (dx2, d, 0), (mo, d, 0)],
                    [g2, scale2, gate1], [(d, F32), (d, BF16)], [d, d, d, d])


def _bwd_mix(dmixed, attn, y0, gl, b_glu, ga, gs):
    da, ds = attn.shape[1], y0.shape[1]

    def body(ri, vi, ro, ao):
        dan, dsn = ri[0][:, :da], ri[0][:, da:]
        at = ri[1][...]
        ra = _rstd(at)
        an = at * ra
        ao[0][...] += _colsum(dan * an)
        ro[0][...] = _norm_bwd(dan * vi[1][...], an, ra)
        z = _gelu(ri[2][...])
        sg = jax.nn.sigmoid(ri[3][...] + vi[0][...])
        o = z * sg
        rs = _rstd(o)
        on = o * rs
        ao[1][...] += _colsum(dsn * on)
        do = _norm_bwd(dsn * vi[2][...], on, rs)
        ro[2][...] = do * sg
        dgl = do * z * sg * (1.0 - sg)
        ao[2][...] += _colsum(dgl)
        ro[1][...] = dgl.astype(BF16)

    return _rowwise(body, "bwd_mix", attn.shape[0],
                    [(dmixed, da + ds, 0), (attn, da, 0), (y0, ds, 0), (gl, ds, 0)], [b_glu, ga, gs],
                    [(da, F32), (ds, BF16), (ds, F32)], [da, ds, ds])


def _bwd_norm1(x, dh, dx1, g1, scale1):
    d = x.shape[1]

    def body(ri, vi, ro, ao):
        xv, dhv = ri[0][...], ri[1][...]
        r = _rstd(xv)
        xn = xv * r
        ao[0][...] += _colsum(dhv)
        ao[1][...] += _colsum(dhv * xn * vi[0][...])
        dn = dhv * (1.0 + vi[1][...])
        ao[2][...] += _colsum(dn * xn)
        ro[0][...] = ri[2][...] + _norm_bwd(dn * vi[0][...], xn, r)

    return _rowwise(body, "bwd_norm1", x.shape[0], [(x, d, 0), (dh, d, 0), (dx1, d, 0)], [g1, scale1],
                    [(d, F32)], [d, d, d])


def _rope_apply(x, cos, sin, sign):
    first = (lax.broadcasted_iota(jnp.int32, cos.shape, 1) % HEAD_DIM) < (HEAD_DIM // 2)
    outs = []
    for j in range(x.shape[1] // LANES):
        xc = x[:, j * LANES:(j + 1) * LANES]
        rot = jnp.where(first, -pltpu.roll(xc, LANES - HEAD_DIM // 2, 1), pltpu.roll(xc, HEAD_DIM // 2, 1))
        outs.append(xc * cos + sign * (rot * sin))
    return outs


def _rope_fwd(proj, cos, sin, d_attn, d_kv):
    scale = HEAD_DIM ** -0.5
    kcb, vcb = d_attn // d_kv, d_attn // d_kv + 1

    def body(ri, vi, ro, ao):
        c, s = ri[3][...], ri[4][...]
        for j, o in enumerate(_rope_apply(ri[0][...], c, s, 1.0)):
            ro[0][:, j * LANES:(j + 1) * LANES] = (o * scale).astype(BF16)
        for j, o in enumerate(_rope_apply(ri[1][...], c, s, 1.0)):
            ro[1][:, j * LANES:(j + 1) * LANES] = o.astype(BF16)
        ro[2][...] = ri[2][...].astype(BF16)

    return _rowwise(body, "rope_fwd", proj.shape[0],
                    [(proj, d_attn, 0), (proj, d_kv, kcb), (proj, d_kv, vcb), (cos, LANES, 0), (sin, LANES, 0)], [],
                    [(d_attn, BF16), (d_kv, BF16), (d_kv, BF16)], [])


def _rope_bwd(dqr, dkc, dkp, dvc, dvp, cos, sin):
    scale = HEAD_DIM ** -0.5
    d_attn, d_kv = dqr.shape[1], dkc.shape[1]

    def body(ri, vi, ro, ao):
        c, s = ri[5][...], ri[6][...]
        for j, o in enumerate(_rope_apply(ri[0][...], c, s, -1.0)):
            ro[0][:, j * LANES:(j + 1) * LANES] = (o * scale).astype(BF16)
        for j, o in enumerate(_rope_apply(ri[1][...] + ri[2][...], c, s, -1.0)):
            ro[1][:, j * LANES:(j + 1) * LANES] = o.astype(BF16)
        ro[2][...] = (ri[3][...] + ri[4][...]).astype(BF16)

    return _rowwise(body, "rope_bwd", dqr.shape[0],
                    [(dqr, d_attn, 0), (dkc, d_kv, 0), (dkp, d_kv, 0), (dvc, d_kv, 0), (dvp, d_kv, 0),
                     (cos, LANES, 0), (sin, LANES, 0)], [],
                    [(d_attn, BF16), (d_kv, BF16), (d_kv, BF16)], [])


def _attn_probs(q, k, sink_ref, g, n):
    rows = Q_PER_KV * WINDOW
    s = lax.dot_general(q, k, _NT, preferred_element_type=F32)
    qi = lax.broadcasted_iota(jnp.int32, (rows, 2 * WINDOW), 0) % WINDOW + WINDOW
    kj = lax.broadcasted_iota(jnp.int32, (rows, 2 * WINDOW), 1)
    rel = qi - kj
    mask = (rel >= 0) & (rel < WINDOW) & ((n > 0) | (kj >= WINDOW))
    s = jnp.where(mask, s, -1e30)
    sink = jnp.concatenate([jnp.full((WINDOW, 1), sink_ref[g * Q_PER_KV + j], F32) for j in range(Q_PER_KV)], axis=0)
    m = jnp.maximum(jnp.max(s, axis=-1, keepdims=True), sink)
    p = jnp.exp(s - m)
    es = jnp.exp(sink - m)
    l = jnp.sum(p, axis=-1, keepdims=True) + es
    return p, l, es


def _attn_specs(n_q, n_kv):
    qspec = pl.BlockSpec((n_q, WINDOW, HEAD_DIM), lambda n: (0, n, 0))
    cur = pl.BlockSpec((n_kv, WINDOW, HEAD_DIM), lambda n: (0, n, 0))
    prev = pl.BlockSpec((n_kv, WINDOW, HEAD_DIM), lambda n: (0, jnp.maximum(n - 1, 0), 0))
    return qspec, cur, prev


def _attn_fwd(q, k, v, sinks, comm=None):
    n_q, n_kv, t = q.shape[0], k.shape[0], k.shape[1]
    rows = Q_PER_KV * WINDOW

    def body(sink_ref, q_ref, kp_ref, kc_ref, vp_ref, vc_ref, o_ref):
        n = pl.program_id(0)
        for g in range(n_kv):
            hs = slice(g * Q_PER_KV, (g + 1) * Q_PER_KV)
            qv = q_ref[hs].reshape(rows, HEAD_DIM)
            kv = jnp.concatenate([kp_ref[g], kc_ref[g]], axis=0)
            vv = jnp.concatenate([vp_ref[g], vc_ref[g]], axis=0)
            p, l, _ = _attn_probs(qv, kv, sink_ref, g, n)
            o = jnp.dot(p.astype(BF16), vv, preferred_element_type=F32) / l
            o_ref[hs] = o.reshape(Q_PER_KV, WINDOW, HEAD_DIM)

    qspec, cur, prev = _attn_specs(n_q, n_kv)
    (out,), carried = _call(
        body, "attn_fwd", (t // WINDOW,),
        [pl.BlockSpec(memory_space=pltpu.SMEM), qspec, prev, cur, prev, cur], [qspec],
        [jax.ShapeDtypeStruct(q.shape, F32)], [], ("arbitrary",), (sinks, q, k, k, v, v), comm)
    return out, carried


def _attn_bwd(q, k, v, o, do, sinks, comm=None):
    n_q, n_kv, t = q.shape[0], k.shape[0], k.shape[1]
    rows = Q_PER_KV * WINDOW

    def body(sink_ref, q_ref, kp_ref, kc_ref, vp_ref, vc_ref, o_ref, do_ref,
             dq_ref, dkc_ref, dkp_ref, dvc_ref, dvp_ref, ds_ref):
        n = pl.program_id(0)

        @pl.when(n == 0)
        def _():
            ds_ref[...] = jnp.zeros_like(ds_ref)

        for g in range(n_kv):
            hs = slice(g * Q_PER_KV, (g + 1) * Q_PER_KV)
            qv = q_ref[hs].reshape(rows, HEAD_DIM)
            kv = jnp.concatenate([kp_ref[g], kc_ref[g]], axis=0)
            vv = jnp.concatenate([vp_ref[g], vc_ref[g]], axis=0)
            p, l, es = _attn_probs(qv, kv, sink_ref, g, n)
            inv_l = 1.0 / l
            pn = p * inv_l
            dov = do_ref[hs].reshape(rows, HEAD_DIM)
            delta = jnp.sum(dov * o_ref[hs].reshape(rows, HEAD_DIM), axis=-1, keepdims=True)
            dob = dov.astype(BF16)
            dv = lax.dot_general(pn.astype(BF16), dob, _TN, preferred_element_type=F32)
            dp = lax.dot_general(dob, vv, _NT, preferred_element_type=F32)
            dsb = (pn * (dp - delta)).astype(BF16)
            dq_ref[hs] = jnp.dot(dsb, kv, preferred_element_type=F32).reshape(Q_PER_KV, WINDOW, HEAD_DIM)
            dk = lax.dot_general(dsb, qv, _TN, preferred_element_type=F32)
            dkp_ref[g] = dk[:WINDOW]
            dkc_ref[g] = dk[WINDOW:]
            dvp_ref[g] = dv[:WINDOW]
            dvc_ref[g] = dv[WINDOW:]
            ds_ref[g] += -(es * inv_l) * delta

    qspec, cur, prev = _attn_specs(n_q, n_kv)
    sspec = pl.BlockSpec((n_kv, rows, 1), lambda n: (0, 0, 0))
    kshape = jax.ShapeDtypeStruct(k.shape, F32)
    return _call(
        body, "attn_bwd", (t // WINDOW,),
        [pl.BlockSpec(memory_space=pltpu.SMEM), qspec, prev, cur, prev, cur, qspec, qspec],
        [qspec, cur, cur, cur, cur, sspec],
        [jax.ShapeDtypeStruct(q.shape, F32), kshape, kshape, kshape, kshape,
         jax.ShapeDtypeStruct((n_kv, rows, 1), F32)],
        [], ("arbitrary",), (sinks, q, k, k, v, v, o, do), comm)


def _cmul(ar, ai, br, bi):
    return ar * br - ai * bi, ar * bi + ai * br


def _scan_consts(ar, ai, half, reverse):
    row = lax.broadcasted_iota(jnp.int32, (SUBLANES, half), 0)
    a2 = _cmul(ar, ai, ar, ai)
    a4 = _cmul(*a2, *a2)
    steps = [(1, ar, ai), (2, *a2), (4, *a4)]
    pr, pi = ar, ai
    pwr = jnp.zeros((SUBLANES, half), F32)
    pwi = jnp.zeros((SUBLANES, half), F32)
    for r in range(SUBLANES):
        sel = row == (SUBLANES - 1 - r if reverse else r)
        pwr = jnp.where(sel, pr, pwr)
        pwi = jnp.where(sel, pi, pwi)
        pr, pi = _cmul(pr, pi, ar, ai)
    return row, steps, pwr, pwi


def _scan8(xr, xi, row, steps, pwr, pwi, cr, ci, reverse):
    for d, er, ei in steps:
        if reverse:
            keep, shift = row < SUBLANES - d, SUBLANES - d
        else:
            keep, shift = row >= d, d
        sr = jnp.where(keep, pltpu.roll(xr, shift, 0), 0.0)
        si = jnp.where(keep, pltpu.roll(xi, shift, 0), 0.0)
        tr, ti = _cmul(er, ei, sr, si)
        xr, xi = xr + tr, xi + ti
    tr, ti = _cmul(pwr, pwi, cr, ci)
    return xr + tr, xi + ti


def _ssm_fwd(proj, ucb0, bexp, cexp, a_cat, d_skip, tt=512, comm=None):
    t = proj.shape[0]
    ngb, cw, two_l = bexp.shape
    half = two_l // 2
    tt = _tile(t, tt, SUBLANES * 2)
    nt = t // tt

    def body(u_ref, b_ref, c_ref, a_ref, d_ref, y_ref, z_ref, st_ref, carry_ref):
        @pl.when(pl.program_id(1) == 0)
        def _():
            carry_ref[...] = jnp.zeros_like(carry_ref)

        u = u_ref[...]
        st_ref[...] = jnp.dot(u.astype(BF16), b_ref[...], preferred_element_type=F32)
        ar, ai = a_ref[:, :half], a_ref[:, half:]
        row, steps, pwr, pwi = _scan_consts(ar, ai, half, False)

        def tile(i, carry):
            base = pl.multiple_of(i * SUBLANES, SUBLANES)
            xr, xi = _scan8(st_ref[pl.ds(base, SUBLANES), :half], st_ref[pl.ds(base, SUBLANES), half:],
                            row, steps, pwr, pwi, carry[0], carry[1], False)
            st_ref[pl.ds(base, SUBLANES), :half] = xr
            st_ref[pl.ds(base, SUBLANES), half:] = xi
            return xr[SUBLANES - 1:, :], xi[SUBLANES - 1:, :]

        cr, ci = lax.fori_loop(0, tt // SUBLANES, tile, (carry_ref[0:1, :half], carry_ref[0:1, half:]),
                               unroll=SCAN_UNROLL)
        carry_ref[0:1, :half] = cr
        carry_ref[0:1, half:] = ci
        y = jnp.dot(st_ref[...].astype(BF16), c_ref[...], preferred_element_type=F32) + d_ref[...] * u
        y_ref[...] = y
        z_ref[...] = _gelu(y).astype(BF16)

    d_ssm = ngb * cw
    return _call(
        body, "ssm_fwd", (ngb, nt),
        [pl.BlockSpec((tt, cw), lambda g, i: (i, ucb0 + g)),
         pl.BlockSpec((None, cw, two_l), lambda g, i: (g, 0, 0)),
         pl.BlockSpec((None, two_l, cw), lambda g, i: (g, 0, 0)),
         pl.BlockSpec((None, 1, two_l), lambda g, i: (g, 0, 0)),
         pl.BlockSpec((1, cw), lambda g, i: (0, g))],
        [pl.BlockSpec((tt, cw), lambda g, i: (i, g)),
         pl.BlockSpec((tt, cw), lambda g, i: (i, g)),
         pl.BlockSpec((tt, two_l), lambda g, i: (i, g))],
        [jax.ShapeDtypeStruct((t, d_ssm), F32), jax.ShapeDtypeStruct((t, d_ssm), BF16),
         jax.ShapeDtypeStruct((t, ngb * two_l), F32)],
        [pltpu.VMEM((SUBLANES, two_l), F32)], ("parallel", "arbitrary"),
        (proj, bexp, cexp, a_cat, d_skip), comm)


def _ssm_bwd(dy0, proj, ucb0, states, bexp, cexp, a_cat, d_skip, tt=512, comm=None):
    t = dy0.shape[0]
    ngb, cw, two_l = bexp.shape
    half = two_l // 2
    tt = _tile(t, tt, SUBLANES * 2)
    nt = t // tt

    def body(dy_ref, u_ref, st_ref, b_ref, c_ref, a_ref, d_ref,
             du_ref, db_ref, dc_ref, da_ref, dd_ref, lam_ref, carry_ref, acc_ref):
        step = pl.program_id(1)

        @pl.when(step == 0)
        def _():
            carry_ref[...] = jnp.zeros_like(carry_ref)
            acc_ref[...] = jnp.zeros_like(acc_ref)
            db_ref[...] = jnp.zeros_like(db_ref)
            dc_ref[...] = jnp.zeros_like(dc_ref)
            dd_ref[...] = jnp.zeros_like(dd_ref)

        dy, u = dy_ref[...], u_ref[...]
        dyb = dy.astype(BF16)
        lam_ref[...] = lax.dot_general(dyb, c_ref[...], _NT, preferred_element_type=F32)
        ar, ai = a_ref[:, :half], -a_ref[:, half:]
        row, steps, pwr, pwi = _scan_consts(ar, ai, half, True)
        last = row == SUBLANES - 1

        def tile(i, carry):
            cr, ci, accr, acci = carry
            base = pl.multiple_of((tt // SUBLANES - 1 - i) * SUBLANES, SUBLANES)
            xr, xi = _scan8(lam_ref[pl.ds(base, SUBLANES), :half], lam_ref[pl.ds(base, SUBLANES), half:],
                            row, steps, pwr, pwi, cr, ci, True)
            lam_ref[pl.ds(base, SUBLANES), :half] = xr
            lam_ref[pl.ds(base, SUBLANES), half:] = xi
            nr = jnp.where(last, cr, pltpu.roll(xr, SUBLANES - 1, 0))
            ni = jnp.where(last, ci, pltpu.roll(xi, SUBLANES - 1, 0))
            sr, si = st_ref[pl.ds(base, SUBLANES), :half], st_ref[pl.ds(base, SUBLANES), half:]
            return xr[0:1, :], xi[0:1, :], accr + sr * nr + si * ni, acci + sr * ni - si * nr

        cr, ci, accr, acci = lax.fori_loop(
            0, tt // SUBLANES, tile,
            (carry_ref[0:1, :half], carry_ref[0:1, half:], acc_ref[:, :half], acc_ref[:, half:]),
            unroll=SCAN_UNROLL)
        carry_ref[0:1, :half] = cr
        carry_ref[0:1, half:] = ci
        acc_ref[:, :half] = accr
        acc_ref[:, half:] = acci
        lamb = lam_ref[...].astype(BF16)
        du = lax.dot_general(lamb, b_ref[...], _NT, preferred_element_type=F32) + d_ref[...] * dy
        du_ref[...] = du.astype(BF16)
        db_ref[...] += lax.dot_general(u.astype(BF16), lamb, _TN, preferred_element_type=F32)
        dc_ref[...] += lax.dot_general(st_ref[...].astype(BF16), dyb, _TN, preferred_element_type=F32)
        dd_ref[...] += _colsum(dy * u)

        @pl.when(step == nt - 1)
        def _():
            da_ref[...] = _colsum(acc_ref[...])

    d_ssm = ngb * cw
    return _call(
        body, "ssm_bwd", (ngb, nt),
        [pl.BlockSpec((tt, cw), lambda g, i: (nt - 1 - i, g)),
         pl.BlockSpec((tt, cw), lambda g, i: (nt - 1 - i, ucb0 + g)),
         pl.BlockSpec((tt, two_l), lambda g, i: (nt - 1 - i, g)),
         pl.BlockSpec((None, cw, two_l), lambda g, i: (g, 0, 0)),
         pl.BlockSpec((None, two_l, cw), lambda g, i: (g, 0, 0)),
         pl.BlockSpec((None, 1, two_l), lambda g, i: (g, 0, 0)),
         pl.BlockSpec((1, cw), lambda g, i: (0, g))],
        [pl.BlockSpec((tt, cw), lambda g, i: (nt - 1 - i, g)),
         pl.BlockSpec((None, cw, two_l), lambda g, i: (g, 0, 0)),
         pl.BlockSpec((None, two_l, cw), lambda g, i: (g, 0, 0)),
         pl.BlockSpec((None, 1, two_l), lambda g, i: (g, 0, 0)),
         pl.BlockSpec((1, cw), lambda g, i: (0, g))],
        [jax.ShapeDtypeStruct((t, d_ssm), BF16),
         jax.ShapeDtypeStruct((ngb, cw, two_l), F32),
         jax.ShapeDtypeStruct((ngb, two_l, cw), F32),
         jax.ShapeDtypeStruct((ngb, 1, two_l), F32),
         jax.ShapeDtypeStruct((1, d_ssm), F32)],
        [pltpu.VMEM((tt, two_l), F32), pltpu.VMEM((SUBLANES, two_l), F32), pltpu.VMEM((SUBLANES, two_l), F32)],
        ("parallel", "arbitrary"), (dy0, proj, states, bexp, cexp, a_cat, d_skip), comm)


def _zoh(lr, li, ls):
    step = jnp.exp(ls)
    e = jnp.exp(lr * step)
    ar, ai = e * jnp.cos(li * step), e * jnp.sin(li * step)
    den = lr * lr + li * li
    cr = ((ar - 1.0) * lr + ai * li) / den
    ci = (ai * lr - (ar - 1.0) * li) / den
    return step, ar, ai, den, cr, ci


def _ssm_param_fwd(lr, li, ls, br, bi):
    def body(lr_ref, li_ref, ls_ref, br_ref, bi_ref, ar_ref, ai_ref, bbr_ref, bbi_ref):
        _, ar, ai, _, cr, ci = _zoh(lr_ref[...], li_ref[...], ls_ref[...])
        ar_ref[...] = ar
        ai_ref[...] = ai
        bbr, bbi = _cmul(cr, ci, br_ref[...], bi_ref[...])
        bbr_ref[...] = bbr
        bbi_ref[...] = bbi

    small, big = jax.ShapeDtypeStruct(lr.shape, F32), jax.ShapeDtypeStruct(br.shape, F32)
    return _pcall(body, name="ssm_param_fwd", out_shape=[small, small, big, big],
                  compiler_params=_params())(lr, li, ls, br, bi)


def _ssm_param_bwd(lr, li, ls, br, bi, gar, gai, gbr, gbi):
    def body(lr_ref, li_ref, ls_ref, br_ref, bi_ref, gar_ref, gai_ref, gbr_ref, gbi_ref,
             dlr_ref, dli_ref, dls_ref, dbr_ref, dbi_ref):
        lrv, liv = lr_ref[...], li_ref[...]
        step, ar, ai, den, cr, ci = _zoh(lrv, liv, ls_ref[...])
        brv, biv, gr, gi = br_ref[...], bi_ref[...], gbr_ref[...], gbi_ref[...]
        dbr_ref[...] = cr * gr + ci * gi
        dbi_ref[...] = cr * gi - ci * gr
        gcr = jnp.sum(brv * gr + biv * gi, axis=1, keepdims=True)
        gci = jnp.sum(brv * gi - biv * gr, axis=1, keepdims=True)
        gtr = gar_ref[...] + (lrv * gcr - liv * gci) / den
        gti = gai_ref[...] + (lrv * gci + liv * gcr) / den
        qr = (cr * lrv + ci * liv) / den
        qi = (ci * lrv - cr * liv) / den
        gzr = ar * gtr + ai * gti
        gzi = ar * gti - ai * gtr
        dlr_ref[...] = step * gzr - (qr * gcr + qi * gci)
        dli_ref[...] = step * gzi - (qr * gci - qi * gcr)
        gstep = jnp.sum(lrv * gzr + liv * gzi, axis=2, keepdims=True)
        dls_ref[...] = jnp.broadcast_to(step * gstep, step.shape)

    small, big = jax.ShapeDtypeStruct(lr.shape, F32), jax.ShapeDtypeStruct(br.shape, F32)
    return _pcall(body, name="ssm_param_bwd", out_shape=[small, small, small, big, big],
                  compiler_params=_params())(lr, li, ls, br, bi, gar, gai, gbr, gbi)


def _block_diag_in(bb):
    g, h, p = bb.shape
    nb, n = g // GROUPS_PER_BLOCK, GROUPS_PER_BLOCK
    b4 = bb.reshape(nb, n, h, p)
    rows = [jnp.pad(b4[:, k], ((0, 0), (0, 0), (k * p, (n - 1 - k) * p))) for k in range(n)]
    return jnp.concatenate(rows, axis=1)


def _block_diag_take(e, h, p):
    nb, n = e.shape[0], GROUPS_PER_BLOCK
    d = jnp.stack([e[:, k * h:(k + 1) * h, k * p:(k + 1) * p] for k in range(n)], axis=1)
    return d.reshape(nb * n, h, p)


def _ada_fwd(c_all, w_sh, b_sh, tn=512):
    bsz, d = c_all.shape
    nsh = w_sh.shape[1]
    tn = _tile(nsh, tn, LANES)

    def body(c_ref, w_ref, b_ref, mod_ref, act_ref):
        act = c_ref[...] * jax.nn.sigmoid(c_ref[...])
        act_ref[...] = act
        mod_ref[...] = jnp.dot(act.astype(BF16), w_ref[...].astype(BF16), preferred_element_type=F32) + b_ref[...]

    return _pcall(
        body, name="ada_fwd", grid=(nsh // tn,),
        in_specs=[pl.BlockSpec((bsz, d), lambda j: (0, 0)), pl.BlockSpec((d, tn), lambda j: (0, j)),
                  pl.BlockSpec((1, tn), lambda j: (0, j))],
        out_specs=[pl.BlockSpec((bsz, tn), lambda j: (0, j)), pl.BlockSpec((bsz, d), lambda j: (0, 0))],
        out_shape=[jax.ShapeDtypeStruct((bsz, nsh), F32), jax.ShapeDtypeStruct((bsz, d), F32)],
        compiler_params=_params(("arbitrary",)),
    )(c_all, w_sh, b_sh)


def _adamw(w, g, m, v, name, outer=None):
    r, c = w.shape
    tr = _tile(r, max(SUBLANES, (256 * 1024) // c // SUBLANES * SUBLANES), SUBLANES)
    c1, c2 = 1.0 / (1.0 - ADAM_B1 ** ADAM_STEP), 1.0 / (1.0 - ADAM_B2 ** ADAM_STEP)
    n_g = 1 if outer is None else 2

    def body(w_ref, m_ref, v_ref, *rest):
        g_refs, (go_ref, d_ref, nm_ref, nv_ref) = rest[:n_g], rest[n_g:]
        if outer is None:
            gv = g_refs[0][...]
        else:
            gv = jnp.dot(g_refs[0][...], g_refs[1][...], precision=lax.Precision.HIGHEST,
                         preferred_element_type=F32)
        nm = ADAM_B1 * m_ref[...] + (1.0 - ADAM_B1) * gv
        nv = ADAM_B2 * v_ref[...] + (1.0 - ADAM_B2) * (gv * gv)
        go_ref[...] = gv
        nm_ref[...] = nm
        nv_ref[...] = nv
        d_ref[...] = -ADAM_LR * ((nm * c1) / (jnp.sqrt(nv * c2) + ADAM_EPS) + ADAM_WD * w_ref[...])

    spec = pl.BlockSpec((tr, c), lambda i: (i, 0))
    if outer is None:
        g_specs, g_ops = [spec], (g,)
    else:
        a, b = outer
        g_specs = [pl.BlockSpec((tr, a.shape[1]), lambda i: (i, 0)), pl.BlockSpec(b.shape, lambda i: (0, 0))]
        g_ops = (a, b)
    shp = jax.ShapeDtypeStruct((r, c), F32)
    res, _ = _call(body, name, (r // tr,), [spec] * 3 + g_specs, [spec] * 4, [shp] * 4, [], ("parallel",),
                   (w, m, v, *g_ops))
    return res


def _sum_leading(arr, out_dtype, name):
    n, r, c = arr.shape
    tr = _tile(r, max(SUBLANES * 2, (512 * 1024) // (c * n) // (SUBLANES * 2) * (SUBLANES * 2)), SUBLANES * 2)

    def body(x_ref, o_ref):
        acc = x_ref[0].astype(F32)
        for k in range(1, n):
            acc = acc + x_ref[k].astype(F32)
        o_ref[...] = acc.astype(out_dtype)

    return _pcall(body, name=name, grid=(r // tr,),
                  in_specs=[pl.BlockSpec((n, tr, c), lambda i: (0, i, 0))],
                  out_specs=pl.BlockSpec((tr, c), lambda i: (i, 0)),
                  out_shape=jax.ShapeDtypeStruct((r, c), out_dtype),
                  compiler_params=_params(("parallel",)))(arr)


def _place():
    x, y, c = lax.axis_index("x"), lax.axis_index("y"), lax.axis_index("c")
    chips = [(1 - x, y), (x, 1 - y), (1 - x, 1 - y)]
    return x, y, c, chips


def _allgather8(v, name):
    m, n = v.shape

    def body(x_ref, out_ref, send_sems, recv_sems, local_sem):
        x, y, c, chips = _place()
        me, sibling = (x, y, c), (x, y, 1 - c)

        def slot(px, py, pc):
            return out_ref.at[4 * px + 2 * py + pc]

        def copy(k, block, to, src=None):
            return pltpu.make_async_remote_copy(
                src_ref=slot(*block) if src is None else src, dst_ref=slot(*block),
                send_sem=send_sems.at[k], recv_sem=recv_sems.at[k], device_id=to, device_id_type=MESH)

        mine = pltpu.make_async_copy(x_ref, slot(*me), local_sem)
        mine.start()
        first = [copy(0, me, sibling, src=x_ref)]
        first += [copy(1 + j, me, (*chip, c), src=x_ref) for j, chip in enumerate(chips)]
        for cp in first:
            cp.start()
        passed = [copy(4 + j, (*chip, c), sibling) for j, chip in enumerate(chips)]
        for j, chip in enumerate(chips):
            copy(1 + j, (*chip, c), me).wait_recv()
            passed[j].start()
        copy(0, sibling, me).wait_recv()
        for j, chip in enumerate(chips):
            copy(4 + j, (*chip, 1 - c), me).wait_recv()
        for cp in first + passed:
            cp.wait_send()
        mine.wait()

    return _pcall(
        body, name=name, out_shape=jax.ShapeDtypeStruct((N_DEV, m, n), v.dtype),
        in_specs=[pl.BlockSpec(memory_space=pltpu.VMEM)], out_specs=pl.BlockSpec(memory_space=pltpu.VMEM),
        scratch_shapes=[pltpu.SemaphoreType.DMA((7,)), pltpu.SemaphoreType.DMA((7,)), pltpu.SemaphoreType.DMA],
        compiler_params=_params(),
    )(v)


def _scalars(*vals):
    return jnp.stack([jnp.asarray(v, jnp.int32) for v in vals])


def _cast_place(w, name, comm=None):
    r, cdim = w.shape
    tr = _tile(r, max(SUBLANES * 2, (512 * 1024) // cdim // (SUBLANES * 2) * (SUBLANES * 2)), SUBLANES * 2)

    def body(w_ref, o_ref):
        o_ref[...] = w_ref[...].astype(BF16)

    (out,), carried = _call(
        body, name, (r // tr,), [pl.BlockSpec((tr, cdim), lambda i: (i, 0))],
        [pl.BlockSpec((None, tr, cdim), lambda i: (2 * lax.axis_index("x") + lax.axis_index("y"), i, 0))],
        [jax.ShapeDtypeStruct((N_CHIPS, r, cdim), BF16)], [], ("parallel",), (w,), comm)
    return out, carried


def _gather_weights(bufs):
    nw = len(bufs)

    def body(*refs):
        outs = refs[nw:2 * nw]
        send_sems, recv_sems = refs[2 * nw:]
        x, y, c, chips = _place()
        me, sibling = (x, y, c), (x, y, 1 - c)

        def copy(w, k, chip, hc, to):
            h = outs[w].shape[1] // 2
            ref = outs[w].at[2 * chip[0] + chip[1], pl.ds(pl.multiple_of(hc * h, SUBLANES * 2), h)]
            return pltpu.make_async_remote_copy(
                src_ref=ref, dst_ref=ref, send_sem=send_sems.at[w, k], recv_sem=recv_sems.at[w, k],
                device_id=to, device_id_type=MESH)

        sent = []
        for w in range(nw):
            for k, chip in enumerate(chips):
                sent.append(copy(w, k, (x, y), c, (*chip, c)))
                sent[-1].start()
        for w in range(nw):
            for k, chip in enumerate(chips):
                copy(w, k, chip, c, me).wait_recv()
                sent.append(copy(w, 3 + k, chip, c, sibling))
                sent[-1].start()
        for w in range(nw):
            for k, chip in enumerate(chips):
                copy(w, 3 + k, chip, 1 - c, me).wait_recv()
        for cp in sent:
            cp.wait_send()

    any_spec = pl.BlockSpec(memory_space=pl.ANY)
    return _pcall(
        body, name="gather_weights",
        out_shape=[jax.ShapeDtypeStruct(b.shape, b.dtype) for b in bufs],
        in_specs=[any_spec] * nw, out_specs=[any_spec] * nw,
        input_output_aliases={w: w for w in range(nw)},
        scratch_shapes=[pltpu.SemaphoreType.DMA((nw, 6)), pltpu.SemaphoreType.DMA((nw, 6))],
        compiler_params=_params(),
    )(*bufs)


class _Comm:
    def __init__(self, ins, outs, aliases, n_sems, start, finish):
        self.ins, self.outs, self.aliases, self.n_sems = ins, outs, aliases, n_sems
        self.start, self.finish = start, finish


def _comm_gather_ici(bufs, spans=None, forwards=None):
    nw = len(bufs)
    spans = spans or [(0, 1, 1)] * nw
    forwards = forwards or [None] * nw
    per = 2 * (N_CHIPS - 1)

    def rows(out, span, hc):
        lo, count, n = span
        unit = out.shape[1] // 2 // n
        return pl.ds(pl.multiple_of((hc * n + lo) * unit, SUBLANES * 2), unit * count)

    def copies(outs, send_sems, recv_sems, incoming):
        x, y, c, chips = _place()
        res = []
        for w, out in enumerate(outs):
            for k, chip in enumerate(chips):
                if spans[w] is not None:
                    blk = chip if incoming else (x, y)
                    ref = out.at[2 * blk[0] + blk[1], rows(out, spans[w], c)]
                    res.append(pltpu.make_async_remote_copy(
                        src_ref=ref, dst_ref=ref, send_sem=send_sems.at[w * per + k],
                        recv_sem=recv_sems.at[w * per + k], device_id=(*chip, c), device_id_type=MESH))
                if forwards[w] is not None:
                    ref = out.at[2 * chip[0] + chip[1], rows(out, forwards[w], 1 - c if incoming else c)]
                    res.append(pltpu.make_async_remote_copy(
                        src_ref=ref, dst_ref=ref, send_sem=send_sems.at[w * per + N_CHIPS - 1 + k],
                        recv_sem=recv_sems.at[w * per + N_CHIPS - 1 + k], device_id=(x, y, 1 - c),
                        device_id_type=MESH))
        return res

    def start(ci, co, send_sems, recv_sems):
        for cp in copies(co, send_sems, recv_sems, False):
            cp.start()

    def finish(ci, co, send_sems, recv_sems):
        for cp in copies(co, send_sems, recv_sems, True):
            cp.wait_recv()
        for cp in copies(co, send_sems, recv_sems, False):
            cp.wait_send()

    return _Comm(list(bufs), [jax.ShapeDtypeStruct(b.shape, b.dtype) for b in bufs],
                 {w: w for w in range(nw)}, per * nw, start, finish)


def _forward_halves(bufs, name, spans=None):
    nw = len(bufs)
    spans = spans or [(0, 1, 1)] * nw

    def body(*refs):
        outs = refs[nw:2 * nw]
        send_sems, recv_sems = refs[2 * nw:]
        x, y, c, chips = _place()

        def copy(w, k, hc):
            chip = chips[k]
            lo, count, n = spans[w]
            unit = outs[w].shape[1] // 2 // n
            ref = outs[w].at[2 * chip[0] + chip[1],
                             pl.ds(pl.multiple_of((hc * n + lo) * unit, SUBLANES * 2), unit * count)]
            return pltpu.make_async_remote_copy(
                src_ref=ref, dst_ref=ref, send_sem=send_sems.at[w, k], recv_sem=recv_sems.at[w, k],
                device_id=(x, y, 1 - c), device_id_type=MESH)

        pairs = [(w, k) for w in range(nw) for k in range(len(chips))]
        for w, k in pairs:
            copy(w, k, c).start()
        for w, k in pairs:
            copy(w, k, 1 - c).wait_recv()
        for w, k in pairs:
            copy(w, k, c).wait_send()

    any_spec = pl.BlockSpec(memory_space=pl.ANY)
    return _pcall(
        body, name=name,
        out_shape=[jax.ShapeDtypeStruct(b.shape, b.dtype) for b in bufs],
        in_specs=[any_spec] * nw, out_specs=[any_spec] * nw,
        input_output_aliases={w: w for w in range(nw)},
        scratch_shapes=[pltpu.SemaphoreType.DMA((nw, N_CHIPS - 1)), pltpu.SemaphoreType.DMA((nw, N_CHIPS - 1))],
        compiler_params=_params(),
    )(*bufs)


def _comm_chip_exchange(psums, spans=None, recvs=None):
    nw = len(psums)
    spans = spans or [(0, 1, 1)] * nw
    recvs = recvs or [None] * nw
    old = [w for w in range(nw) if recvs[w] is not None]
    new = [w for w in range(nw) if recvs[w] is None]

    def copies(ci, co, send_sems, recv_sems):
        x, y, c, chips = _place()
        dsts = {w: co[i] for i, w in enumerate(old + new)}
        res = []
        for w, (lo, count, n) in enumerate(spans):
            unit = psums[w].shape[1] // n
            rows = pl.ds(lo * unit, count * unit)
            for k, chip in enumerate(chips):
                res.append(pltpu.make_async_remote_copy(
                    src_ref=ci[w].at[2 * chip[0] + chip[1], rows], dst_ref=dsts[w].at[k, rows],
                    send_sem=send_sems.at[w * len(chips) + k], recv_sem=recv_sems.at[w * len(chips) + k],
                    device_id=(*chip, c), device_id_type=MESH))
        return res

    def start(ci, co, send_sems, recv_sems):
        for cp in copies(ci, co, send_sems, recv_sems):
            cp.start()

    def finish(ci, co, send_sems, recv_sems):
        cps = copies(ci, co, send_sems, recv_sems)
        for cp in cps:
            cp.wait_recv()
        for cp in cps:
            cp.wait_send()

    shape = lambda p: jax.ShapeDtypeStruct((N_CHIPS - 1,) + p.shape[1:], p.dtype)
    return _Comm(list(psums) + [recvs[w] for w in old], [shape(psums[w]) for w in old + new],
                 {nw + i: i for i in range(len(old))}, (N_CHIPS - 1) * nw, start, finish)


class _Offset:
    class _At:
        def __init__(self, sems, base):
            self.sems, self.base = sems, base

        def __getitem__(self, k):
            return self.sems.at[self.base + k]

    def __init__(self, sems, base):
        self.at = _Offset._At(sems, base)


def _join(a, b):
    na_i, na_o = len(a.ins), len(a.outs)

    def both(fa, fb):
        def run(ci, co, send_sems, recv_sems):
            fa(ci[:na_i], co[:na_o], send_sems, recv_sems)
            fb(ci[na_i:], co[na_o:], _Offset(send_sems, a.n_sems), _Offset(recv_sems, a.n_sems))
        return run

    aliases = dict(a.aliases)
    aliases.update({na_i + i: na_o + o for i, o in b.aliases.items()})
    return _Comm(list(a.ins) + list(b.ins), list(a.outs) + list(b.outs), aliases, a.n_sems + b.n_sems,
                 both(a.start, b.start), both(a.finish, b.finish))


def _comm_pair_exchange(views):
    def copies(ci, co, send_sems, recv_sems):
        x, y, c, _ = _place()
        return [pltpu.make_async_remote_copy(
            src_ref=ci[w].at[k, 1 - c], dst_ref=co[w].at[k], send_sem=send_sems.at[w * N_CHIPS + k],
            recv_sem=recv_sems.at[w * N_CHIPS + k], device_id=(x, y, 1 - c), device_id_type=MESH)
            for w in range(len(views)) for k in range(N_CHIPS)]

    def start(ci, co, send_sems, recv_sems):
        for cp in copies(ci, co, send_sems, recv_sems):
            cp.start()

    def finish(ci, co, send_sems, recv_sems):
        cps = copies(ci, co, send_sems, recv_sems)
        for cp in cps:
            cp.wait_recv()
        for cp in cps:
            cp.wait_send()

    outs = [jax.ShapeDtypeStruct((N_CHIPS,) + v.shape[2:], v.dtype) for v in views]
    return _Comm(list(views), outs, {}, N_CHIPS * len(views), start, finish)


def _pair_exchange(grads, name):
    nw = len(grads)

    def body(*refs):
        ins, outs = refs[:nw], refs[nw:2 * nw]
        send_sems, recv_sems = refs[2 * nw:]
        x, y, c, _ = _place()
        sibling = (x, y, 1 - c)

        def copy(w, k):
            return pltpu.make_async_remote_copy(
                src_ref=ins[w].at[k, 1 - c], dst_ref=outs[w].at[k],
                send_sem=send_sems.at[w, k], recv_sem=recv_sems.at[w, k], device_id=sibling, device_id_type=MESH)

        copies = [copy(w, k) for w in range(nw) for k in range(N_CHIPS)]
        for cp in copies:
            cp.start()
        for cp in copies:
            cp.wait_recv()
        for cp in copies:
            cp.wait_send()

    any_spec = pl.BlockSpec(memory_space=pl.ANY)
    return _pcall(
        body, name=name,
        out_shape=[jax.ShapeDtypeStruct((N_CHIPS,) + g.shape[2:], g.dtype) for g in grads],
        in_specs=[any_spec] * nw, out_specs=[any_spec] * nw,
        scratch_shapes=[pltpu.SemaphoreType.DMA((nw, N_CHIPS)), pltpu.SemaphoreType.DMA((nw, N_CHIPS))],
        compiler_params=_params(),
    )(*grads)


def _pair_sum(view, recv, core, name):
    n, _, h, cdim = view.shape
    th = _tile(h, max(SUBLANES * 2, (512 * 1024) // cdim // (SUBLANES * 2) * (SUBLANES * 2)), SUBLANES * 2)

    def body(s_ref, a_ref, b_ref, o_ref):
        o_ref[...] = (a_ref[...].astype(F32) + b_ref[...].astype(F32)).astype(BF16)

    grid_spec = pltpu.PrefetchScalarGridSpec(
        num_scalar_prefetch=1, grid=(n, h // th),
        in_specs=[pl.BlockSpec((None, None, th, cdim), lambda k, i, s: (k, s[0], i, 0)),
                  pl.BlockSpec((None, th, cdim), lambda k, i, s: (k, i, 0))],
        out_specs=pl.BlockSpec((None, th, cdim), lambda k, i, s: (k, i, 0)))
    return _pcall(body, name=name, grid_spec=grid_spec, out_shape=jax.ShapeDtypeStruct((n, h, cdim), BF16),
                  compiler_params=_params(("parallel", "parallel")))(_scalars(core), view, recv)


def _chip_sum(psums, recv, chip, core, name):
    _, h, cdim = psums.shape
    th = _tile(h, max(SUBLANES * 2, (256 * 1024) // cdim // (SUBLANES * 2) * (SUBLANES * 2)), SUBLANES * 2)

    def body(chip_ref, core_ref, a_ref, b_ref, o_ref):
        acc = a_ref[...].astype(F32)
        for k in range(N_CHIPS - 1):
            acc = acc + b_ref[k].astype(F32)
        o_ref[...] = acc

    grid_spec = pltpu.PrefetchScalarGridSpec(
        num_scalar_prefetch=2, grid=(h // th,),
        in_specs=[pl.BlockSpec((None, th, cdim), lambda i, s, t: (s[0], i, 0)),
                  pl.BlockSpec((N_CHIPS - 1, th, cdim), lambda i, s, t: (0, i, 0))],
        out_specs=pl.BlockSpec((None, th, cdim), lambda i, s, t: (t[0], i, 0)))
    return _pcall(body, name=name, grid_spec=grid_spec, out_shape=jax.ShapeDtypeStruct((2, h, cdim), F32),
                  compiler_params=_params(("parallel",)))(_scalars(chip), _scalars(core), psums, recv)


def _share_halves(bufs):
    nw = len(bufs)

    def body(*refs):
        outs = refs[nw:2 * nw]
        send_sems, recv_sems = refs[2 * nw:]
        x, y, c, _ = _place()
        copies = [pltpu.make_async_remote_copy(
            src_ref=outs[w].at[c], dst_ref=outs[w].at[c], send_sem=send_sems.at[w], recv_sem=recv_sems.at[w],
            device_id=(x, y, 1 - c), device_id_type=MESH) for w in range(nw)]
        for cp in copies:
            cp.start()
        for w in range(nw):
            pltpu.make_async_remote_copy(
                src_ref=outs[w].at[1 - c], dst_ref=outs[w].at[1 - c], send_sem=send_sems.at[w],
                recv_sem=recv_sems.at[w], device_id=(x, y, 1 - c), device_id_type=MESH).wait_recv()
        for cp in copies:
            cp.wait_send()

    any_spec = pl.BlockSpec(memory_space=pl.ANY)
    return _pcall(
        body, name="grad_share_halves",
        out_shape=[jax.ShapeDtypeStruct(b.shape, b.dtype) for b in bufs],
        in_specs=[any_spec] * nw, out_specs=[any_spec] * nw,
        input_output_aliases={w: w for w in range(nw)},
        scratch_shapes=[pltpu.SemaphoreType.DMA((nw,)), pltpu.SemaphoreType.DMA((nw,))],
        compiler_params=_params(),
    )(*bufs)


def _pack(arrays):
    flat = jnp.concatenate([a.reshape(-1).astype(F32) for a in arrays])
    unit = 2 * SUBLANES * PACK_COLS
    pad = (-flat.shape[0]) % unit
    return jnp.pad(flat, (0, pad)).reshape(-1, PACK_COLS)


def _unpack(buf, shapes):
    flat, out, off = buf.reshape(-1), [], 0
    for s in shapes:
        n = math.prod(s)
        out.append(flat[off:off + n].reshape(s))
        off += n
    return out


def kernel(x, c, w_ada, b_ada, norm1_g, w_in, sinks, ssm_lam_re, ssm_lam_im, ssm_log_step, ssm_b_re, ssm_b_im, ssm_c_re, ssm_c_im, ssm_d, w_glu, b_glu, attn_out_g, ssm_out_g, w_out, norm2_g, w_ff1, w_ff2, final_g, loss_target, m_w_ada, m_b_ada, m_norm1_g, m_w_in, m_sinks, m_ssm_lam_re, m_ssm_lam_im, m_ssm_log_step, m_ssm_b_re, m_ssm_b_im, m_ssm_c_re, m_ssm_c_im, m_ssm_d, m_w_glu, m_b_glu, m_attn_out_g, m_ssm_out_g, m_w_out, m_norm2_g, m_w_ff1, m_w_ff2, m_final_g, v_w_ada, v_b_ada, v_norm1_g, v_w_in, v_sinks, v_ssm_lam_re, v_ssm_lam_im, v_ssm_log_step, v_ssm_b_re, v_ssm_b_im, v_ssm_c_re, v_ssm_c_im, v_ssm_d, v_w_glu, v_b_glu, v_attn_out_g, v_ssm_out_g, v_w_out, v_norm2_g, v_w_ff1, v_w_ff2, v_final_g):
    t, d = x.shape[1], x.shape[2]
    d_attn, d_ssm = attn_out_g.shape[1], ssm_d.shape[1]
    d_in = w_in.shape[2] * N_CHIPS
    d_kv = (d_in - d_attn - d_ssm) // 2
    n_q, n_kv = d_attn // HEAD_DIM, d_kv // HEAD_DIM
    n_grp = ssm_lam_re.shape[1]
    assert n_q == n_kv * Q_PER_KV and t % WINDOW == 0 and d_ssm == n_grp * SSM_GROUP
    assert d_kv % LANES == 0 and d_attn % d_kv == 0 and n_grp % GROUPS_PER_BLOCK == 0
    cw = GROUPS_PER_BLOCK * SSM_GROUP
    ucb0 = (d_attn + 2 * d_kv) // cw
    assert (d_attn + 2 * d_kv) % cw == 0
    xi, yi, ci = lax.axis_index("x"), lax.axis_index("y"), lax.axis_index("c")
    chip = 2 * xi + yi
    dev = 2 * chip + ci
    xs, tgt = x[0], loss_target[0]
    vec = lambda a: a.reshape(1, -1)

    n_ada = w_ada.shape[2]
    c_all = _allgather8(c.reshape(SUBLANES, d // SUBLANES), "gather_c").reshape(N_DEV, d)
    b_sh = lax.dynamic_slice_in_dim(b_ada, chip * n_ada, n_ada, axis=1)
    mod_sh, c_act = _ada_fwd(c_all, w_ada[0], b_sh)
    mod_all = _allgather8(mod_sh, "gather_mod")
    mod_me = lax.dynamic_index_in_dim(mod_all[0::2], dev, axis=1, keepdims=False)
    mod_me = mod_me.reshape(N_CHIPS * n_ada // d, 1, d)
    shift1, scale1, gate1, shift2, scale2, gate2 = [mod_me[i] for i in range(N_MOD)]

    own = {}
    own["w_in"], _ = _cast_place(w_in[0], "cast_w_in")
    own["w_glu"], _ = _cast_place(w_glu[0], "cast_w_glu")
    own["w_ff1"], (win_b,) = _cast_place(w_ff1[0], "cast_w_ff1", _comm_gather_ici([own["w_in"]], [(0, 1, 4)]))
    own["w_ff2"], (win_b,) = _cast_place(w_ff2[0], "cast_w_ff2",
                                         _comm_gather_ici([win_b], [(1, 1, 4)], [(0, 1, 4)]))
    half_view = lambda g, w: g.reshape(N_CHIPS, 2, w.shape[0] // 2, w.shape[1])

    g3 = lambda a: a.reshape(n_grp, 1, STATE)
    lr3, li3 = g3(ssm_lam_re[0]), g3(ssm_lam_im[0])
    ls3 = jnp.broadcast_to(ssm_log_step[0].reshape(n_grp, 1, 1), (n_grp, 1, STATE))
    b_re3, b_im3 = ssm_b_re[0].transpose(0, 2, 1), ssm_b_im[0].transpose(0, 2, 1)
    a_re, a_im, bb_re, bb_im = _ssm_param_fwd(lr3, li3, ls3, b_re3, b_im3)
    ngb = n_grp // GROUPS_PER_BLOCK
    a_cat = jnp.concatenate([a_re.reshape(ngb, 1, -1), a_im.reshape(ngb, 1, -1)], axis=-1)
    bexp = jnp.concatenate([_block_diag_in(bb_re.astype(BF16)), _block_diag_in(bb_im.astype(BF16))], axis=-1)
    cexp = jnp.concatenate([_block_diag_in(ssm_c_re[0].astype(BF16)), _block_diag_in(-ssm_c_im[0].astype(BF16))],
                           axis=-1).transpose(0, 2, 1)

    half = HEAD_DIM // 2
    inv_freq = ROPE_THETA ** (-jnp.arange(half, dtype=F32) / half)
    ang = jnp.arange(t, dtype=F32)[:, None] * inv_freq[None, :]
    cos = jnp.tile(jnp.cos(ang), (1, LANES // half))
    sin = jnp.tile(jnp.sin(ang), (1, LANES // half))

    h, (win_b,) = _norm_mod(xs, norm1_g, scale1, shift1, _comm_gather_ici([win_b], [(2, 2, 4)], [(1, 1, 4)]))
    own["w_out"], (win_s,) = _cast_place(w_out[0], "cast_w_out", _comm_gather_ici([win_b], [None], [(2, 2, 4)]))
    proj, (wglu_ici, wout_ici) = _matmul(h, win_s, "nn", "mm_in", [F32], b_stacked=True, tn=win_s.shape[2],
                                         comm=_comm_gather_ici([own["w_glu"], own["w_out"]]))
    qr, kr, vb = _rope_fwd(proj, cos, sin, d_attn, d_kv)
    heads = lambda a, n: a.reshape(t, n, HEAD_DIM).transpose(1, 0, 2)
    unheads = lambda a: a.transpose(1, 0, 2).reshape(t, -1)
    qh, kh, vh = heads(qr, n_q), heads(kr, n_kv), heads(vb, n_kv)
    whole = (0, 1, 1)
    oh, (wff1_ici, wglu_s, wout_s) = _attn_fwd(
        qh, kh, vh, sinks[0],
        comm=_comm_gather_ici([own["w_ff1"], wglu_ici, wout_ici], [(0, 2, 4), None, None], [None, whole, whole]))
    attn = unheads(oh)
    (y0, z, states), (wff1_ici, wff2_ici) = _ssm_fwd(
        proj, ucb0, bexp, cexp, a_cat, ssm_d,
        comm=_comm_gather_ici([wff1_ici, own["w_ff2"]], [(2, 2, 4), (0, 1, 8)], [(0, 2, 4), None]))
    wglu = wglu_s.reshape(d_ssm, d_ssm)
    wout = wout_s.reshape(d_attn + d_ssm, d)
    gl = _matmul(z, wglu, "nn", "mm_glu", [F32])
    mixed = _mix(attn, y0, gl, b_glu, attn_out_g, ssm_out_g)
    mo, (wff2_ici, wff1_s) = _matmul(
        mixed, wout, "nn", "mm_out", [F32],
        comm=_comm_gather_ici([wff2_ici, wff1_ici], [(1, 1, 8), None], [(0, 1, 8), (2, 2, 4)]))
    x1, h2 = _res_norm_mod(xs, mo, gate1, norm2_g, scale2, shift2)

    def relu2(acc):
        r = jnp.maximum(acc, 0.0)
        return acc, r * r

    (a_act, rr), (wff2_ici,) = _matmul(h2, wff1_s, "nn", "mm_ff1", [BF16, BF16], epilogue=relu2, b_stacked=True,
                                       comm=_comm_gather_ici([wff2_ici], [(2, 6, 8)], [(1, 1, 8)]))
    wff2 = _forward_halves([wff2_ici], "forward_halves_ff2", [(2, 6, 8)])[0].reshape(-1, d)
    ff = _matmul(rr, wff2, "nn", "mm_ff2", [F32])
    dx2, dff, loss_cols, dgf, dgate2 = _final(x1, ff, tgt, gate2, vec(final_g))
    loss = lax.psum(0.5 * jnp.sum(loss_cols) / d, ("x", "y", "c"))

    d_relu2 = lambda acc, av: (acc * 2.0 * jnp.maximum(av.astype(F32), 0.0),)
    v_ff2 = half_view(_matmul(rr, dff, "tn", "mm_gw_ff2", [BF16]), w_ff2[0])
    da, (x_ff2,) = _matmul(dff, wff2, "nt", "mm_dff2", [BF16], epilogue=d_relu2, extras=(a_act,),
                           comm=_comm_pair_exchange([v_ff2]))
    p_ff2 = _pair_sum(v_ff2, x_ff2, ci, "pair_sum_ff2")
    gw_ff1, (r_ff2,) = _matmul(h2, da, "tn", "mm_gw_ff1", [BF16], out_stacked=N_CHIPS,
                               comm=_comm_chip_exchange([p_ff2], [(0, 2, 4)]))
    v_ff1 = half_view(gw_ff1, w_ff1[0])
    dh2, (r_ff2, x_ff1) = _matmul(
        da, wff1_s, "nt", "mm_dff1", [F32], b_stacked=True,
        comm=_join(_comm_chip_exchange([p_ff2], [(2, 2, 4)], [r_ff2]), _comm_pair_exchange([v_ff1])))
    p_ff1 = _pair_sum(v_ff1, x_ff1, ci, "pair_sum_ff1")
    dx1, dmo, dshift2, dscale2, dg2, dgate1 = _bwd_norm2(x1, dh2, dx2, mo, norm2_g, scale2, gate1)
    dmixed, (r_ff1,) = _matmul(dmo, wout, "nt", "mm_dout", [F32], comm=_comm_chip_exchange([p_ff1], [(0, 1, 8)]))
    gw_out, (r_ff1,) = _matmul(mixed, dmo, "tn", "mm_gw_out", [BF16],
                               comm=_comm_chip_exchange([p_ff1], [(1, 1, 8)], [r_ff1]))
    dattn, dgl, dzp, dga, dgs, dbglu = _bwd_mix(dmixed, attn, y0, gl, b_glu, attn_out_g, ssm_out_g)
    d_gelu = lambda acc, dz, yv: ((acc + dz) * _gelu_grad(yv),)
    dy0 = _matmul(dgl, wglu, "nt", "mm_dglu", [F32], epilogue=d_gelu, extras=(dzp, y0))
    gw_glu = _matmul(z, dgl, "tn", "mm_gw_glu", [BF16])
    v_mix = [half_view(gw_glu, w_glu[0]), half_view(gw_out, w_out[0])]
    p_mix = [_pair_sum(v, p, ci, "pair_sum_" + n)
             for n, v, p in zip(("glu", "out"), v_mix, _pair_exchange(v_mix, "pair_exchange_mix"))]
    (du, dbexp, dcexp, da_bar, dd), (r_ff1, *r_mix) = _ssm_bwd(
        dy0, proj, ucb0, states, bexp, cexp, a_cat, ssm_d,
        comm=_join(_comm_chip_exchange([p_ff1], [(2, 2, 8)], [r_ff1]), _comm_chip_exchange(p_mix)))
    doh = heads(dattn, n_q)
    (dqh, dkc, dkp, dvc, dvp, dsink), (r_ff1,) = _attn_bwd(
        qh, kh, vh, oh, doh, sinks[0], comm=_comm_chip_exchange([p_ff1], [(4, 4, 8)], [r_ff1]))
    up = lambda a: jnp.concatenate([unheads(a)[WINDOW:], jnp.zeros((WINDOW, d_kv), F32)], axis=0)
    dq, dk, dv = _rope_bwd(unheads(dqh), unheads(dkc), up(dkp), unheads(dvc), up(dvp), cos, sin)
    dproj = jnp.concatenate([dq, dk, dv, du], axis=1)
    v_in = half_view(_matmul(h, dproj, "tn", "mm_gw_in", [BF16], out_stacked=N_CHIPS, tn=win_s.shape[2]), w_in[0])
    p_in = _pair_sum(v_in, _pair_exchange([v_in], "pair_exchange_in")[0], ci, "pair_sum_in")
    dh, (r_in,) = _matmul(dproj, win_s, "nt", "mm_din", [F32], b_stacked=True, tk=win_s.shape[2],
                          comm=_comm_chip_exchange([p_in]))
    grad_x, dshift1, dscale1, dg1 = _bwd_norm1(xs, dh, dx1, norm1_g, scale1)

    half_l = GROUPS_PER_BLOCK * STATE
    ga_re = da_bar[:, 0, :half_l].reshape(n_grp, 1, STATE)
    ga_im = da_bar[:, 0, half_l:].reshape(n_grp, 1, STATE)
    gbb_re = _block_diag_take(dbexp[:, :, :half_l], SSM_GROUP, STATE)
    gbb_im = _block_diag_take(dbexp[:, :, half_l:], SSM_GROUP, STATE)
    dcexp_t = dcexp.transpose(0, 2, 1)
    gc_re = _block_diag_take(dcexp_t[:, :, :half_l], SSM_GROUP, STATE)
    gc_im = -_block_diag_take(dcexp_t[:, :, half_l:], SSM_GROUP, STATE)
    dmod = jnp.concatenate([dshift1, dscale1, dgate1, dshift2, dscale2, dgate2], axis=1)
    dsinks = dsink.reshape(n_q, WINDOW).sum(axis=1)
    pieces = [dmod, dg1, dsinks, ga_re, ga_im, gbb_re, gbb_im, gc_re, gc_im, dd, dbglu, dga, dgs, dg2, dgf]
    gathered = _allgather8(_pack(pieces).astype(BF16), "gather_small")
    summed = _sum_leading(gathered, F32, "sum_small")
    (g_b_ada, g_norm1, g_sinks, ga_re, ga_im, gbb_re, gbb_im, g_c_re, g_c_im, g_d, g_b_glu, g_attn_g, g_ssm_g,
     g_norm2, g_final) = _unpack(summed, [p.shape for p in pieces])
    g_lr, g_li, g_ls, g_b_re3, g_b_im3 = _ssm_param_bwd(lr3, li3, ls3, b_re3, b_im3, ga_re, ga_im, gbb_re, gbb_im)
    small_grads = [
        g_b_ada, g_norm1, g_sinks.reshape(1, -1), g_lr.reshape(1, n_grp, STATE), g_li.reshape(1, n_grp, STATE),
        g_ls[:, 0, 0].reshape(1, n_grp), g_b_re3.transpose(0, 2, 1)[None], g_b_im3.transpose(0, 2, 1)[None],
        g_c_re[None], g_c_im[None], g_d, g_b_glu, g_attn_g, g_ssm_g, g_norm2, g_final.reshape(-1)]
    small_w = [b_ada, norm1_g, sinks, ssm_lam_re, ssm_lam_im, ssm_log_step, ssm_b_re, ssm_b_im, ssm_c_re,
               ssm_c_im, ssm_d, b_glu, attn_out_g, ssm_out_g, norm2_g, final_g]
    small_m = [m_b_ada, m_norm1_g, m_sinks, m_ssm_lam_re, m_ssm_lam_im, m_ssm_log_step, m_ssm_b_re, m_ssm_b_im,
               m_ssm_c_re, m_ssm_c_im, m_ssm_d, m_b_glu, m_attn_out_g, m_ssm_out_g, m_norm2_g, m_final_g]
    small_v = [v_b_ada, v_norm1_g, v_sinks, v_ssm_lam_re, v_ssm_lam_im, v_ssm_log_step, v_ssm_b_re, v_ssm_b_im,
               v_ssm_c_re, v_ssm_c_im, v_ssm_d, v_b_glu, v_attn_out_g, v_ssm_out_g, v_norm2_g, v_final_g]
    small_grads = [g.reshape(w.shape) for g, w in zip(small_grads, small_w)]
    _, s_delta, s_m, s_v = _adamw(_pack(small_w), _pack(small_grads), _pack(small_m), _pack(small_v), "adamw_small")
    shapes = [w.shape for w in small_w]
    s_delta, s_m, s_v = _unpack(s_delta, shapes), _unpack(s_m, shapes), _unpack(s_v, shapes)

    dmod_rows = gathered.reshape(N_DEV, -1)[:, :dmod.shape[1]]
    dmod_sh = lax.dynamic_slice_in_dim(dmod_rows, chip * n_ada, n_ada, axis=1).astype(F32)
    upd_ada = _adamw(w_ada[0], None, m_w_ada[0], v_w_ada[0], "adamw_w_ada", outer=(c_act.T, dmod_sh))

    big_w = [w_in[0], w_glu[0], w_out[0], w_ff1[0], w_ff2[0]]
    psums = [p_in, *p_mix, p_ff1, p_ff2]
    recvd = [r_in, *r_mix, r_ff1, r_ff2]
    halves = [_chip_sum(p, r, chip, ci, f"chip_sum_{i}") for i, (p, r) in enumerate(zip(psums, recvd))]
    big_grads = [s.reshape(w.shape) for s, w in zip(_share_halves(halves), big_w)]

    big_names = ["w_in", "w_glu", "w_out", "w_ff1", "w_ff2"]
    big_m = [m_w_in[0], m_w_glu[0], m_w_out[0], m_w_ff1[0], m_w_ff2[0]]
    big_v = [v_w_in[0], v_w_glu[0], v_w_out[0], v_w_ff1[0], v_w_ff2[0]]
    big_upd = {n: _adamw(w, g, m, v, "adamw_" + n) for n, w, g, m, v in zip(big_names, big_w, big_grads, big_m, big_v)}
    big_upd["w_ada"] = upd_ada
    big_names = ["w_ada"] + big_names

    order = ["w_ada", "b_ada", "norm1_g", "w_in", "sinks", "ssm_lam_re", "ssm_lam_im", "ssm_log_step", "ssm_b_re",
             "ssm_b_im", "ssm_c_re", "ssm_c_im", "ssm_d", "w_glu", "b_glu", "attn_out_g", "ssm_out_g", "w_out",
             "norm2_g", "w_ff1", "w_ff2", "final_g"]
    small_names = [n for n in order if n not in big_names]
    grads, deltas, new_m, new_v = {}, {}, {}, {}
    for i, n in enumerate(small_names):
        grads[n], deltas[n], new_m[n], new_v[n] = small_grads[i], s_delta[i], s_m[i], s_v[i]
    for n in big_names:
        grads[n], deltas[n], new_m[n], new_v[n] = [a[None] for a in big_upd[n]]
    return (loss, grad_x[None], *[grads[n] for n in order], *[deltas[n] for n in order],
            *[new_m[n] for n in order], *[new_v[n] for n in order])
```

```python
import functools
import math

import jax
import jax.numpy as jnp
from jax import lax
from jax.experimental import pallas as pl
from jax.experimental.pallas import tpu as pltpu

F32 = jnp.float32
BF16 = jnp.bfloat16
MESH = pl.DeviceIdType.MESH

EPS = 1e-6
HEAD_DIM = 64
Q_PER_KV = 8
WINDOW = 128
ROPE_THETA = 10000.0
SSM_GROUP = 16
STATE = 64
GROUPS_PER_BLOCK = 16
SCAN_UNROLL = 2
N_MOD = 6
N_CHIPS = 4
N_DEV = 8
ADAM_LR = 0.001
ADAM_B1 = 0.9
ADAM_B2 = 0.999
ADAM_EPS = 1e-08
ADAM_WD = 0.01
ADAM_STEP = 10
LANES = 128
SUBLANES = 8
VMEM_LIMIT = 56 * 1024 * 1024
PACK_COLS = 512


def _pcall(body, **kw):
    return pl.pallas_call(body, **kw)


def _params(sem=None):
    return pltpu.CompilerParams(dimension_semantics=sem, vmem_limit_bytes=VMEM_LIMIT)


def _call(body, name, grid, in_specs, out_specs, out_shape, scratch, sem, operands, comm=None):
    if comm is None:
        res = _pcall(body, name=name, grid=grid, in_specs=in_specs, out_specs=out_specs, out_shape=out_shape,
                     scratch_shapes=scratch, compiler_params=_params(sem))(*operands)
        return tuple(res), ()
    n_in, n_out, n_ci, n_co = len(in_specs), len(out_specs), len(comm.ins), len(comm.outs)

    def carrying(*refs):
        ins, ci = refs[:n_in], refs[n_in:n_in + n_ci]
        outs = refs[n_in + n_ci:n_in + n_ci + n_out]
        co = refs[n_in + n_ci + n_out:n_in + n_ci + n_out + n_co]
        rest, send_sems, recv_sems = refs[n_in + n_ci + n_out + n_co:-2], refs[-2], refs[-1]
        ids = [pl.program_id(a) for a in range(len(grid))]
        first = functools.reduce(lambda p, q: p & q, [i == 0 for i in ids])
        last = functools.reduce(lambda p, q: p & q, [i == g - 1 for i, g in zip(ids, grid)])

        @pl.when(first)
        def _():
            comm.start(ci, co, send_sems, recv_sems)

        body(*ins, *outs, *rest)

        @pl.when(last)
        def _():
            comm.finish(ci, co, send_sems, recv_sems)

    any_spec = pl.BlockSpec(memory_space=pl.ANY)
    res = _pcall(
        carrying, name=name, grid=grid, in_specs=list(in_specs) + [any_spec] * n_ci,
        out_specs=list(out_specs) + [any_spec] * n_co, out_shape=list(out_shape) + list(comm.outs),
        input_output_aliases={n_in + ci: n_out + co for ci, co in comm.aliases.items()},
        scratch_shapes=list(scratch) + [pltpu.SemaphoreType.DMA((comm.n_sems,)), pltpu.SemaphoreType.DMA((comm.n_sems,))],
        compiler_params=_params(("arbitrary",) * len(grid)),
    )(*operands, *comm.ins)
    return tuple(res[:n_out]), tuple(res[n_out:])


def _tile(n, want, unit):
    if n <= want:
        return n
    t = (want // unit) * unit
    while t > unit and n % t:
        t -= unit
    assert n % t == 0, (n, want, unit)
    return t


_NN = (((1,), (0,)), ((), ()))
_NT = (((1,), (1,)), ((), ()))
_TN = (((0,), (0,)), ((), ()))


def _matmul(a, b, mode, name, out_dtypes, epilogue=None, extras=(), b_stacked=False, out_stacked=0,
            tm=1024, tn=1024, tk=4096, precision=None, comm=None):
    if mode == "nn":
        m, kdim = a.shape
        if b_stacked:
            s, _, nsh = b.shape
            n = s * nsh
            tn = _tile(nsh, tn, LANES)
        else:
            n = b.shape[1]
            tn = _tile(n, tn, LANES)
        tm, tk = _tile(m, tm, SUBLANES * 2), _tile(kdim, tk, LANES)
        a_spec = pl.BlockSpec((tm, tk), lambda i, j, k: (i, k))
        if b_stacked:
            npb = nsh // tn
            b_spec = pl.BlockSpec((None, tk, tn), lambda i, j, k: (j // npb, k, j % npb))
        else:
            b_spec = pl.BlockSpec((tk, tn), lambda i, j, k: (k, j))
        dims = _NN
    elif mode == "nt":
        m, kdim = a.shape
        if b_stacked:
            s, n, ksh = b.shape
            tk = _tile(ksh, tk, LANES)
            kpb = ksh // tk
            tn = _tile(n, tn, LANES)
            b_spec = pl.BlockSpec((None, tn, tk), lambda i, j, k: (k // kpb, j, k % kpb))
        else:
            n = b.shape[0]
            tk = _tile(kdim, tk, LANES)
            tn = _tile(n, tn, LANES)
            b_spec = pl.BlockSpec((tn, tk), lambda i, j, k: (j, k))
        tm = _tile(m, tm, SUBLANES * 2)
        a_spec = pl.BlockSpec((tm, tk), lambda i, j, k: (i, k))
        dims = _NT
    else:
        kdim, m = a.shape
        n = b.shape[1]
        tm = _tile(m, tm, LANES)
        tk = _tile(kdim, tk, SUBLANES * 2)
        if out_stacked:
            nsh = n // out_stacked
            tn = _tile(nsh, tn, LANES)
        else:
            tn = _tile(n, tn, LANES)
        a_spec = pl.BlockSpec((tk, tm), lambda i, j, k: (k, i))
        b_spec = pl.BlockSpec((tk, tn), lambda i, j, k: (k, j))
        dims = _TN
    nk = kdim // tk
    grid = (m // tm, n // tn, nk)
    if out_stacked:
        npo = (n // out_stacked) // tn
        o_spec = pl.BlockSpec((None, tm, tn), lambda i, j, k: (j // npo, i, j % npo))
        out_shape = [jax.ShapeDtypeStruct((out_stacked, m, n // out_stacked), dt) for dt in out_dtypes]
    else:
        o_spec = pl.BlockSpec((tm, tn), lambda i, j, k: (i, j))
        out_shape = [jax.ShapeDtypeStruct((m, n), dt) for dt in out_dtypes]
    x_spec = pl.BlockSpec((tm, tn), lambda i, j, k: (i, j))
    n_ex, n_out = len(extras), len(out_dtypes)

    def body(a_ref, b_ref, *rest):
        ex_refs, out_refs, acc_ref = rest[:n_ex], rest[n_ex:n_ex + n_out], rest[-1]
        k = pl.program_id(2)

        def finish(acc):
            outs = (acc,) if epilogue is None else epilogue(acc, *[r[...] for r in ex_refs])
            for r, o in zip(out_refs, outs):
                r[...] = o.astype(r.dtype)

        part = lax.dot_general(a_ref[...], b_ref[...], dims, precision=precision, preferred_element_type=F32)
        if nk == 1:
            finish(part)
        else:
            @pl.when(k == 0)
            def _():
                acc_ref[...] = part

            @pl.when(k > 0)
            def _():
                acc_ref[...] += part

            @pl.when(k == nk - 1)
            def _():
                finish(acc_ref[...])

    res, carried = _call(
        body, name, grid, [a_spec, b_spec] + [x_spec] * n_ex, [o_spec] * n_out, out_shape,
        [pltpu.VMEM((tm, tn) if nk > 1 else (SUBLANES, LANES), F32)], ("parallel", "parallel", "arbitrary"),
        (a, b, *extras), comm)
    main = res[0] if n_out == 1 else res
    return (main, carried) if comm else main


def _rowwise(body, name, rows, row_ins, vec_ins, row_outs, acc_outs, tr=128, comm=None):
    tr = _tile(rows, tr, SUBLANES * 2)
    n_ri, n_vi, n_ro, n_ao = len(row_ins), len(vec_ins), len(row_outs), len(acc_outs)

    def kern(*refs):
        ri, vi = refs[:n_ri], refs[n_ri:n_ri + n_vi]
        ro = refs[n_ri + n_vi:n_ri + n_vi + n_ro]
        ao = refs[n_ri + n_vi + n_ro:]

        @pl.when(pl.program_id(0) == 0)
        def _():
            for r in ao:
                r[...] = jnp.zeros_like(r)

        body(ri, vi, ro, ao)

    in_specs = [pl.BlockSpec((tr, w), functools.partial(lambda i, cb: (i, cb), cb=cb)) for _, w, cb in row_ins]
    in_specs += [pl.BlockSpec(v.shape, lambda i: (0, 0)) for v in vec_ins]
    out_specs = [pl.BlockSpec((tr, w), lambda i: (i, 0)) for w, _ in row_outs]
    out_specs += [pl.BlockSpec((1, w), lambda i: (0, 0)) for w in acc_outs]
    out_shape = [jax.ShapeDtypeStruct((rows, w), dt) for w, dt in row_outs]
    out_shape += [jax.ShapeDtypeStruct((1, w), F32) for w in acc_outs]
    res, carried = _call(kern, name, (rows // tr,), in_specs, out_specs, out_shape, [], ("arbitrary",),
                         (*[a for a, _, _ in row_ins], *vec_ins), comm)
    return (list(res), carried) if comm else list(res)


def _colsum(x):
    return jnp.sum(x, axis=0, keepdims=True)


def _rstd(x):
    return lax.rsqrt(jnp.mean(x * x, axis=-1, keepdims=True) + EPS)


def _norm_bwd(dxn, xn, r):
    return r * (dxn - xn * jnp.mean(dxn * xn, axis=-1, keepdims=True))


_SQRT_HALF = math.sqrt(0.5)
_INV_SQRT_2PI = 1.0 / math.sqrt(2.0 * math.pi)


def _gelu(y):
    return 0.5 * y * (1.0 + lax.erf(y * _SQRT_HALF))


def _gelu_grad(y):
    return 0.5 * (1.0 + lax.erf(y * _SQRT_HALF)) + y * jnp.exp(-0.5 * y * y) * _INV_SQRT_2PI


def _norm_mod(x, g, scale, shift, comm):
    def body(ri, vi, ro, ao):
        xv = ri[0][...]
        h = xv * _rstd(xv) * vi[0][...] * (1.0 + vi[1][...]) + vi[2][...]
        ro[0][...] = h.astype(BF16)

    d = x.shape[1]
    (h,), carried = _rowwise(body, "norm_mod", x.shape[0], [(x, d, 0)], [g, scale, shift], [(d, BF16)], [], comm=comm)
    return h, carried


def _res_norm_mod(x, mo, gate, g, scale, shift):
    def body(ri, vi, ro, ao):
        x1 = ri[0][...] + vi[0][...] * ri[1][...]
        ro[0][...] = x1
        ro[1][...] = (x1 * _rstd(x1) * vi[1][...] * (1.0 + vi[2][...]) + vi[3][...]).astype(BF16)

    d = x.shape[1]
    return _rowwise(body, "res_norm_mod", x.shape[0], [(x, d, 0), (mo, d, 0)], [gate, g, scale, shift],
                    [(d, F32), (d, BF16)], [])


def _mix(attn, y0, gl, b_glu, ga, gs):
    da, ds = attn.shape[1], y0.shape[1]

    def body(ri, vi, ro, ao):
        at = ri[0][...]
        z = _gelu(ri[1][...])
        o = z * jax.nn.sigmoid(ri[2][...] + vi[0][...])
        ro[0][:, :da] = (at * _rstd(at) * vi[1][...]).astype(BF16)
        ro[0][:, da:] = (o * _rstd(o) * vi[2][...]).astype(BF16)

    return _rowwise(body, "mix", attn.shape[0], [(attn, da, 0), (y0, ds, 0), (gl, ds, 0)], [b_glu, ga, gs],
                    [(da + ds, BF16)], [])[0]


def _final(x1, ff, tgt, gate2, gf):
    d = x1.shape[1]

    def body(ri, vi, ro, ao):
        ffv = ri[1][...]
        x2 = ri[0][...] + vi[0][...] * ffv
        r = _rstd(x2)
        xn = x2 * r
        e = xn * vi[1][...] - ri[2][...]
        ao[0][...] += _colsum(e * e)
        dy = e * (1.0 / d)
        ao[1][...] += _colsum(dy * xn)
        dx2 = _norm_bwd(dy * vi[1][...], xn, r)
        ao[2][...] += _colsum(dx2 * ffv)
        ro[0][...] = dx2
        ro[1][...] = (dx2 * vi[0][...]).astype(BF16)

    return _rowwise(body, "final", x1.shape[0], [(x1, d, 0), (ff, d, 0), (tgt, d, 0)], [gate2, gf],
                    [(d, F32), (d, BF16)], [d, d, d])


def _bwd_norm2(x1, dh2, dx2, mo, g2, scale2, gate1):
    d = x1.shape[1]

    def body(ri, vi, ro, ao):
        xv, dh = ri[0][...], ri[1][...]
        r = _rstd(xv)
        xn = xv * r
        ao[0][...] += _colsum(dh)
        ao[1][...] += _colsum(dh * xn * vi[0][...])
        dn = dh * (1.0 + vi[1][...])
        ao[2][...] += _colsum(dn * xn)
        dx1 = ri[2][...] + _norm_bwd(dn * vi[0][...], xn, r)
        ao[3][...] += _colsum(dx1 * ri[3][...])
        ro[0][...] = dx1
        ro[1][...] = (dx1 * vi[2][...]).astype(BF16)

    return _rowwise(body, "bwd_norm2", x1.shape[0], [(x1, d, 0), (dh2, d, 0), (dx2, d, 0), (mo, d, 0)],
                    [g2, scale2, gate1], [(d, F32), (d, BF16)], [d, d, d, d])


def _bwd_mix(dmixed, attn, y0, gl, b_glu, ga, gs):
    da, ds = attn.shape[1], y0.shape[1]

    def body(ri, vi, ro, ao):
        dan, dsn = ri[0][:, :da], ri[0][:, da:]
        at = ri[1][...]
        ra = _rstd(at)
        an = at * ra
        ao[0][...] += _colsum(dan * an)
        ro[0][...] = _norm_bwd(dan * vi[1][...], an, ra)
        z = _gelu(ri[2][...])
        sg = jax.nn.sigmoid(ri[3][...] + vi[0][...])
        o = z * sg
        rs = _rstd(o)
        on = o * rs
        ao[1][...] += _colsum(dsn * on)
        do = _norm_bwd(dsn * vi[2][...], on, rs)
        ro[2][...] = do * sg
        dgl = do * z * sg * (1.0 - sg)
        ao[2][...] += _colsum(dgl)
        ro[1][...] = dgl.astype(BF16)

    return _rowwise(body, "bwd_mix", attn.shape[0],
                    [(dmixed, da + ds, 0), (attn, da, 0), (y0, ds, 0), (gl, ds, 0)], [b_glu, ga, gs],
                    [(da, F32), (ds, BF16), (ds, F32)], [da, ds, ds])


def _bwd_norm1(x, dh, dx1, g1, scale1):
    d = x.shape[1]

    def body(ri, vi, ro, ao):
        xv, dhv = ri[0][...], ri[1][...]
        r = _rstd(xv)
        xn = xv * r
        ao[0][...] += _colsum(dhv)
        ao[1][...] += _colsum(dhv * xn * vi[0][...])
        dn = dhv * (1.0 + vi[1][...])
        ao[2][...] += _colsum(dn * xn)
        ro[0][...] = ri[2][...] + _norm_bwd(dn * vi[0][...], xn, r)

    return _rowwise(body, "bwd_norm1", x.shape[0], [(x, d, 0), (dh, d, 0), (dx1, d, 0)], [g1, scale1],
                    [(d, F32)], [d, d, d])


def _rope_apply(x, cos, sin, sign):
    first = (lax.broadcasted_iota(jnp.int32, cos.shape, 1) % HEAD_DIM) < (HEAD_DIM // 2)
    outs = []
    for j in range(x.shape[1] // LANES):
        xc = x[:, j * LANES:(j + 1) * LANES]
        rot = jnp.where(first, -pltpu.roll(xc, LANES - HEAD_DIM // 2, 1), pltpu.roll(xc, HEAD_DIM // 2, 1))
        outs.append(xc * cos + sign * (rot * sin))
    return outs


def _rope_fwd(proj, cos, sin, d_attn, d_kv):
    scale = HEAD_DIM ** -0.5
    kcb, vcb = d_attn // d_kv, d_attn // d_kv + 1

    def body(ri, vi, ro, ao):
        c, s = ri[3][...], ri[4][...]
        for j, o in enumerate(_rope_apply(ri[0][...], c, s, 1.0)):
            ro[0][:, j * LANES:(j + 1) * LANES] = (o * scale).astype(BF16)
        for j, o in enumerate(_rope_apply(ri[1][...], c, s, 1.0)):
            ro[1][:, j * LANES:(j + 1) * LANES] = o.astype(BF16)
        ro[2][...] = ri[2][...].astype(BF16)

    return _rowwise(body, "rope_fwd", proj.shape[0],
                    [(proj, d_attn, 0), (proj, d_kv, kcb), (proj, d_kv, vcb), (cos, LANES, 0), (sin, LANES, 0)], [],
                    [(d_attn, BF16), (d_kv, BF16), (d_kv, BF16)], [])


def _rope_bwd(dqr, dkc, dkp, dvc, dvp, cos, sin):
    scale = HEAD_DIM ** -0.5
    d_attn, d_kv = dqr.shape[1], dkc.shape[1]

    def body(ri, vi, ro, ao):
        c, s = ri[5][...], ri[6][...]
        for j, o in enumerate(_rope_apply(ri[0][...], c, s, -1.0)):
            ro[0][:, j * LANES:(j + 1) * LANES] = (o * scale).astype(BF16)
        for j, o in enumerate(_rope_apply(ri[1][...] + ri[2][...], c, s, -1.0)):
            ro[1][:, j * LANES:(j + 1) * LANES] = o.astype(BF16)
        ro[2][...] = (ri[3][...] + ri[4][...]).astype(BF16)

    return _rowwise(body, "rope_bwd", dqr.shape[0],
                    [(dqr, d_attn, 0), (dkc, d_kv, 0), (dkp, d_kv, 0), (dvc, d_kv, 0), (dvp, d_kv, 0),
                     (cos, LANES, 0), (sin, LANES, 0)], [],
                    [(d_attn, BF16), (d_kv, BF16), (d_kv, BF16)], [])


def _attn_probs(q, k, sink_ref, g, n):
    rows = Q_PER_KV * WINDOW
    s = lax.dot_general(q, k, _NT, preferred_element_type=F32)
    qi = lax.broadcasted_iota(jnp.int32, (rows, 2 * WINDOW), 0) % WINDOW + WINDOW
    kj = lax.broadcasted_iota(jnp.int32, (rows, 2 * WINDOW), 1)
    rel = qi - kj
    mask = (rel >= 0) & (rel < WINDOW) & ((n > 0) | (kj >= WINDOW))
    s = jnp.where(mask, s, -1e30)
    sink = jnp.concatenate([jnp.full((WINDOW, 1), sink_ref[g * Q_PER_KV + j], F32) for j in range(Q_PER_KV)], axis=0)
    m = jnp.maximum(jnp.max(s, axis=-1, keepdims=True), sink)
    p = jnp.exp(s - m)
    es = jnp.exp(sink - m)
    l = jnp.sum(p, axis=-1, keepdims=True) + es
    return p, l, es


def _attn_specs(n_q, n_kv):
    qspec = pl.BlockSpec((n_q, WINDOW, HEAD_DIM), lambda n: (0, n, 0))
    cur = pl.BlockSpec((n_kv, WINDOW, HEAD_DIM), lambda n: (0, n, 0))
    prev = pl.BlockSpec((n_kv, WINDOW, HEAD_DIM), lambda n: (0, jnp.maximum(n - 1, 0), 0))
    return qspec, cur, prev


def _attn_fwd(q, k, v, sinks, comm=None):
    n_q, n_kv, t = q.shape[0], k.shape[0], k.shape[1]
    rows = Q_PER_KV * WINDOW

    def body(sink_ref, q_ref, kp_ref, kc_ref, vp_ref, vc_ref, o_ref):
        n = pl.program_id(0)
        for g in range(n_kv):
            hs = slice(g * Q_PER_KV, (g + 1) * Q_PER_KV)
            qv = q_ref[hs].reshape(rows, HEAD_DIM)
            kv = jnp.concatenate([kp_ref[g], kc_ref[g]], axis=0)
            vv = jnp.concatenate([vp_ref[g], vc_ref[g]], axis=0)
            p, l, _ = _attn_probs(qv, kv, sink_ref, g, n)
            o = jnp.dot(p.astype(BF16), vv, preferred_element_type=F32) / l
            o_ref[hs] = o.reshape(Q_PER_KV, WINDOW, HEAD_DIM)

    qspec, cur, prev = _attn_specs(n_q, n_kv)
    (out,), carried = _call(
        body, "attn_fwd", (t // WINDOW,),
        [pl.BlockSpec(memory_space=pltpu.SMEM), qspec, prev, cur, prev, cur], [qspec],
        [jax.ShapeDtypeStruct(q.shape, F32)], [], ("arbitrary",), (sinks, q, k, k, v, v), comm)
    return out, carried


def _attn_bwd(q, k, v, o, do, sinks, comm=None):
    n_q, n_kv, t = q.shape[0], k.shape[0], k.shape[1]
    rows = Q_PER_KV * WINDOW

    def body(sink_ref, q_ref, kp_ref, kc_ref, vp_ref, vc_ref, o_ref, do_ref,
             dq_ref, dkc_ref, dkp_ref, dvc_ref, dvp_ref, ds_ref):
        n = pl.program_id(0)

        @pl.when(n == 0)
        def _():
            ds_ref[...] = jnp.zeros_like(ds_ref)

        for g in range(n_kv):
            hs = slice(g * Q_PER_KV, (g + 1) * Q_PER_KV)
            qv = q_ref[hs].reshape(rows, HEAD_DIM)
            kv = jnp.concatenate([kp_ref[g], kc_ref[g]], axis=0)
            vv = jnp.concatenate([vp_ref[g], vc_ref[g]], axis=0)
            p, l, es = _attn_probs(qv, kv, sink_ref, g, n)
            inv_l = 1.0 / l
            pn = p * inv_l
            dov = do_ref[hs].reshape(rows, HEAD_DIM)
            delta = jnp.sum(dov * o_ref[hs].reshape(rows, HEAD_DIM), axis=-1, keepdims=True)
            dob = dov.astype(BF16)
            dv = lax.dot_general(pn.astype(BF16), dob, _TN, preferred_element_type=F32)
            dp = lax.dot_general(dob, vv, _NT, preferred_element_type=F32)
            dsb = (pn * (dp - delta)).astype(BF16)
            dq_ref[hs] = jnp.dot(dsb, kv, preferred_element_type=F32).reshape(Q_PER_KV, WINDOW, HEAD_DIM)
            dk = lax.dot_general(dsb, qv, _TN, preferred_element_type=F32)
            dkp_ref[g] = dk[:WINDOW]
            dkc_ref[g] = dk[WINDOW:]
            dvp_ref[g] = dv[:WINDOW]
            dvc_ref[g] = dv[WINDOW:]
            ds_ref[g] += -(es * inv_l) * delta

    qspec, cur, prev = _attn_specs(n_q, n_kv)
    sspec = pl.BlockSpec((n_kv, rows, 1), lambda n: (0, 0, 0))
    kshape = jax.ShapeDtypeStruct(k.shape, F32)
    return _call(
        body, "attn_bwd", (t // WINDOW,),
        [pl.BlockSpec(memory_space=pltpu.SMEM), qspec, prev, cur, prev, cur, qspec, qspec],
        [qspec, cur, cur, cur, cur, sspec],
        [jax.ShapeDtypeStruct(q.shape, F32), kshape, kshape, kshape, kshape,
         jax.ShapeDtypeStruct((n_kv, rows, 1), F32)],
        [], ("arbitrary",), (sinks, q, k, k, v, v, o, do), comm)


def _cmul(ar, ai, br, bi):
    return ar * br - ai * bi, ar * bi + ai * br


def _scan_consts(ar, ai, half, reverse):
    row = lax.broadcasted_iota(jnp.int32, (SUBLANES, half), 0)
    a2 = _cmul(ar, ai, ar, ai)
    a4 = _cmul(*a2, *a2)
    steps = [(1, ar, ai), (2, *a2), (4, *a4)]
    pr, pi = ar, ai
    pwr = jnp.zeros((SUBLANES, half), F32)
    pwi = jnp.zeros((SUBLANES, half), F32)
    for r in range(SUBLANES):
        sel = row == (SUBLANES - 1 - r if reverse else r)
        pwr = jnp.where(sel, pr, pwr)
        pwi = jnp.where(sel, pi, pwi)
        pr, pi = _cmul(pr, pi, ar, ai)
    return row, steps, pwr, pwi


def _scan8(xr, xi, row, steps, pwr, pwi, cr, ci, reverse):
    for d, er, ei in steps:
        if reverse:
            keep, shift = row < SUBLANES - d, SUBLANES - d
        else:
            keep, shift = row >= d, d
        sr = jnp.where(keep, pltpu.roll(xr, shift, 0), 0.0)
        si = jnp.where(keep, pltpu.roll(xi, shift, 0), 0.0)
        tr, ti = _cmul(er, ei, sr, si)
        xr, xi = xr + tr, xi + ti
    tr, ti = _cmul(pwr, pwi, cr, ci)
    return xr + tr, xi + ti


def _ssm_fwd(proj, ucb0, bexp, cexp, a_cat, d_skip, tt=512, comm=None):
    t = proj.shape[0]
    ngb, cw, two_l = bexp.shape
    half = two_l // 2
    tt = _tile(t, tt, SUBLANES * 2)
    nt = t // tt

    def body(u_ref, b_ref, c_ref, a_ref, d_ref, y_ref, z_ref, st_ref, carry_ref):
        @pl.when(pl.program_id(1) == 0)
        def _():
            carry_ref[...] = jnp.zeros_like(carry_ref)

        u = u_ref[...]
        st_ref[...] = jnp.dot(u.astype(BF16), b_ref[...], preferred_element_type=F32)
        ar, ai = a_ref[:, :half], a_ref[:, half:]
        row, steps, pwr, pwi = _scan_consts(ar, ai, half, False)

        def tile(i, carry):
            base = pl.multiple_of(i * SUBLANES, SUBLANES)
            xr, xi = _scan8(st_ref[pl.ds(base, SUBLANES), :half], st_ref[pl.ds(base, SUBLANES), half:],
                            row, steps, pwr, pwi, carry[0], carry[1], False)
            st_ref[pl.ds(base, SUBLANES), :half] = xr
            st_ref[pl.ds(base, SUBLANES), half:] = xi
            return xr[SUBLANES - 1:, :], xi[SUBLANES - 1:, :]

        cr, ci = lax.fori_loop(0, tt // SUBLANES, tile, (carry_ref[0:1, :half], carry_ref[0:1, half:]),
                               unroll=SCAN_UNROLL)
        carry_ref[0:1, :half] = cr
        carry_ref[0:1, half:] = ci
        y = jnp.dot(st_ref[...].astype(BF16), c_ref[...], preferred_element_type=F32) + d_ref[...] * u
        y_ref[...] = y
        z_ref[...] = _gelu(y).astype(BF16)

    d_ssm = ngb * cw
    return _call(
        body, "ssm_fwd", (ngb, nt),
        [pl.BlockSpec((tt, cw), lambda g, i: (i, ucb0 + g)),
         pl.BlockSpec((None, cw, two_l), lambda g, i: (g, 0, 0)),
         pl.BlockSpec((None, two_l, cw), lambda g, i: (g, 0, 0)),
         pl.BlockSpec((None, 1, two_l), lambda g, i: (g, 0, 0)),
         pl.BlockSpec((1, cw), lambda g, i: (0, g))],
        [pl.BlockSpec((tt, cw), lambda g, i: (i, g)),
         pl.BlockSpec((tt, cw), lambda g, i: (i, g)),
         pl.BlockSpec((tt, two_l), lambda g, i: (i, g))],
        [jax.ShapeDtypeStruct((t, d_ssm), F32), jax.ShapeDtypeStruct((t, d_ssm), BF16),
         jax.ShapeDtypeStruct((t, ngb * two_l), F32)],
        [pltpu.VMEM((SUBLANES, two_l), F32)], ("parallel", "arbitrary"),
        (proj, bexp, cexp, a_cat, d_skip), comm)


def _ssm_bwd(dy0, proj, ucb0, states, bexp, cexp, a_cat, d_skip, tt=512, comm=None):
    t = dy0.shape[0]
    ngb, cw, two_l = bexp.shape
    half = two_l // 2
    tt = _tile(t, tt, SUBLANES * 2)
    nt = t // tt

    def body(dy_ref, u_ref, st_ref, b_ref, c_ref, a_ref, d_ref,
             du_ref, db_ref, dc_ref, da_ref, dd_ref, lam_ref, carry_ref, acc_ref):
        step = pl.program_id(1)

        @pl.when(step == 0)
        def _():
            carry_ref[...] = jnp.zeros_like(carry_ref)
            acc_ref[...] = jnp.zeros_like(acc_ref)
            db_ref[...] = jnp.zeros_like(db_ref)
            dc_ref[...] = jnp.zeros_like(dc_ref)
            dd_ref[...] = jnp.zeros_like(dd_ref)

        dy, u = dy_ref[...], u_ref[...]
        dyb = dy.astype(BF16)
        lam_ref[...] = lax.dot_general(dyb, c_ref[...], _NT, preferred_element_type=F32)
        ar, ai = a_ref[:, :half], -a_ref[:, half:]
        row, steps, pwr, pwi = _scan_consts(ar, ai, half, True)
        last = row == SUBLANES - 1

        def tile(i, carry):
            cr, ci, accr, acci = carry
            base = pl.multiple_of((tt // SUBLANES - 1 - i) * SUBLANES, SUBLANES)
            xr, xi = _scan8(lam_ref[pl.ds(base, SUBLANES), :half], lam_ref[pl.ds(base, SUBLANES), half:],
                            row, steps, pwr, pwi, cr, ci, True)
            lam_ref[pl.ds(base, SUBLANES), :half] = xr
            lam_ref[pl.ds(base, SUBLANES), half:] = xi
            nr = jnp.where(last, cr, pltpu.roll(xr, SUBLANES - 1, 0))
            ni = jnp.where(last, ci, pltpu.roll(xi, SUBLANES - 1, 0))
            sr, si = st_ref[pl.ds(base, SUBLANES), :half], st_ref[pl.ds(base, SUBLANES), half:]
            return xr[0:1, :], xi[0:1, :], accr + sr * nr + si * ni, acci + sr * ni - si * nr

        cr, ci, accr, acci = lax.fori_loop(
            0, tt // SUBLANES, tile,
            (carry_ref[0:1, :half], carry_ref[0:1, half:], acc_ref[:, :half], acc_ref[:, half:]),
            unroll=SCAN_UNROLL)
        carry_ref[0:1, :half] = cr
        carry_ref[0:1, half:] = ci
        acc_ref[:, :half] = accr
        acc_ref[:, half:] = acci
        lamb = lam_ref[...].astype(BF16)
        du = lax.dot_general(lamb, b_ref[...], _NT, preferred_element_type=F32) + d_ref[...] * dy
        du_ref[...] = du.astype(BF16)
        db_ref[...] += lax.dot_general(u.astype(BF16), lamb, _TN, preferred_element_type=F32)
        dc_ref[...] += lax.dot_general(st_ref[...].astype(BF16), dyb, _TN, preferred_element_type=F32)
        dd_ref[...] += _colsum(dy * u)

        @pl.when(step == nt - 1)
        def _():
            da_ref[...] = _colsum(acc_ref[...])

    d_ssm = ngb * cw
    return _call(
        body, "ssm_bwd", (ngb, nt),
        [pl.BlockSpec((tt, cw), lambda g, i: (nt - 1 - i, g)),
         pl.BlockSpec((tt, cw), lambda g, i: (nt - 1 - i, ucb0 + g)),
         pl.BlockSpec((tt, two_l), lambda g, i: (nt - 1 - i, g)),
         pl.BlockSpec((None, cw, two_l), lambda g, i: (g, 0, 0)),
         pl.BlockSpec((None, two_l, cw), lambda g, i: (g, 0, 0)),
         pl.BlockSpec((None, 1, two_l), lambda g, i: (g, 0, 0)),
         pl.BlockSpec((1, cw), lambda g, i: (0, g))],
        [pl.BlockSpec((tt, cw), lambda g, i: (nt - 1 - i, g)),
         pl.BlockSpec((None, cw, two_l), lambda g, i: (g, 0, 0)),
         pl.BlockSpec((None, two_l, cw), lambda g, i: (g, 0, 0)),
         pl.BlockSpec((None, 1, two_l), lambda g, i: (g, 0, 0)),
         pl.BlockSpec((1, cw), lambda g, i: (0, g))],
        [jax.ShapeDtypeStruct((t, d_ssm), BF16),
         jax.ShapeDtypeStruct((ngb, cw, two_l), F32),
         jax.ShapeDtypeStruct((ngb, two_l, cw), F32),
         jax.ShapeDtypeStruct((ngb, 1, two_l), F32),
         jax.ShapeDtypeStruct((1, d_ssm), F32)],
        [pltpu.VMEM((tt, two_l), F32), pltpu.VMEM((SUBLANES, two_l), F32), pltpu.VMEM((SUBLANES, two_l), F32)],
        ("parallel", "arbitrary"), (dy0, proj, states, bexp, cexp, a_cat, d_skip), comm)


def _zoh(lr, li, ls):
    step = jnp.exp(ls)
    e = jnp.exp(lr * step)
    ar, ai = e * jnp.cos(li * step), e * jnp.sin(li * step)
    den = lr * lr + li * li
    cr = ((ar - 1.0) * lr + ai * li) / den
    ci = (ai * lr - (ar - 1.0) * li) / den
    return step, ar, ai, den, cr, ci


def _ssm_param_fwd(lr, li, ls, br, bi):
    def body(lr_ref, li_ref, ls_ref, br_ref, bi_ref, ar_ref, ai_ref, bbr_ref, bbi_ref):
        _, ar, ai, _, cr, ci = _zoh(lr_ref[...], li_ref[...], ls_ref[...])
        ar_ref[...] = ar
        ai_ref[...] = ai
        bbr, bbi = _cmul(cr, ci, br_ref[...], bi_ref[...])
        bbr_ref[...] = bbr
        bbi_ref[...] = bbi

    small, big = jax.ShapeDtypeStruct(lr.shape, F32), jax.ShapeDtypeStruct(br.shape, F32)
    return _pcall(body, name="ssm_param_fwd", out_shape=[small, small, big, big],
                  compiler_params=_params())(lr, li, ls, br, bi)


def _ssm_param_bwd(lr, li, ls, br, bi, gar, gai, gbr, gbi):
    def body(lr_ref, li_ref, ls_ref, br_ref, bi_ref, gar_ref, gai_ref, gbr_ref, gbi_ref,
             dlr_ref, dli_ref, dls_ref, dbr_ref, dbi_ref):
        lrv, liv = lr_ref[...], li_ref[...]
        step, ar, ai, den, cr, ci = _zoh(lrv, liv, ls_ref[...])
        brv, biv, gr, gi = br_ref[...], bi_ref[...], gbr_ref[...], gbi_ref[...]
        dbr_ref[...] = cr * gr + ci * gi
        dbi_ref[...] = cr * gi - ci * gr
        gcr = jnp.sum(brv * gr + biv * gi, axis=1, keepdims=True)
        gci = jnp.sum(brv * gi - biv * gr, axis=1, keepdims=True)
        gtr = gar_ref[...] + (lrv * gcr - liv * gci) / den
        gti = gai_ref[...] + (lrv * gci + liv * gcr) / den
        qr = (cr * lrv + ci * liv) / den
        qi = (ci * lrv - cr * liv) / den
        gzr = ar * gtr + ai * gti
        gzi = ar * gti - ai * gtr
        dlr_ref[...] = step * gzr - (qr * gcr + qi * gci)
        dli_ref[...] = step * gzi - (qr * gci - qi * gcr)
        gstep = jnp.sum(lrv * gzr + liv * gzi, axis=2, keepdims=True)
        dls_ref[...] = jnp.broadcast_to(step * gstep, step.shape)

    small, big = jax.ShapeDtypeStruct(lr.shape, F32), jax.ShapeDtypeStruct(br.shape, F32)
    return _pcall(body, name="ssm_param_bwd", out_shape=[small, small, small, big, big],
                  compiler_params=_params())(lr, li, ls, br, bi, gar, gai, gbr, gbi)


def _block_diag_in(bb):
    g, h, p = bb.shape
    nb, n = g // GROUPS_PER_BLOCK, GROUPS_PER_BLOCK
    b4 = bb.reshape(nb, n, h, p)
    rows = [jnp.pad(b4[:, k], ((0, 0), (0, 0), (k * p, (n - 1 - k) * p))) for k in range(n)]
    return jnp.concatenate(rows, axis=1)


def _block_diag_take(e, h, p):
    nb, n = e.shape[0], GROUPS_PER_BLOCK
    d = jnp.stack([e[:, k * h:(k + 1) * h, k * p:(k + 1) * p] for k in range(n)], axis=1)
    return d.reshape(nb * n, h, p)


def _ada_fwd(c_all, w_sh, b_sh, comm, tn=512):
    bsz, d = c_all.shape
    nsh = w_sh.shape[1]
    tn = _tile(nsh, tn, LANES)

    def body(c_ref, w_ref, b_ref, mod_ref, act_ref):
        act = c_ref[...] * jax.nn.sigmoid(c_ref[...])
        act_ref[...] = act
        mod_ref[...] = jnp.dot(act.astype(BF16), w_ref[...].astype(BF16), preferred_element_type=F32) + b_ref[...]

    return _call(
        body, "ada_fwd", (nsh // tn,),
        [pl.BlockSpec((bsz, d), lambda j: (0, 0)), pl.BlockSpec((d, tn), lambda j: (0, j)),
         pl.BlockSpec((1, tn), lambda j: (0, j))],
        [pl.BlockSpec((bsz, tn), lambda j: (0, j)), pl.BlockSpec((bsz, d), lambda j: (0, 0))],
        [jax.ShapeDtypeStruct((bsz, nsh), F32), jax.ShapeDtypeStruct((bsz, d), F32)],
        [], ("arbitrary",), (c_all, w_sh, b_sh), comm)


def _adamw(w, g, m, v, name, outer=None):
    r, c = w.shape
    tr = _tile(r, max(SUBLANES, (256 * 1024) // c // SUBLANES * SUBLANES), SUBLANES)
    c1, c2 = 1.0 / (1.0 - ADAM_B1 ** ADAM_STEP), 1.0 / (1.0 - ADAM_B2 ** ADAM_STEP)
    n_g = 1 if outer is None else 2

    def body(w_ref, m_ref, v_ref, *rest):
        g_refs, (go_ref, d_ref, nm_ref, nv_ref) = rest[:n_g], rest[n_g:]
        if outer is None:
            gv = g_refs[0][...]
        else:
            gv = jnp.dot(g_refs[0][...], g_refs[1][...], precision=lax.Precision.HIGHEST,
                         preferred_element_type=F32)
        nm = ADAM_B1 * m_ref[...] + (1.0 - ADAM_B1) * gv
        nv = ADAM_B2 * v_ref[...] + (1.0 - ADAM_B2) * (gv * gv)
        go_ref[...] = gv
        nm_ref[...] = nm
        nv_ref[...] = nv
        d_ref[...] = -ADAM_LR * ((nm * c1) / (jnp.sqrt(nv * c2) + ADAM_EPS) + ADAM_WD * w_ref[...])

    spec = pl.BlockSpec((tr, c), lambda i: (i, 0))
    if outer is None:
        g_specs, g_ops = [spec], (g,)
    else:
        a, b = outer
        g_specs = [pl.BlockSpec((tr, a.shape[1]), lambda i: (i, 0)), pl.BlockSpec(b.shape, lambda i: (0, 0))]
        g_ops = (a, b)
    shp = jax.ShapeDtypeStruct((r, c), F32)
    res, _ = _call(body, name, (r // tr,), [spec] * 3 + g_specs, [spec] * 4, [shp] * 4, [], ("parallel",),
                   (w, m, v, *g_ops))
    return res


def _sum_leading(arr, out_dtype, name):
    n, r, c = arr.shape
    tr = _tile(r, max(SUBLANES * 2, (512 * 1024) // (c * n) // (SUBLANES * 2) * (SUBLANES * 2)), SUBLANES * 2)

    def body(x_ref, o_ref):
        acc = x_ref[0].astype(F32)
        for k in range(1, n):
            acc = acc + x_ref[k].astype(F32)
        o_ref[...] = acc.astype(out_dtype)

    return _pcall(body, name=name, grid=(r // tr,),
                  in_specs=[pl.BlockSpec((n, tr, c), lambda i: (0, i, 0))],
                  out_specs=pl.BlockSpec((tr, c), lambda i: (i, 0)),
                  out_shape=jax.ShapeDtypeStruct((r, c), out_dtype),
                  compiler_params=_params(("parallel",)))(arr)


def _place():
    x, y, c = lax.axis_index("x"), lax.axis_index("y"), lax.axis_index("c")
    chips = [(1 - x, y), (x, 1 - y), (1 - x, 1 - y)]
    return x, y, c, chips


def _allgather8(v, name):
    m, n = v.shape

    def body(x_ref, out_ref, send_sems, recv_sems, local_sem):
        x, y, c, chips = _place()
        me, sibling = (x, y, c), (x, y, 1 - c)

        def slot(px, py, pc):
            return out_ref.at[4 * px + 2 * py + pc]

        def copy(k, block, to, src=None):
            return pltpu.make_async_remote_copy(
                src_ref=slot(*block) if src is None else src, dst_ref=slot(*block),
                send_sem=send_sems.at[k], recv_sem=recv_sems.at[k], device_id=to, device_id_type=MESH)

        mine = pltpu.make_async_copy(x_ref, slot(*me), local_sem)
        mine.start()
        first = [copy(0, me, sibling, src=x_ref)]
        first += [copy(1 + j, me, (*chip, c), src=x_ref) for j, chip in enumerate(chips)]
        for cp in first:
            cp.start()
        passed = [copy(4 + j, (*chip, c), sibling) for j, chip in enumerate(chips)]
        for j, chip in enumerate(chips):
            copy(1 + j, (*chip, c), me).wait_recv()
            passed[j].start()
        copy(0, sibling, me).wait_recv()
        for j, chip in enumerate(chips):
            copy(4 + j, (*chip, 1 - c), me).wait_recv()
        for cp in first + passed:
            cp.wait_send()
        mine.wait()

    return _pcall(
        body, name=name, out_shape=jax.ShapeDtypeStruct((N_DEV, m, n), v.dtype),
        in_specs=[pl.BlockSpec(memory_space=pltpu.VMEM)], out_specs=pl.BlockSpec(memory_space=pltpu.VMEM),
        scratch_shapes=[pltpu.SemaphoreType.DMA((7,)), pltpu.SemaphoreType.DMA((7,)), pltpu.SemaphoreType.DMA],
        compiler_params=_params(),
    )(v)


def _scalars(*vals):
    return jnp.stack([jnp.asarray(v, jnp.int32) for v in vals])


def _cast_place(w, name, comm=None):
    r, cdim = w.shape
    tr = _tile(r, max(SUBLANES * 2, (512 * 1024) // cdim // (SUBLANES * 2) * (SUBLANES * 2)), SUBLANES * 2)

    def body(w_ref, o_ref):
        o_ref[...] = w_ref[...].astype(BF16)

    (out,), carried = _call(
        body, name, (r // tr,), [pl.BlockSpec((tr, cdim), lambda i: (i, 0))],
        [pl.BlockSpec((None, tr, cdim), lambda i: (2 * lax.axis_index("x") + lax.axis_index("y"), i, 0))],
        [jax.ShapeDtypeStruct((N_CHIPS, r, cdim), BF16)], [], ("parallel",), (w,), comm)
    return out, carried


def _gather_weights(bufs):
    nw = len(bufs)

    def body(*refs):
        outs = refs[nw:2 * nw]
        send_sems, recv_sems = refs[2 * nw:]
        x, y, c, chips = _place()
        me, sibling = (x, y, c), (x, y, 1 - c)

        def copy(w, k, chip, hc, to):
            h = outs[w].shape[1] // 2
            ref = outs[w].at[2 * chip[0] + chip[1], pl.ds(pl.multiple_of(hc * h, SUBLANES * 2), h)]
            return pltpu.make_async_remote_copy(
                src_ref=ref, dst_ref=ref, send_sem=send_sems.at[w, k], recv_sem=recv_sems.at[w, k],
                device_id=to, device_id_type=MESH)

        sent = []
        for w in range(nw):
            for k, chip in enumerate(chips):
                sent.append(copy(w, k, (x, y), c, (*chip, c)))
                sent[-1].start()
        for w in range(nw):
            for k, chip in enumerate(chips):
                copy(w, k, chip, c, me).wait_recv()
                sent.append(copy(w, 3 + k, chip, c, sibling))
                sent[-1].start()
        for w in range(nw):
            for k, chip in enumerate(chips):
                copy(w, 3 + k, chip, 1 - c, me).wait_recv()
        for cp in sent:
            cp.wait_send()

    any_spec = pl.BlockSpec(memory_space=pl.ANY)
    return _pcall(
        body, name="gather_weights",
        out_shape=[jax.ShapeDtypeStruct(b.shape, b.dtype) for b in bufs],
        in_specs=[any_spec] * nw, out_specs=[any_spec] * nw,
        input_output_aliases={w: w for w in range(nw)},
        scratch_shapes=[pltpu.SemaphoreType.DMA((nw, 6)), pltpu.SemaphoreType.DMA((nw, 6))],
        compiler_params=_params(),
    )(*bufs)


class _Comm:
    def __init__(self, ins, outs, aliases, n_sems, start, finish):
        self.ins, self.outs, self.aliases, self.n_sems = ins, outs, aliases, n_sems
        self.start, self.finish = start, finish


def _comm_gather_ici(bufs, spans=None, forwards=None):
    nw = len(bufs)
    spans = spans or [(0, 1, 1)] * nw
    forwards = forwards or [None] * nw
    per = 2 * (N_CHIPS - 1)

    def rows(out, span, hc):
        lo, count, n = span
        unit = out.shape[1] // 2 // n
        return pl.ds(pl.multiple_of((hc * n + lo) * unit, SUBLANES * 2), unit * count)

    def copies(outs, send_sems, recv_sems, incoming):
        x, y, c, chips = _place()
        res = []
        for w, out in enumerate(outs):
            for k, chip in enumerate(chips):
                if spans[w] is not None:
                    blk = chip if incoming else (x, y)
                    ref = out.at[2 * blk[0] + blk[1], rows(out, spans[w], c)]
                    res.append(pltpu.make_async_remote_copy(
                        src_ref=ref, dst_ref=ref, send_sem=send_sems.at[w * per + k],
                        recv_sem=recv_sems.at[w * per + k], device_id=(*chip, c), device_id_type=MESH))
                if forwards[w] is not None:
                    ref = out.at[2 * chip[0] + chip[1], rows(out, forwards[w], 1 - c if incoming else c)]
                    res.append(pltpu.make_async_remote_copy(
                        src_ref=ref, dst_ref=ref, send_sem=send_sems.at[w * per + N_CHIPS - 1 + k],
                        recv_sem=recv_sems.at[w * per + N_CHIPS - 1 + k], device_id=(x, y, 1 - c),
                        device_id_type=MESH))
        return res

    def start(ci, co, send_sems, recv_sems):
        for cp in copies(co, send_sems, recv_sems, False):
            cp.start()

    def finish(ci, co, send_sems, recv_sems):
        for cp in copies(co, send_sems, recv_sems, True):
            cp.wait_recv()
        for cp in copies(co, send_sems, recv_sems, False):
            cp.wait_send()

    return _Comm(list(bufs), [jax.ShapeDtypeStruct(b.shape, b.dtype) for b in bufs],
                 {w: w for w in range(nw)}, per * nw, start, finish)


def _forward_halves(bufs, name, spans=None):
    nw = len(bufs)
    spans = spans or [(0, 1, 1)] * nw

    def body(*refs):
        outs = refs[nw:2 * nw]
        send_sems, recv_sems = refs[2 * nw:]
        x, y, c, chips = _place()

        def copy(w, k, hc):
            chip = chips[k]
            lo, count, n = spans[w]
            unit = outs[w].shape[1] // 2 // n
            ref = outs[w].at[2 * chip[0] + chip[1],
                             pl.ds(pl.multiple_of((hc * n + lo) * unit, SUBLANES * 2), unit * count)]
            return pltpu.make_async_remote_copy(
                src_ref=ref, dst_ref=ref, send_sem=send_sems.at[w, k], recv_sem=recv_sems.at[w, k],
                device_id=(x, y, 1 - c), device_id_type=MESH)

        pairs = [(w, k) for w in range(nw) for k in range(len(chips))]
        for w, k in pairs:
            copy(w, k, c).start()
        for w, k in pairs:
            copy(w, k, 1 - c).wait_recv()
        for w, k in pairs:
            copy(w, k, c).wait_send()

    any_spec = pl.BlockSpec(memory_space=pl.ANY)
    return _pcall(
        body, name=name,
        out_shape=[jax.ShapeDtypeStruct(b.shape, b.dtype) for b in bufs],
        in_specs=[any_spec] * nw, out_specs=[any_spec] * nw,
        input_output_aliases={w: w for w in range(nw)},
        scratch_shapes=[pltpu.SemaphoreType.DMA((nw, N_CHIPS - 1)), pltpu.SemaphoreType.DMA((nw, N_CHIPS - 1))],
        compiler_params=_params(),
    )(*bufs)


def _comm_chip_exchange(psums, spans=None, recvs=None):
    nw = len(psums)
    spans = spans or [(0, 1, 1)] * nw
    recvs = recvs or [None] * nw
    old = [w for w in range(nw) if recvs[w] is not None]
    new = [w for w in range(nw) if recvs[w] is None]

    def copies(ci, co, send_sems, recv_sems):
        x, y, c, chips = _place()
        dsts = {w: co[i] for i, w in enumerate(old + new)}
        res = []
        for w, (lo, count, n) in enumerate(spans):
            unit = psums[w].shape[1] // n
            rows = pl.ds(lo * unit, count * unit)
            for k, chip in enumerate(chips):
                res.append(pltpu.make_async_remote_copy(
                    src_ref=ci[w].at[2 * chip[0] + chip[1], rows], dst_ref=dsts[w].at[k, rows],
                    send_sem=send_sems.at[w * len(chips) + k], recv_sem=recv_sems.at[w * len(chips) + k],
                    device_id=(*chip, c), device_id_type=MESH))
        return res

    def start(ci, co, send_sems, recv_sems):
        for cp in copies(ci, co, send_sems, recv_sems):
            cp.start()

    def finish(ci, co, send_sems, recv_sems):
        cps = copies(ci, co, send_sems, recv_sems)
        for cp in cps:
            cp.wait_recv()
        for cp in cps:
            cp.wait_send()

    shape = lambda p: jax.ShapeDtypeStruct((N_CHIPS - 1,) + p.shape[1:], p.dtype)
    return _Comm(list(psums) + [recvs[w] for w in old], [shape(psums[w]) for w in old + new],
                 {nw + i: i for i in range(len(old))}, (N_CHIPS - 1) * nw, start, finish)


class _Offset:
    class _At:
        def __init__(self, sems, base):
            self.sems, self.base = sems, base

        def __getitem__(self, k):
            return self.sems.at[self.base + k]

    def __init__(self, sems, base):
        self.at = _Offset._At(sems, base)


def _join(a, b):
    na_i, na_o = len(a.ins), len(a.outs)

    def both(fa, fb):
        def run(ci, co, send_sems, recv_sems):
            fa(ci[:na_i], co[:na_o], send_sems, recv_sems)
            fb(ci[na_i:], co[na_o:], _Offset(send_sems, a.n_sems), _Offset(recv_sems, a.n_sems))
        return run

    aliases = dict(a.aliases)
    aliases.update({na_i + i: na_o + o for i, o in b.aliases.items()})
    return _Comm(list(a.ins) + list(b.ins), list(a.outs) + list(b.outs), aliases, a.n_sems + b.n_sems,
                 both(a.start, b.start), both(a.finish, b.finish))


def _comm_pair_exchange(views):
    def copies(ci, co, send_sems, recv_sems):
        x, y, c, _ = _place()
        return [pltpu.make_async_remote_copy(
            src_ref=ci[w].at[k, 1 - c], dst_ref=co[w].at[k], send_sem=send_sems.at[w * N_CHIPS + k],
            recv_sem=recv_sems.at[w * N_CHIPS + k], device_id=(x, y, 1 - c), device_id_type=MESH)
            for w in range(len(views)) for k in range(N_CHIPS)]

    def start(ci, co, send_sems, recv_sems):
        for cp in copies(ci, co, send_sems, recv_sems):
            cp.start()

    def finish(ci, co, send_sems, recv_sems):
        cps = copies(ci, co, send_sems, recv_sems)
        for cp in cps:
            cp.wait_recv()
        for cp in cps:
            cp.wait_send()

    outs = [jax.ShapeDtypeStruct((N_CHIPS,) + v.shape[2:], v.dtype) for v in views]
    return _Comm(list(views), outs, {}, N_CHIPS * len(views), start, finish)


def _pair_exchange(grads, name):
    nw = len(grads)

    def body(*refs):
        ins, outs = refs[:nw], refs[nw:2 * nw]
        send_sems, recv_sems = refs[2 * nw:]
        x, y, c, _ = _place()
        sibling = (x, y, 1 - c)

        def copy(w, k):
            return pltpu.make_async_remote_copy(
                src_ref=ins[w].at[k, 1 - c], dst_ref=outs[w].at[k],
                send_sem=send_sems.at[w, k], recv_sem=recv_sems.at[w, k], device_id=sibling, device_id_type=MESH)

        copies = [copy(w, k) for w in range(nw) for k in range(N_CHIPS)]
        for cp in copies:
            cp.start()
        for cp in copies:
            cp.wait_recv()
        for cp in copies:
            cp.wait_send()

    any_spec = pl.BlockSpec(memory_space=pl.ANY)
    return _pcall(
        body, name=name,
        out_shape=[jax.ShapeDtypeStruct((N_CHIPS,) + g.shape[2:], g.dtype) for g in grads],
        in_specs=[any_spec] * nw, out_specs=[any_spec] * nw,
        scratch_shapes=[pltpu.SemaphoreType.DMA((nw, N_CHIPS)), pltpu.SemaphoreType.DMA((nw, N_CHIPS))],
        compiler_params=_params(),
    )(*grads)


def _pair_sum(view, recv, core, name):
    n, _, h, cdim = view.shape
    th = _tile(h, max(SUBLANES * 2, (512 * 1024) // cdim // (SUBLANES * 2) * (SUBLANES * 2)), SUBLANES * 2)

    def body(s_ref, a_ref, b_ref, o_ref):
        o_ref[...] = (a_ref[...].astype(F32) + b_ref[...].astype(F32)).astype(BF16)

    grid_spec = pltpu.PrefetchScalarGridSpec(
        num_scalar_prefetch=1, grid=(n, h // th),
        in_specs=[pl.BlockSpec((None, None, th, cdim), lambda k, i, s: (k, s[0], i, 0)),
                  pl.BlockSpec((None, th, cdim), lambda k, i, s: (k, i, 0))],
        out_specs=pl.BlockSpec((None, th, cdim), lambda k, i, s: (k, i, 0)))
    return _pcall(body, name=name, grid_spec=grid_spec, out_shape=jax.ShapeDtypeStruct((n, h, cdim), BF16),
                  compiler_params=_params(("parallel", "parallel")))(_scalars(core), view, recv)


def _chip_sum(psums, recv, chip, core, name):
    _, h, cdim = psums.shape
    th = _tile(h, max(SUBLANES * 2, (256 * 1024) // cdim // (SUBLANES * 2) * (SUBLANES * 2)), SUBLANES * 2)

    def body(chip_ref, core_ref, a_ref, b_ref, o_ref):
        acc = a_ref[...].astype(F32)
        for k in range(N_CHIPS - 1):
            acc = acc + b_ref[k].astype(F32)
        o_ref[...] = acc

    grid_spec = pltpu.PrefetchScalarGridSpec(
        num_scalar_prefetch=2, grid=(h // th,),
        in_specs=[pl.BlockSpec((None, th, cdim), lambda i, s, t: (s[0], i, 0)),
                  pl.BlockSpec((N_CHIPS - 1, th, cdim), lambda i, s, t: (0, i, 0))],
        out_specs=pl.BlockSpec((None, th, cdim), lambda i, s, t: (t[0], i, 0)))
    return _pcall(body, name=name, grid_spec=grid_spec, out_shape=jax.ShapeDtypeStruct((2, h, cdim), F32),
                  compiler_params=_params(("parallel",)))(_scalars(chip), _scalars(core), psums, recv)


def _share_halves(bufs):
    nw = len(bufs)

    def body(*refs):
        outs = refs[nw:2 * nw]
        send_sems, recv_sems = refs[2 * nw:]
        x, y, c, _ = _place()
        copies = [pltpu.make_async_remote_copy(
            src_ref=outs[w].at[c], dst_ref=outs[w].at[c], send_sem=send_sems.at[w], recv_sem=recv_sems.at[w],
            device_id=(x, y, 1 - c), device_id_type=MESH) for w in range(nw)]
        for cp in copies:
            cp.start()
        for w in range(nw):
            pltpu.make_async_remote_copy(
                src_ref=outs[w].at[1 - c], dst_ref=outs[w].at[1 - c], send_sem=send_sems.at[w],
                recv_sem=recv_sems.at[w], device_id=(x, y, 1 - c), device_id_type=MESH).wait_recv()
        for cp in copies:
            cp.wait_send()

    any_spec = pl.BlockSpec(memory_space=pl.ANY)
    return _pcall(
        body, name="grad_share_halves",
        out_shape=[jax.ShapeDtypeStruct(b.shape, b.dtype) for b in bufs],
        in_specs=[any_spec] * nw, out_specs=[any_spec] * nw,
        input_output_aliases={w: w for w in range(nw)},
        scratch_shapes=[pltpu.SemaphoreType.DMA((nw,)), pltpu.SemaphoreType.DMA((nw,))],
        compiler_params=_params(),
    )(*bufs)


def _pack(arrays):
    flat = jnp.concatenate([a.reshape(-1).astype(F32) for a in arrays])
    unit = 2 * SUBLANES * PACK_COLS
    pad = (-flat.shape[0]) % unit
    return jnp.pad(flat, (0, pad)).reshape(-1, PACK_COLS)


def _unpack(buf, shapes):
    flat, out, off = buf.reshape(-1), [], 0
    for s in shapes:
        n = math.prod(s)
        out.append(flat[off:off + n].reshape(s))
        off += n
    return out


def kernel(x, c, w_ada, b_ada, norm1_g, w_in, sinks, ssm_lam_re, ssm_lam_im, ssm_log_step, ssm_b_re, ssm_b_im, ssm_c_re, ssm_c_im, ssm_d, w_glu, b_glu, attn_out_g, ssm_out_g, w_out, norm2_g, w_ff1, w_ff2, final_g, loss_target, m_w_ada, m_b_ada, m_norm1_g, m_w_in, m_sinks, m_ssm_lam_re, m_ssm_lam_im, m_ssm_log_step, m_ssm_b_re, m_ssm_b_im, m_ssm_c_re, m_ssm_c_im, m_ssm_d, m_w_glu, m_b_glu, m_attn_out_g, m_ssm_out_g, m_w_out, m_norm2_g, m_w_ff1, m_w_ff2, m_final_g, v_w_ada, v_b_ada, v_norm1_g, v_w_in, v_sinks, v_ssm_lam_re, v_ssm_lam_im, v_ssm_log_step, v_ssm_b_re, v_ssm_b_im, v_ssm_c_re, v_ssm_c_im, v_ssm_d, v_w_glu, v_b_glu, v_attn_out_g, v_ssm_out_g, v_w_out, v_norm2_g, v_w_ff1, v_w_ff2, v_final_g):
    t, d = x.shape[1], x.shape[2]
    d_attn, d_ssm = attn_out_g.shape[1], ssm_d.shape[1]
    d_in = w_in.shape[2] * N_CHIPS
    d_kv = (d_in - d_attn - d_ssm) // 2
    n_q, n_kv = d_attn // HEAD_DIM, d_kv // HEAD_DIM
    n_grp = ssm_lam_re.shape[1]
    assert n_q == n_kv * Q_PER_KV and t % WINDOW == 0 and d_ssm == n_grp * SSM_GROUP
    assert d_kv % LANES == 0 and d_attn % d_kv == 0 and n_grp % GROUPS_PER_BLOCK == 0
    cw = GROUPS_PER_BLOCK * SSM_GROUP
    ucb0 = (d_attn + 2 * d_kv) // cw
    assert (d_attn + 2 * d_kv) % cw == 0
    xi, yi, ci = lax.axis_index("x"), lax.axis_index("y"), lax.axis_index("c")
    chip = 2 * xi + yi
    dev = 2 * chip + ci
    xs, tgt = x[0], loss_target[0]
    vec = lambda a: a.reshape(1, -1)

    n_ada = w_ada.shape[2]
    c_all = _allgather8(c.reshape(SUBLANES, d // SUBLANES), "gather_c").reshape(N_DEV, d)
    b_sh = lax.dynamic_slice_in_dim(b_ada, chip * n_ada, n_ada, axis=1)
    own = {}
    own["w_in"], _ = _cast_place(w_in[0], "cast_w_in")
    (mod_sh, c_act), (win_b,) = _ada_fwd(c_all, w_ada[0], b_sh, _comm_gather_ici([own["w_in"]], [(0, 1, 4)]))
    mod_all = _allgather8(mod_sh, "gather_mod")
    mod_me = lax.dynamic_index_in_dim(mod_all[0::2], dev, axis=1, keepdims=False)
    mod_me = mod_me.reshape(N_CHIPS * n_ada // d, 1, d)
    shift1, scale1, gate1, shift2, scale2, gate2 = [mod_me[i] for i in range(N_MOD)]

    own["w_glu"], _ = _cast_place(w_glu[0], "cast_w_glu")
    own["w_ff1"], (win_b,) = _cast_place(w_ff1[0], "cast_w_ff1",
                                         _comm_gather_ici([win_b], [(1, 1, 4)], [(0, 1, 4)]))
    own["w_ff2"], (win_b,) = _cast_place(w_ff2[0], "cast_w_ff2",
                                         _comm_gather_ici([win_b], [(2, 1, 4)], [(1, 1, 4)]))
    half_view = lambda g, w: g.reshape(N_CHIPS, 2, w.shape[0] // 2, w.shape[1])

    g3 = lambda a: a.reshape(n_grp, 1, STATE)
    lr3, li3 = g3(ssm_lam_re[0]), g3(ssm_lam_im[0])
    ls3 = jnp.broadcast_to(ssm_log_step[0].reshape(n_grp, 1, 1), (n_grp, 1, STATE))
    b_re3, b_im3 = ssm_b_re[0].transpose(0, 2, 1), ssm_b_im[0].transpose(0, 2, 1)
    a_re, a_im, bb_re, bb_im = _ssm_param_fwd(lr3, li3, ls3, b_re3, b_im3)
    ngb = n_grp // GROUPS_PER_BLOCK
    a_cat = jnp.concatenate([a_re.reshape(ngb, 1, -1), a_im.reshape(ngb, 1, -1)], axis=-1)
    bexp = jnp.concatenate([_block_diag_in(bb_re.astype(BF16)), _block_diag_in(bb_im.astype(BF16))], axis=-1)
    cexp = jnp.concatenate([_block_diag_in(ssm_c_re[0].astype(BF16)), _block_diag_in(-ssm_c_im[0].astype(BF16))],
                           axis=-1).transpose(0, 2, 1)

    half = HEAD_DIM // 2
    inv_freq = ROPE_THETA ** (-jnp.arange(half, dtype=F32) / half)
    ang = jnp.arange(t, dtype=F32)[:, None] * inv_freq[None, :]
    cos = jnp.tile(jnp.cos(ang), (1, LANES // half))
    sin = jnp.tile(jnp.sin(ang), (1, LANES // half))

    h, (win_b,) = _norm_mod(xs, norm1_g, scale1, shift1, _comm_gather_ici([win_b], [(3, 1, 4)], [(2, 1, 4)]))
    own["w_out"], (win_s,) = _cast_place(w_out[0], "cast_w_out", _comm_gather_ici([win_b], [None], [(3, 1, 4)]))
    proj, (wglu_ici, wout_ici) = _matmul(h, win_s, "nn", "mm_in", [F32], b_stacked=True, tn=win_s.shape[2],
                                         comm=_comm_gather_ici([own["w_glu"], own["w_out"]]))
    qr, kr, vb = _rope_fwd(proj, cos, sin, d_attn, d_kv)
    heads = lambda a, n: a.reshape(t, n, HEAD_DIM).transpose(1, 0, 2)
    unheads = lambda a: a.transpose(1, 0, 2).reshape(t, -1)
    qh, kh, vh = heads(qr, n_q), heads(kr, n_kv), heads(vb, n_kv)
    whole = (0, 1, 1)
    oh, (wff1_ici, wglu_s, wout_s) = _attn_fwd(
        qh, kh, vh, sinks[0],
        comm=_comm_gather_ici([own["w_ff1"], wglu_ici, wout_ici], [(0, 2, 4), None, None], [None, whole, whole]))
    attn = unheads(oh)
    (y0, z, states), (wff1_ici, wff2_ici) = _ssm_fwd(
        proj, ucb0, bexp, cexp, a_cat, ssm_d,
        comm=_comm_gather_ici([wff1_ici, own["w_ff2"]], [(2, 2, 4), (0, 1, 8)], [(0, 2, 4), None]))
    wglu = wglu_s.reshape(d_ssm, d_ssm)
    wout = wout_s.reshape(d_attn + d_ssm, d)
    gl = _matmul(z, wglu, "nn", "mm_glu", [F32])
    mixed = _mix(attn, y0, gl, b_glu, attn_out_g, ssm_out_g)
    mo, (wff2_ici, wff1_s) = _matmul(
        mixed, wout, "nn", "mm_out", [F32],
        comm=_comm_gather_ici([wff2_ici, wff1_ici], [(1, 1, 8), None], [(0, 1, 8), (2, 2, 4)]))
    x1, h2 = _res_norm_mod(xs, mo, gate1, norm2_g, scale2, shift2)

    def relu2(acc):
        r = jnp.maximum(acc, 0.0)
        return acc, r * r

    (a_act, rr), (wff2_ici,) = _matmul(h2, wff1_s, "nn", "mm_ff1", [BF16, BF16], epilogue=relu2, b_stacked=True,
                                       comm=_comm_gather_ici([wff2_ici], [(2, 6, 8)], [(1, 1, 8)]))
    wff2 = _forward_halves([wff2_ici], "forward_halves_ff2", [(2, 6, 8)])[0].reshape(-1, d)
    ff = _matmul(rr, wff2, "nn", "mm_ff2", [F32])
    dx2, dff, loss_cols, dgf, dgate2 = _final(x1, ff, tgt, gate2, vec(final_g))
    loss = lax.psum(0.5 * jnp.sum(loss_cols) / d, ("x", "y", "c"))

    d_relu2 = lambda acc, av: (acc * 2.0 * jnp.maximum(av.astype(F32), 0.0),)
    v_ff2 = half_view(_matmul(rr, dff, "tn", "mm_gw_ff2", [BF16]), w_ff2[0])
    da, (x_ff2,) = _matmul(dff, wff2, "nt", "mm_dff2", [BF16], epilogue=d_relu2, extras=(a_act,),
                           comm=_comm_pair_exchange([v_ff2]))
    p_ff2 = _pair_sum(v_ff2, x_ff2, ci, "pair_sum_ff2")
    gw_ff1, (r_ff2,) = _matmul(h2, da, "tn", "mm_gw_ff1", [BF16], out_stacked=N_CHIPS,
                               comm=_comm_chip_exchange([p_ff2], [(0, 2, 4)]))
    v_ff1 = half_view(gw_ff1, w_ff1[0])
    dh2, (r_ff2, x_ff1) = _matmul(
        da, wff1_s, "nt", "mm_dff1", [F32], b_stacked=True,
        comm=_join(_comm_chip_exchange([p_ff2], [(2, 2, 4)], [r_ff2]), _comm_pair_exchange([v_ff1])))
    p_ff1 = _pair_sum(v_ff1, x_ff1, ci, "pair_sum_ff1")
    dx1, dmo, dshift2, dscale2, dg2, dgate1 = _bwd_norm2(x1, dh2, dx2, mo, norm2_g, scale2, gate1)
    dmixed, (r_ff1,) = _matmul(dmo, wout, "nt", "mm_dout", [F32], comm=_comm_chip_exchange([p_ff1], [(0, 1, 8)]))
    gw_out, (r_ff1,) = _matmul(mixed, dmo, "tn", "mm_gw_out", [BF16],
                               comm=_comm_chip_exchange([p_ff1], [(1, 1, 8)], [r_ff1]))
    dattn, dgl, dzp, dga, dgs, dbglu = _bwd_mix(dmixed, attn, y0, gl, b_glu, attn_out_g, ssm_out_g)
    d_gelu = lambda acc, dz, yv: ((acc + dz) * _gelu_grad(yv),)
    dy0 = _matmul(dgl, wglu, "nt", "mm_dglu", [F32], epilogue=d_gelu, extras=(dzp, y0))
    gw_glu = _matmul(z, dgl, "tn", "mm_gw_glu", [BF16])
    v_mix = [half_view(gw_glu, w_glu[0]), half_view(gw_out, w_out[0])]
    p_mix = [_pair_sum(v, p, ci, "pair_sum_" + n)
             for n, v, p in zip(("glu", "out"), v_mix, _pair_exchange(v_mix, "pair_exchange_mix"))]
    (du, dbexp, dcexp, da_bar, dd), (r_ff1, *r_mix) = _ssm_bwd(
        dy0, proj, ucb0, states, bexp, cexp, a_cat, ssm_d,
        comm=_join(_comm_chip_exchange([p_ff1], [(2, 2, 8)], [r_ff1]), _comm_chip_exchange(p_mix)))
    doh = heads(dattn, n_q)
    (dqh, dkc, dkp, dvc, dvp, dsink), (r_ff1,) = _attn_bwd(
        qh, kh, vh, oh, doh, sinks[0], comm=_comm_chip_exchange([p_ff1], [(4, 4, 8)], [r_ff1]))
    up = lambda a: jnp.concatenate([unheads(a)[WINDOW:], jnp.zeros((WINDOW, d_kv), F32)], axis=0)
    dq, dk, dv = _rope_bwd(unheads(dqh), unheads(dkc), up(dkp), unheads(dvc), up(dvp), cos, sin)
    dproj = jnp.concatenate([dq, dk, dv, du], axis=1)
    v_in = half_view(_matmul(h, dproj, "tn", "mm_gw_in", [BF16], out_stacked=N_CHIPS, tn=win_s.shape[2]), w_in[0])
    p_in = _pair_sum(v_in, _pair_exchange([v_in], "pair_exchange_in")[0], ci, "pair_sum_in")
    dh, (r_in,) = _matmul(dproj, win_s, "nt", "mm_din", [F32], b_stacked=True, tk=win_s.shape[2],
                          comm=_comm_chip_exchange([p_in]))
    grad_x, dshift1, dscale1, dg1 = _bwd_norm1(xs, dh, dx1, norm1_g, scale1)

    half_l = GROUPS_PER_BLOCK * STATE
    ga_re = da_bar[:, 0, :half_l].reshape(n_grp, 1, STATE)
    ga_im = da_bar[:, 0, half_l:].reshape(n_grp, 1, STATE)
    gbb_re = _block_diag_take(dbexp[:, :, :half_l], SSM_GROUP, STATE)
    gbb_im = _block_diag_take(dbexp[:, :, half_l:], SSM_GROUP, STATE)
    dcexp_t = dcexp.transpose(0, 2, 1)
    gc_re = _block_diag_take(dcexp_t[:, :, :half_l], SSM_GROUP, STATE)
    gc_im = -_block_diag_take(dcexp_t[:, :, half_l:], SSM_GROUP, STATE)
    dmod = jnp.concatenate([dshift1, dscale1, dgate1, dshift2, dscale2, dgate2], axis=1)
    dsinks = dsink.reshape(n_q, WINDOW).sum(axis=1)
    pieces = [dmod, dg1, dsinks, ga_re, ga_im, gbb_re, gbb_im, gc_re, gc_im, dd, dbglu, dga, dgs, dg2, dgf]
    gathered = _allgather8(_pack(pieces).astype(BF16), "gather_small")
    summed = _sum_leading(gathered, F32, "sum_small")
    (g_b_ada, g_norm1, g_sinks, ga_re, ga_im, gbb_re, gbb_im, g_c_re, g_c_im, g_d, g_b_glu, g_attn_g, g_ssm_g,
     g_norm2, g_final) = _unpack(summed, [p.shape for p in pieces])
    g_lr, g_li, g_ls, g_b_re3, g_b_im3 = _ssm_param_bwd(lr3, li3, ls3, b_re3, b_im3, ga_re, ga_im, gbb_re, gbb_im)
    small_grads = [
        g_b_ada, g_norm1, g_sinks.reshape(1, -1), g_lr.reshape(1, n_grp, STATE), g_li.reshape(1, n_grp, STATE),
        g_ls[:, 0, 0].reshape(1, n_grp), g_b_re3.transpose(0, 2, 1)[None], g_b_im3.transpose(0, 2, 1)[None],
        g_c_re[None], g_c_im[None], g_d, g_b_glu, g_attn_g, g_ssm_g, g_norm2, g_final.reshape(-1)]
    small_w = [b_ada, norm1_g, sinks, ssm_lam_re, ssm_lam_im, ssm_log_step, ssm_b_re, ssm_b_im, ssm_c_re,
               ssm_c_im, ssm_d, b_glu, attn_out_g, ssm_out_g, norm2_g, final_g]
    small_m = [m_b_ada, m_norm1_g, m_sinks, m_ssm_lam_re, m_ssm_lam_im, m_ssm_log_step, m_ssm_b_re, m_ssm_b_im,
               m_ssm_c_re, m_ssm_c_im, m_ssm_d, m_b_glu, m_attn_out_g, m_ssm_out_g, m_norm2_g, m_final_g]
    small_v = [v_b_ada, v_norm1_g, v_sinks, v_ssm_lam_re, v_ssm_lam_im, v_ssm_log_step, v_ssm_b_re, v_ssm_b_im,
               v_ssm_c_re, v_ssm_c_im, v_ssm_d, v_b_glu, v_attn_out_g, v_ssm_out_g, v_norm2_g, v_final_g]
    small_grads = [g.reshape(w.shape) for g, w in zip(small_grads, small_w)]
    _, s_delta, s_m, s_v = _adamw(_pack(small_w), _pack(small_grads), _pack(small_m), _pack(small_v), "adamw_small")
    shapes = [w.shape for w in small_w]
    s_delta, s_m, s_v = _unpack(s_delta, shapes), _unpack(s_m, shapes), _unpack(s_v, shapes)

    dmod_rows = gathered.reshape(N_DEV, -1)[:, :dmod.shape[1]]
    dmod_sh = lax.dynamic_slice_in_dim(dmod_rows, chip * n_ada, n_ada, axis=1).astype(F32)
    upd_ada = _adamw(w_ada[0], None, m_w_ada[0], v_w_ada[0], "adamw_w_ada", outer=(c_act.T, dmod_sh))

    big_w = [w_in[0], w_glu[0], w_out[0], w_ff1[0], w_ff2[0]]
    psums = [p_in, *p_mix, p_ff1, p_ff2]
    recvd = [r_in, *r_mix, r_ff1, r_ff2]
    halves = [_chip_sum(p, r, chip, ci, f"chip_sum_{i}") for i, (p, r) in enumerate(zip(psums, recvd))]
    big_grads = [s.reshape(w.shape) for s, w in zip(_share_halves(halves), big_w)]

    big_names = ["w_in", "w_glu", "w_out", "w_ff1", "w_ff2"]
    big_m = [m_w_in[0], m_w_glu[0], m_w_out[0], m_w_ff1[0], m_w_ff2[0]]
    big_v = [v_w_in[0], v_w_glu[0], v_w_out[0], v_w_ff1[0], v_w_ff2[0]]
    big_upd = {n: _adamw(w, g, m, v, "adamw_" + n) for n, w, g, m, v in zip(big_names, big_w, big_grads, big_m, big_v)}
    big_upd["w_ada"] = upd_ada
    big_names = ["w_ada"] + big_names

    order = ["w_ada", "b_ada", "norm1_g", "w_in", "sinks", "ssm_lam_re", "ssm_lam_im", "ssm_log_step", "ssm_b_re",
             "ssm_b_im", "ssm_c_re", "ssm_c_im", "ssm_d", "w_glu", "b_glu", "attn_out_g", "ssm_out_g", "w_out",
             "norm2_g", "w_ff1", "w_ff2", "final_g"]
    small_names = [n for n in order if n not in big_names]
    grads, deltas, new_m, new_v = {}, {}, {}, {}
    for i, n in enumerate(small_names):
        grads[n], deltas[n], new_m[n], new_v[n] = small_grads[i], s_delta[i], s_m[i], s_v[i]
    for n in big_names:
        grads[n], deltas[n], new_m[n], new_v[n] = [a[None] for a in big_upd[n]]
    return (loss, grad_x[None], *[grads[n] for n in order], *[deltas[n] for n in order],
            *[new_m[n] for n in order], *[new_v[n] for n in order])
```
